```python
import jax, jax.numpy as jnp
from jax import lax
import numpy as np

D_MODEL = 1024
BATCH = 2
SEQ = 8192
DEPTH = 1

CHUNK = 64
Q_BLOCK = 128
CONV_DIM = D_MODEL // 2
CONV_WIDTH = 3
N_HEADS = 8
NOPE_DIM = 64
ROPE_DIM = 32
V_DIM = 64
Q_LORA = 384
KV_LORA = 256
ROPE_THETA = 10000.0
D_FF = 2816
PLE_DIM = 256
EPS = 1e-6

QK_DIM = NOPE_DIM + ROPE_DIM
SPLIT_SIZES = (CONV_DIM, CONV_DIM, CONV_DIM, Q_LORA, KV_LORA, ROPE_DIM, D_MODEL, D_MODEL)
N_IN = CONV_DIM * 3 + Q_LORA + KV_LORA + ROPE_DIM + 2 * D_MODEL

kernel_name = "hybrid_conv_mla_gated_stream_block"


def _rmsnorm(x, g):
    xf = x.astype(jnp.float32)
    y = xf * lax.rsqrt(jnp.mean(xf * xf, axis=-1, keepdims=True) + EPS)
    return (y * g.astype(jnp.float32)).astype(x.dtype)


def _causal_dwconv3(x, w):
    s = x.shape[1]
    xp = jnp.pad(x, ((0, 0), (CONV_WIDTH - 1, 0), (0, 0)))
    return xp[:, 0:s] * w[0] + xp[:, 1:s + 1] * w[1] + xp[:, 2:s + 2] * w[2]


def _rope(x, positions):
    half = ROPE_DIM // 2
    inv = 1.0 / (ROPE_THETA ** (jnp.arange(half, dtype=jnp.float32) * (2.0 / ROPE_DIM)))
    ang = positions.astype(jnp.float32)[..., None] * inv
    cos = jnp.cos(ang)[:, :, None, :]
    sin = jnp.sin(ang)[:, :, None, :]
    xf = x.astype(jnp.float32)
    x1, x2 = xf[..., :half], xf[..., half:]
    return jnp.concatenate([x1 * cos - x2 * sin, x2 * cos + x1 * sin], axis=-1).astype(x.dtype)


def _chunk_causal_attention(q, k, v):
    bsz, s, h, dqk = q.shape
    nblk = s // Q_BLOCK
    scale = dqk ** -0.5
    k_chunk = jnp.arange(s) // CHUNK
    qb = q.reshape(bsz, nblk, Q_BLOCK, h, dqk).transpose(1, 0, 2, 3, 4)
    kf = k.astype(jnp.float32)
    vf = v.astype(jnp.float32)
    neg = jnp.finfo(jnp.float32).min

    def one_block(args):
        qi, bi = args
        q_chunk = (bi * Q_BLOCK + jnp.arange(Q_BLOCK)) // CHUNK
        mask = k_chunk[None, :] <= q_chunk[:, None]
        sc = jnp.einsum('bqhd,bkhd->bhqk', qi.astype(jnp.float32), kf) * scale
        sc = jnp.where(mask[None, None], sc, neg)
        pr = jax.nn.softmax(sc, axis=-1)
        return jnp.einsum('bhqk,bkhd->bqhd', pr, vf)

    out = lax.map(one_block, (qb, jnp.arange(nblk)))
    return out.transpose(1, 0, 2, 3, 4).reshape(bsz, s, h, v.shape[-1]).astype(v.dtype)


def setup_inputs(seed: int = 0) -> dict:
    key = jax.random.key(seed)
    ks = jax.random.split(key, 32)
    L = DEPTH
    f32 = jnp.float32

    def w(k, shape, fan_in):
        return jax.random.normal(k, shape, f32) * (fan_in ** -0.5)

    def gain(k, shape):
        return 1.0 + 0.05 * jax.random.normal(k, shape, f32)

    x = jax.random.normal(ks[0], (BATCH, SEQ, D_MODEL), f32)
    p = jax.random.normal(ks[1], (DEPTH, BATCH, SEQ, PLE_DIM), f32)
    offset = jax.random.randint(ks[2], (BATCH, 1), 0, 4096, dtype=jnp.int32)
    positions = offset + jnp.arange(SEQ, dtype=jnp.int32)[None, :]
    return {
        "x": x,
        "p": p,
        "positions": positions,
        "g_mix_pre": gain(ks[3], (L, D_MODEL)),
        "w_in": w(ks[4], (L, D_MODEL, N_IN), D_MODEL),
        "conv_a_w": w(ks[5], (L, CONV_WIDTH, CONV_DIM), CONV_WIDTH),
        "w_a_out": w(ks[6], (L, CONV_DIM, D_MODEL), CONV_DIM),
        "g_q_lat": gain(ks[7], (L, Q_LORA)),
        "w_q_up": w(ks[8], (L, Q_LORA, N_HEADS * QK_DIM), Q_LORA),
        "g_kv_lat": gain(ks[9], (L, KV_LORA)),
        "w_kv_up": w(ks[10], (L, KV_LORA, N_HEADS * (NOPE_DIM + V_DIM)), KV_LORA),
        "w_b_out": w(ks[11], (L, N_HEADS * V_DIM, D_MODEL), N_HEADS * V_DIM),
        "w_o": w(ks[12], (L, D_MODEL, D_MODEL), D_MODEL),
        "g_mix_post": gain(ks[13], (L, D_MODEL)),
        "g_ffn_pre": gain(ks[14], (L, D_MODEL)),
        "w_ffn_up": w(ks[15], (L, D_MODEL, 2 * D_FF), D_MODEL),
        "conv_ffn_w": w(ks[16], (L, CONV_WIDTH, 2 * D_FF), CONV_WIDTH),
        "b_ffn_conv": 0.01 * jax.random.normal(ks[17], (L, 2 * D_FF), f32),
        "w_ffn_down": w(ks[18], (L, D_FF, D_MODEL), D_FF),
        "g_ffn_post": gain(ks[19], (L, D_MODEL)),
        "w_ple_proj": w(ks[20], (L, PLE_DIM, D_MODEL), PLE_DIM),
        "w_ple_gate": w(ks[21], (L, D_MODEL, D_MODEL), D_MODEL),
        "g_ple_post": gain(ks[22], (L, D_MODEL)),
    }


def reference(x, p, positions, g_mix_pre, w_in, conv_a_w, w_a_out, g_q_lat, w_q_up,
              g_kv_lat, w_kv_up, w_b_out, w_o, g_mix_post, g_ffn_pre, w_ffn_up,
              conv_ffn_w, b_ffn_conv, w_ffn_down, g_ffn_post, w_ple_proj, w_ple_gate,
              g_ple_post):
    bsz, s, _ = x.shape
    split_pts = list(np.cumsum(SPLIT_SIZES)[:-1])
    h = x
    for i in range(DEPTH):
        u = _rmsnorm(h, g_mix_pre[i])
        z = u @ w_in[i]
        a_b, a_c, a_x, q_lat, kv_lat, k_rope, gate_a, gate_b = jnp.split(z, split_pts, axis=-1)

        ya = a_b * _causal_dwconv3(a_c * a_x, conv_a_w[i])
        ya = ya @ w_a_out[i]

        q = (_rmsnorm(q_lat, g_q_lat[i]) @ w_q_up[i]).reshape(bsz, s, N_HEADS, QK_DIM)
        kv = (_rmsnorm(kv_lat, g_kv_lat[i]) @ w_kv_up[i]).reshape(bsz, s, N_HEADS, NOPE_DIM + V_DIM)
        q_nope, q_pe = q[..., :NOPE_DIM], q[..., NOPE_DIM:]
        k_nope, v = kv[..., :NOPE_DIM], kv[..., NOPE_DIM:]
        q_pe = _rope(q_pe, positions)
        k_pe = _rope(k_rope[:, :, None, :], positions)
        q_full = jnp.concatenate([q_nope, q_pe], axis=-1)
        k_full = jnp.concatenate(
            [k_nope, jnp.broadcast_to(k_pe, (bsz, s, N_HEADS, ROPE_DIM))], axis=-1)
        ob = _chunk_causal_attention(q_full, k_full, v).reshape(bsz, s, N_HEADS * V_DIM)
        yb = ob @ w_b_out[i]

        mixed = jax.nn.sigmoid(gate_a) * ya + jax.nn.sigmoid(gate_b) * yb
        h = h + _rmsnorm(mixed @ w_o[i], g_mix_post[i])

        u2 = _rmsnorm(h, g_ffn_pre[i])
        up = _causal_dwconv3(u2 @ w_ffn_up[i], conv_ffn_w[i]) + b_ffn_conv[i]
        f_gate, f_val = up[..., :D_FF], up[..., D_FF:]
        f = (jax.nn.gelu(f_gate, approximate=True) * f_val) @ w_ffn_down[i]
        h = h + _rmsnorm(f, g_ffn_post[i])

        e = p[i] @ w_ple_proj[i]
        g = jax.nn.sigmoid(h @ w_ple_gate[i])
        h = h + _rmsnorm(e * g, g_ple_post[i])
    return h
```

```python
import functools
import math

import jax
import jax.numpy as jnp
from jax.experimental import pallas as pl
from jax.experimental.pallas import tpu as pltpu

D_MODEL = 1024
CHUNK = 64
CONV_DIM = 512
N_HEADS = 8
NOPE_DIM = 64
ROPE_DIM = 32
V_DIM = 64
Q_LORA = 384
KV_LORA = 256
ROPE_THETA = 10000.0
D_FF = 2816
PLE_DIM = 256
EPS = 1e-6
QK_DIM = NOPE_DIM + ROPE_DIM

LANES = 128
SUBLANES = 8
HEAD_SLOT = LANES
HP = N_HEADS * HEAD_SLOT
LAT_COLS = Q_LORA + KV_LORA + 2 * LANES
VMEM_LIMIT = 56 * 1024 * 1024

TS_MIX = 512
TQ = 512
TK = 512
TS_OUT = 512
TS_FFN = 512
TS_PLE = 512
FFN_CHUNKS = ((0, 1024), (1024, 1024), (2048, 768))

NEG_BIG = -1e30
BF16 = jnp.bfloat16
F32 = jnp.float32


def _rms(x, g):
    return x * jax.lax.rsqrt(jnp.mean(x * x, axis=-1, keepdims=True) + EPS) * g


def _dot(a, b):
    return jnp.dot(a, b, preferred_element_type=F32)


def _const_spec(shape):
    return pl.BlockSpec(shape, lambda *_: (0,) * len(shape))


def _shifted_rows(stage_ref, carry_ref, pre, rows, width):
    stage_ref[0:SUBLANES, 0:width] = carry_ref[...]
    stage_ref[SUBLANES:SUBLANES + rows, 0:width] = pre
    x1 = stage_ref[SUBLANES - 1:SUBLANES - 1 + rows, 0:width]
    x2 = stage_ref[SUBLANES - 2:SUBLANES - 2 + rows, 0:width]
    carry_ref[...] = pre[rows - SUBLANES:rows, :]
    return x1, x2


def _mix_in_kernel(x_ref, pos_ref, invf_ref, g_pre_ref, w_abc_ref, w_lat_ref,
                   w_ga_ref, w_gb_ref, conv_w_ref, w_a_out_ref, g_q_ref, w_q_ref,
                   g_kv_ref, w_k_ref, w_v_ref, vone_ref,
                   q_ref, k_ref, v_ref, gya_ref, sgb_ref,
                   carry_ref, stage_ref, *, q_scale):
    ts = x_ref.shape[0]

    @pl.when(pl.program_id(1) == 0)
    def _():
        carry_ref[...] = jnp.zeros_like(carry_ref)

    u = _rms(x_ref[...], g_pre_ref[...]).astype(BF16)

    abc = _dot(u, w_abc_ref[...])
    a_b = abc[:, 0:CONV_DIM]
    ca = abc[:, CONV_DIM:2 * CONV_DIM] * abc[:, 2 * CONV_DIM:3 * CONV_DIM]
    x1, x2 = _shifted_rows(stage_ref, carry_ref, ca, ts, CONV_DIM)
    cw = conv_w_ref[...]
    cv = x2 * cw[0:1, :] + x1 * cw[1:2, :] + ca * cw[2:3, :]
    ya = _dot((a_b * cv).astype(BF16), w_a_out_ref[...])
    gya_ref[...] = (jax.nn.sigmoid(_dot(u, w_ga_ref[...])) * ya).astype(BF16)
    sgb_ref[...] = jax.nn.sigmoid(_dot(u, w_gb_ref[...])).astype(BF16)

    lat = _dot(u, w_lat_ref[...])
    q_lat = lat[:, 0:Q_LORA]
    kv_lat = lat[:, Q_LORA:Q_LORA + KV_LORA]
    kr = lat[:, Q_LORA + KV_LORA:Q_LORA + KV_LORA + LANES]
    kr_rot = lat[:, Q_LORA + KV_LORA + LANES:LAT_COLS]

    ang = pos_ref[...].astype(F32) * invf_ref[...]
    cos = jnp.cos(ang)
    sin = jnp.sin(ang)

    qn = _rms(q_lat, g_q_ref[...]).astype(BF16)
    qq = _dot(qn, w_q_ref[...])
    kvn = _rms(kv_lat, g_kv_ref[...]).astype(BF16)
    kk = _dot(kvn, w_k_ref[...])
    vv = _dot(kvn, w_v_ref[...]) + vone_ref[...]
    v_ref[...] = vv.astype(BF16)

    k_pe = kr * cos + kr_rot * sin
    cos_q = cos * q_scale
    sin_q = sin * q_scale
    for h in range(N_HEADS):
        sl = slice(h * HEAD_SLOT, (h + 1) * HEAD_SLOT)
        q_ref[:, sl] = (qq[:, sl] * cos_q
                        + qq[:, HP + h * HEAD_SLOT:HP + (h + 1) * HEAD_SLOT] * sin_q).astype(BF16)
        k_ref[:, sl] = (kk[:, sl] + k_pe).astype(BF16)


def _attn_kernel(q_ref, k_ref, v_ref, o_ref):
    tq = q_ref.shape[0]
    qi = pl.program_id(2)
    q = q_ref[...]

    def step(k_tile, v_tile, carry, mask):
        m, acc = carry
        s = jax.lax.dot_general(q, k_tile, (((1,), (1,)), ((), ())),
                                preferred_element_type=F32)
        if mask is not None:
            s = jnp.where(mask, s, NEG_BIG)
        m_new = jnp.maximum(m, jnp.max(s, axis=-1, keepdims=True))
        p = jnp.exp2(s - m_new)
        acc = acc * jnp.exp2(m - m_new) + _dot(p.astype(BF16), v_tile)
        return m_new, acc

    def full_tile(j, carry):
        off = pl.multiple_of(j * TK, TK)
        return step(k_ref[pl.ds(off, TK), :], v_ref[pl.ds(off, TK), :], carry, None)

    init = (jnp.full((tq, 1), NEG_BIG, F32), jnp.zeros((tq, HEAD_SLOT), F32))
    carry = jax.lax.fori_loop(0, qi * (tq // TK), full_tile, init)

    off = pl.multiple_of(qi * tq, tq)
    rq = jax.lax.broadcasted_iota(jnp.int32, (tq, tq), 0) // CHUNK
    ck = jax.lax.broadcasted_iota(jnp.int32, (tq, tq), 1) // CHUNK
    _, acc = step(k_ref[pl.ds(off, tq), :], v_ref[pl.ds(off, tq), :], carry, ck <= rq)

    o_ref[...] = (acc / acc[:, V_DIM:V_DIM + 1]).astype(BF16)


def _mix_out_kernel(x_ref, o_ref, gya_ref, sgb_ref, w_b_ref, w_o_ref, g_ref, h_ref):
    yb = _dot(o_ref[...], w_b_ref[...])
    mixed = gya_ref[...].astype(F32) + sgb_ref[...].astype(F32) * yb
    mo = _dot(mixed.astype(BF16), w_o_ref[...])
    h_ref[...] = x_ref[...] + _rms(mo, g_ref[...])


def _ffn_kernel(h_ref, g_pre_ref, w_up_ref, conv_w_ref, b_ref, w_down_ref, g_post_ref,
                out_ref, carry_ref, stage_ref):
    ts = h_ref.shape[0]

    @pl.when(pl.program_id(1) == 0)
    def _():
        carry_ref[...] = jnp.zeros_like(carry_ref)

    h = h_ref[...]
    u = _rms(h, g_pre_ref[...]).astype(BF16)

    def conv_cols(col, width):
        cols = slice(col, col + width)
        pre = _dot(u, w_up_ref[:, cols])
        x1, x2 = _shifted_rows(stage_ref, carry_ref.at[:, cols], pre, ts, width)
        cw = conv_w_ref[:, cols]
        return x2 * cw[0:1, :] + x1 * cw[1:2, :] + pre * cw[2:3, :] + b_ref[:, cols]

    f = jnp.zeros((ts, D_MODEL), F32)
    for off, width in FFN_CHUNKS:
        act = jax.nn.gelu(conv_cols(off, width), approximate=True)
        val = conv_cols(D_FF + off, width)
        f = f + _dot((act * val).astype(BF16), w_down_ref[off:off + width, :])
    out_ref[...] = h + _rms(f, g_post_ref[...])


def _ple_kernel(h_ref, p_ref, w_proj_ref, w_gate_ref, g_ref, out_ref):
    h = h_ref[...]
    e = _dot(p_ref[...].astype(BF16), w_proj_ref[...])
    g = jax.nn.sigmoid(_dot(h.astype(BF16), w_gate_ref[...]))
    out_ref[...] = h + _rms(e * g, g_ref[...])


def _row_spec(ts, width):
    return pl.BlockSpec((None, ts, width), lambda b, s: (b, s, 0))


def _params(sem):
    return pltpu.CompilerParams(dimension_semantics=sem, vmem_limit_bytes=VMEM_LIMIT)


def _layer(x, p, positions, g_mix_pre, w_in, conv_a_w, w_a_out, g_q_lat, w_q_up, g_kv_lat,
           w_kv_up, w_b_out, w_o, g_mix_post, g_ffn_pre, w_ffn_up, conv_ffn_w, b_ffn_conv,
           w_ffn_down, g_ffn_post, w_ple_proj, w_ple_gate, g_ple_post):
    bsz, seq, _ = x.shape
    row = lambda v: v.reshape(1, -1).astype(F32)

    c0 = 3 * CONV_DIM
    c1 = c0 + Q_LORA
    c2 = c1 + KV_LORA
    c3 = c2 + ROPE_DIM
    c4 = c3 + D_MODEL
    half = ROPE_DIM // 2
    w_kr = w_in[:, c2:c3]
    w_kr_rot = jnp.concatenate([-w_kr[:, half:], w_kr[:, :half]], axis=1)
    pad_slot = lambda w: jnp.pad(w, ((0, 0), (NOPE_DIM, HEAD_SLOT - NOPE_DIM - ROPE_DIM)))
    w_abc = w_in[:, 0:c0].astype(BF16)
    w_lat = jnp.concatenate([w_in[:, c0:c2], pad_slot(w_kr), pad_slot(w_kr_rot)], axis=1).astype(BF16)
    w_ga = w_in[:, c3:c4].astype(BF16)
    w_gb = w_in[:, c4:].astype(BF16)

    wq = w_q_up.reshape(Q_LORA, N_HEADS, QK_DIM)
    wq_plain = jnp.pad(wq, ((0, 0), (0, 0), (0, HEAD_SLOT - QK_DIM)))
    wq_rot = jnp.concatenate([-wq[:, :, NOPE_DIM + half:], wq[:, :, NOPE_DIM:NOPE_DIM + half]], axis=2)
    wq_rot = jnp.pad(wq_rot, ((0, 0), (0, 0), (NOPE_DIM, HEAD_SLOT - QK_DIM)))
    w_q = jnp.concatenate([wq_plain.reshape(Q_LORA, HP), wq_rot.reshape(Q_LORA, HP)], axis=1).astype(BF16)

    wkv = w_kv_up.reshape(KV_LORA, N_HEADS, NOPE_DIM + V_DIM)
    w_k = jnp.pad(wkv[:, :, :NOPE_DIM], ((0, 0), (0, 0), (0, HEAD_SLOT - NOPE_DIM))).reshape(KV_LORA, HP).astype(BF16)
    w_v = jnp.pad(wkv[:, :, NOPE_DIM:], ((0, 0), (0, 0), (0, HEAD_SLOT - V_DIM))).reshape(KV_LORA, HP).astype(BF16)
    vone = jnp.tile((jnp.arange(HEAD_SLOT) == V_DIM).astype(F32), N_HEADS).reshape(1, HP)
    w_b = jnp.pad(w_b_out.reshape(N_HEADS, V_DIM, D_MODEL),
                  ((0, 0), (0, HEAD_SLOT - V_DIM), (0, 0))).reshape(HP, D_MODEL).astype(BF16)

    inv = 1.0 / (ROPE_THETA ** (jnp.arange(half, dtype=F32) * (2.0 / ROPE_DIM)))
    invf = jnp.pad(jnp.concatenate([inv, inv]), (NOPE_DIM, HEAD_SLOT - QK_DIM)).reshape(1, HEAD_SLOT)
    pos3 = positions.reshape(bsz, seq, 1)
    q_scale = (QK_DIM ** -0.5) * math.log2(math.e)

    act = jax.ShapeDtypeStruct((bsz, seq, HP), BF16)
    q, k, v, gya, sgb = pl.pallas_call(
        functools.partial(_mix_in_kernel, q_scale=q_scale),
        grid=(bsz, seq // TS_MIX),
        in_specs=[_row_spec(TS_MIX, D_MODEL), _row_spec(TS_MIX, 1),
                  _const_spec((1, HEAD_SLOT)), _const_spec((1, D_MODEL)),
                  _const_spec((D_MODEL, c0)), _const_spec((D_MODEL, LAT_COLS)),
                  _const_spec((D_MODEL, D_MODEL)), _const_spec((D_MODEL, D_MODEL)),
                  _const_spec((3, CONV_DIM)), _const_spec((CONV_DIM, D_MODEL)),
                  _const_spec((1, Q_LORA)), _const_spec((Q_LORA, 2 * HP)),
                  _const_spec((1, KV_LORA)), _const_spec((KV_LORA, HP)),
                  _const_spec((KV_LORA, HP)), _const_spec((1, HP))],
        out_specs=[_row_spec(TS_MIX, HP)] * 5,
        out_shape=[act] * 5,
        scratch_shapes=[pltpu.VMEM((SUBLANES, CONV_DIM), F32),
                        pltpu.VMEM((TS_MIX + SUBLANES, CONV_DIM), F32)],
        compiler_params=_params(("arbitrary", "arbitrary")),
        name="mix_in",
    )(x, pos3, invf, row(g_mix_pre), w_abc, w_lat, w_ga, w_gb, conv_a_w.astype(F32),
      w_a_out.astype(BF16), row(g_q_lat), w_q, row(g_kv_lat), w_k, w_v, vone)

    head_rows = lambda rows: pl.BlockSpec((None, rows, HEAD_SLOT), lambda b, h, i: (b, i, h))
    head_all = pl.BlockSpec((None, seq, HEAD_SLOT), lambda b, h, i: (b, 0, h))
    o = pl.pallas_call(
        _attn_kernel,
        grid=(bsz, N_HEADS, seq // TQ),
        in_specs=[head_rows(TQ), head_all, head_all],
        out_specs=head_rows(TQ),
        out_shape=act,
        compiler_params=_params(("arbitrary", "arbitrary", "arbitrary")),
        name="attn",
    )(q, k, v)

    h1 = pl.pallas_call(
        _mix_out_kernel,
        grid=(bsz, seq // TS_OUT),
        in_specs=[_row_spec(TS_OUT, D_MODEL), _row_spec(TS_OUT, HP), _row_spec(TS_OUT, HP),
                  _row_spec(TS_OUT, HP), _const_spec((HP, D_MODEL)),
                  _const_spec((D_MODEL, D_MODEL)), _const_spec((1, D_MODEL))],
        out_specs=_row_spec(TS_OUT, D_MODEL),
        out_shape=jax.ShapeDtypeStruct(x.shape, F32),
        compiler_params=_params(("arbitrary", "arbitrary")),
        name="mix_out",
    )(x, o, gya, sgb, w_b, w_o.astype(BF16), row(g_mix_post))

    max_w = max(w for _, w in FFN_CHUNKS)
    h2 = pl.pallas_call(
        _ffn_kernel,
        grid=(bsz, seq // TS_FFN),
        in_specs=[_row_spec(TS_FFN, D_MODEL), _const_spec((1, D_MODEL)),
                  _const_spec((D_MODEL, 2 * D_FF)), _const_spec((3, 2 * D_FF)),
                  _const_spec((1, 2 * D_FF)), _const_spec((D_FF, D_MODEL)),
                  _const_spec((1, D_MODEL))],
        out_specs=_row_spec(TS_FFN, D_MODEL),
        out_shape=jax.ShapeDtypeStruct(x.shape, F32),
        scratch_shapes=[pltpu.VMEM((SUBLANES, 2 * D_FF), F32),
                        pltpu.VMEM((TS_FFN + SUBLANES, max_w), F32)],
        compiler_params=_params(("arbitrary", "arbitrary")),
        name="ffn",
    )(h1, row(g_ffn_pre), w_ffn_up.astype(BF16), conv_ffn_w.astype(F32), row(b_ffn_conv),
      w_ffn_down.astype(BF16), row(g_ffn_post))

    h3 = pl.pallas_call(
        _ple_kernel,
        grid=(bsz, seq // TS_PLE),
        in_specs=[_row_spec(TS_PLE, D_MODEL), _row_spec(TS_PLE, PLE_DIM),
                  _const_spec((PLE_DIM, D_MODEL)), _const_spec((D_MODEL, D_MODEL)),
                  _const_spec((1, D_MODEL))],
        out_specs=_row_spec(TS_PLE, D_MODEL),
        out_shape=jax.ShapeDtypeStruct(x.shape, F32),
        compiler_params=_params(("arbitrary", "arbitrary")),
        name="ple",
    )(h2, p, w_ple_proj.astype(BF16), w_ple_gate.astype(BF16), row(g_ple_post))
    return h3


def kernel(x, p, positions, g_mix_pre, w_in, conv_a_w, w_a_out, g_q_lat, w_q_up, g_kv_lat, w_kv_up, w_b_out, w_o, g_mix_post, g_ffn_pre, w_ffn_up, conv_ffn_w, b_ffn_conv, w_ffn_down, g_ffn_post, w_ple_proj, w_ple_gate, g_ple_post):
    h = x
    for i in range(g_mix_pre.shape[0]):
        h = _layer(h, p[i], positions, g_mix_pre[i], w_in[i], conv_a_w[i], w_a_out[i], g_q_lat[i],
                   w_q_up[i], g_kv_lat[i], w_kv_up[i], w_b_out[i], w_o[i], g_mix_post[i],
                   g_ffn_pre[i], w_ffn_up[i], conv_ffn_w[i], b_ffn_conv[i], w_ffn_down[i],
                   g_ffn_post[i], w_ple_proj[i], w_ple_gate[i], g_ple_post[i])
    return h
```

```python
import functools
import math

import jax
import jax.numpy as jnp
from jax.experimental import pallas as pl
from jax.experimental.pallas import tpu as pltpu

D_MODEL = 1024
CHUNK = 64
CONV_DIM = 512
N_HEADS = 8
NOPE_DIM = 64
ROPE_DIM = 32
V_DIM = 64
Q_LORA = 384
KV_LORA = 256
ROPE_THETA = 10000.0
D_FF = 2816
PLE_DIM = 256
EPS = 1e-6
QK_DIM = NOPE_DIM + ROPE_DIM

LANES = 128
SUBLANES = 8
HEAD_SLOT = LANES
HP = N_HEADS * HEAD_SLOT
LAT_COLS = Q_LORA + KV_LORA + 2 * LANES
VMEM_LIMIT = 56 * 1024 * 1024

TS_MIX = 512
TQ = 512
HEADS_PER_STEP = 2
TS_OUT = 512
TS_FFN = 512
TS_PLE = 512
FFN_CHUNKS = ((0, 1024), (1024, 1024), (2048, 768))

NEG_BIG = -1e30
BF16 = jnp.bfloat16
F32 = jnp.float32


def _rms(x, g):
    return x * jax.lax.rsqrt(jnp.mean(x * x, axis=-1, keepdims=True) + EPS) * g


def _dot(a, b):
    return jnp.dot(a, b, preferred_element_type=F32)


def _const_spec(shape):
    return pl.BlockSpec(shape, lambda *_: (0,) * len(shape))


def _shifted_rows(stage_ref, carry_ref, pre, rows, width):
    stage_ref[0:SUBLANES, 0:width] = carry_ref[...]
    stage_ref[SUBLANES:SUBLANES + rows, 0:width] = pre
    x1 = stage_ref[SUBLANES - 1:SUBLANES - 1 + rows, 0:width]
    x2 = stage_ref[SUBLANES - 2:SUBLANES - 2 + rows, 0:width]
    carry_ref[...] = pre[rows - SUBLANES:rows, :]
    return x1, x2


def _mix_in_kernel(x_ref, pos_ref, invf_ref, g_pre_ref, w_abc_ref, w_lat_ref,
                   w_ga_ref, w_gb_ref, conv_w_ref, w_a_out_ref, g_q_ref, w_q_ref,
                   g_kv_ref, w_k_ref, w_v_ref, vone_ref,
                   q_ref, k_ref, v_ref, gya_ref, sgb_ref,
                   carry_ref, stage_ref, *, q_scale):
    ts = x_ref.shape[0]

    @pl.when(pl.program_id(1) == 0)
    def _():
        carry_ref[...] = jnp.zeros_like(carry_ref)

    u = _rms(x_ref[...], g_pre_ref[...]).astype(BF16)

    abc = _dot(u, w_abc_ref[...])
    a_b = abc[:, 0:CONV_DIM]
    ca = abc[:, CONV_DIM:2 * CONV_DIM] * abc[:, 2 * CONV_DIM:3 * CONV_DIM]
    x1, x2 = _shifted_rows(stage_ref, carry_ref, ca, ts, CONV_DIM)
    cw = conv_w_ref[...]
    cv = x2 * cw[0:1, :] + x1 * cw[1:2, :] + ca * cw[2:3, :]
    ya = _dot((a_b * cv).astype(BF16), w_a_out_ref[...])
    gya_ref[...] = (jax.nn.sigmoid(_dot(u, w_ga_ref[...])) * ya).astype(BF16)
    sgb_ref[...] = jax.nn.sigmoid(_dot(u, w_gb_ref[...])).astype(BF16)

    lat = _dot(u, w_lat_ref[...])
    q_lat = lat[:, 0:Q_LORA]
    kv_lat = lat[:, Q_LORA:Q_LORA + KV_LORA]
    kr = lat[:, Q_LORA + KV_LORA:Q_LORA + KV_LORA + LANES]
    kr_rot = lat[:, Q_LORA + KV_LORA + LANES:LAT_COLS]

    ang = pos_ref[...].astype(F32) * invf_ref[...]
    cos = jnp.cos(ang)
    sin = jnp.sin(ang)

    qn = _rms(q_lat, g_q_ref[...]).astype(BF16)
    qq = _dot(qn, w_q_ref[...])
    kvn = _rms(kv_lat, g_kv_ref[...]).astype(BF16)
    kk = _dot(kvn, w_k_ref[...])
    vv = _dot(kvn, w_v_ref[...]) + vone_ref[...]
    v_ref[...] = vv.astype(BF16)

    k_pe = kr * cos + kr_rot * sin
    cos_q = cos * q_scale
    sin_q = sin * q_scale
    for h in range(N_HEADS):
        sl = slice(h * HEAD_SLOT, (h + 1) * HEAD_SLOT)
        q_ref[:, sl] = (qq[:, sl] * cos_q
                        + qq[:, HP + h * HEAD_SLOT:HP + (h + 1) * HEAD_SLOT] * sin_q).astype(BF16)
        k_ref[:, sl] = (kk[:, sl] + k_pe).astype(BF16)


def _attn_kernel(q_ref, k_ref, v_ref, o_ref, s_ref, m_ref, acc_ref):
    tq = q_ref.shape[0]
    qi = pl.program_id(2)
    heads = range(HEADS_PER_STEP)
    lanes = lambda h: slice(h * HEAD_SLOT, (h + 1) * HEAD_SLOT)

    def scores(j, buf):
        off = pl.multiple_of(j * tq, tq)
        for h in heads:
            s_ref[buf, h] = jax.lax.dot_general(
                q_ref[:, lanes(h)], k_ref[pl.ds(off, tq), lanes(h)],
                (((1,), (1,)), ((), ())), preferred_element_type=F32)

    def accumulate(j, buf, mask):
        off = pl.multiple_of(j * tq, tq)
        load = ((lambda h: s_ref[buf, h]) if mask is None
                else (lambda h: jnp.where(mask, s_ref[buf, h], NEG_BIG)))
        for h in heads:
            m = m_ref[h]
            m_new = jnp.maximum(m, jnp.max(load(h), axis=-1, keepdims=True))
            p = jnp.exp2(load(h) - m_new)
            pv = _dot(p.astype(BF16), v_ref[pl.ds(off, tq), lanes(h)])
            acc_ref[h] = acc_ref[h] * jnp.exp2(m - m_new) + pv
            m_ref[h] = m_new

    m_ref[...] = jnp.full_like(m_ref, NEG_BIG)
    acc_ref[...] = jnp.zeros_like(acc_ref)
    scores(0, 0)

    def tile_pair(i, _):
        scores(2 * i + 1, 1)
        accumulate(2 * i, 0, None)
        scores(2 * i + 2, 0)
        accumulate(2 * i + 1, 1, None)
        return 0

    jax.lax.fori_loop(0, qi // 2, tile_pair, 0)

    rq = jax.lax.broadcasted_iota(jnp.int32, (tq, tq), 0) // CHUNK
    ck = jax.lax.broadcasted_iota(jnp.int32, (tq, tq), 1) // CHUNK

    @pl.when(qi % 2 == 0)
    def _():
        accumulate(qi, 0, ck <= rq)

    @pl.when(qi % 2 == 1)
    def _():
        scores(qi, 1)
        accumulate(qi - 1, 0, None)
        accumulate(qi, 1, ck <= rq)

    for h in heads:
        acc = acc_ref[h]
        o_ref[:, lanes(h)] = (acc / acc[:, V_DIM:V_DIM + 1]).astype(BF16)


def _mix_out_kernel(x_ref, o_ref, gya_ref, sgb_ref, w_b_ref, w_o_ref, g_ref, h_ref):
    yb = _dot(o_ref[...], w_b_ref[...])
    mixed = gya_ref[...].astype(F32) + sgb_ref[...].astype(F32) * yb
    mo = _dot(mixed.astype(BF16), w_o_ref[...])
    h_ref[...] = x_ref[...] + _rms(mo, g_ref[...])


def _ffn_kernel(h_ref, g_pre_ref, w_up_ref, conv_w_ref, b_ref, w_down_ref, g_post_ref,
                out_ref, carry_ref, stage_ref):
    ts = h_ref.shape[0]

    @pl.when(pl.program_id(1) == 0)
    def _():
        carry_ref[...] = jnp.zeros_like(carry_ref)

    h = h_ref[...]
    u = _rms(h, g_pre_ref[...]).astype(BF16)

    def conv_cols(col, width):
        cols = slice(col, col + width)
        pre = _dot(u, w_up_ref[:, cols])
        x1, x2 = _shifted_rows(stage_ref, carry_ref.at[:, cols], pre, ts, width)
        cw = conv_w_ref[:, cols]
        return x2 * cw[0:1, :] + x1 * cw[1:2, :] + pre * cw[2:3, :] + b_ref[:, cols]

    f = jnp.zeros((ts, D_MODEL), F32)
    for off, width in FFN_CHUNKS:
        act = jax.nn.gelu(conv_cols(off, width), approximate=True)
        val = conv_cols(D_FF + off, width)
        f = f + _dot((act * val).astype(BF16), w_down_ref[off:off + width, :])
    out_ref[...] = h + _rms(f, g_post_ref[...])


def _ple_kernel(h_ref, p_ref, w_proj_ref, w_gate_ref, g_ref, out_ref):
    h = h_ref[...]
    e = _dot(p_ref[...].astype(BF16), w_proj_ref[...])
    g = jax.nn.sigmoid(_dot(h.astype(BF16), w_gate_ref[...]))
    out_ref[...] = h + _rms(e * g, g_ref[...])


def _row_spec(ts, width):
    return pl.BlockSpec((None, ts, width), lambda b, s: (b, s, 0))


def _params(sem):
    return pltpu.CompilerParams(dimension_semantics=sem, vmem_limit_bytes=VMEM_LIMIT)


def _layer(x, p, positions, g_mix_pre, w_in, conv_a_w, w_a_out, g_q_lat, w_q_up, g_kv_lat,
           w_kv_up, w_b_out, w_o, g_mix_post, g_ffn_pre, w_ffn_up, conv_ffn_w, b_ffn_conv,
           w_ffn_down, g_ffn_post, w_ple_proj, w_ple_gate, g_ple_post):
    bsz, seq, _ = x.shape
    row = lambda v: v.reshape(1, -1).astype(F32)

    c0 = 3 * CONV_DIM
    c1 = c0 + Q_LORA
    c2 = c1 + KV_LORA
    c3 = c2 + ROPE_DIM
    c4 = c3 + D_MODEL
    half = ROPE_DIM // 2
    w_kr = w_in[:, c2:c3]
    w_kr_rot = jnp.concatenate([-w_kr[:, half:], w_kr[:, :half]], axis=1)
    pad_slot = lambda w: jnp.pad(w, ((0, 0), (NOPE_DIM, HEAD_SLOT - NOPE_DIM - ROPE_DIM)))
    w_abc = w_in[:, 0:c0].astype(BF16)
    w_lat = jnp.concatenate([w_in[:, c0:c2], pad_slot(w_kr), pad_slot(w_kr_rot)], axis=1).astype(BF16)
    w_ga = w_in[:, c3:c4].astype(BF16)
    w_gb = w_in[:, c4:].astype(BF16)

    wq = w_q_up.reshape(Q_LORA, N_HEADS, QK_DIM)
    wq_plain = jnp.pad(wq, ((0, 0), (0, 0), (0, HEAD_SLOT - QK_DIM)))
    wq_rot = jnp.concatenate([-wq[:, :, NOPE_DIM + half:], wq[:, :, NOPE_DIM:NOPE_DIM + half]], axis=2)
    wq_rot = jnp.pad(wq_rot, ((0, 0), (0, 0), (NOPE_DIM, HEAD_SLOT - QK_DIM)))
    w_q = jnp.concatenate([wq_plain.reshape(Q_LORA, HP), wq_rot.reshape(Q_LORA, HP)], axis=1).astype(BF16)

    wkv = w_kv_up.reshape(KV_LORA, N_HEADS, NOPE_DIM + V_DIM)
    w_k = jnp.pad(wkv[:, :, :NOPE_DIM], ((0, 0), (0, 0), (0, HEAD_SLOT - NOPE_DIM))).reshape(KV_LORA, HP).astype(BF16)
    w_v = jnp.pad(wkv[:, :, NOPE_DIM:], ((0, 0), (0, 0), (0, HEAD_SLOT - V_DIM))).reshape(KV_LORA, HP).astype(BF16)
    vone = jnp.tile((jnp.arange(HEAD_SLOT) == V_DIM).astype(F32), N_HEADS).reshape(1, HP)
    w_b = jnp.pad(w_b_out.reshape(N_HEADS, V_DIM, D_MODEL),
                  ((0, 0), (0, HEAD_SLOT - V_DIM), (0, 0))).reshape(HP, D_MODEL).astype(BF16)

    inv = 1.0 / (ROPE_THETA ** (jnp.arange(half, dtype=F32) * (2.0 / ROPE_DIM)))
    invf = jnp.pad(jnp.concatenate([inv, inv]), (NOPE_DIM, HEAD_SLOT - QK_DIM)).reshape(1, HEAD_SLOT)
    pos3 = positions.reshape(bsz, seq, 1)
    q_scale = (QK_DIM ** -0.5) * math.log2(math.e)

    act = jax.ShapeDtypeStruct((bsz, seq, HP), BF16)
    q, k, v, gya, sgb = pl.pallas_call(
        functools.partial(_mix_in_kernel, q_scale=q_scale),
        grid=(bsz, seq // TS_MIX),
        in_specs=[_row_spec(TS_MIX, D_MODEL), _row_spec(TS_MIX, 1),
                  _const_spec((1, HEAD_SLOT)), _const_spec((1, D_MODEL)),
                  _const_spec((D_MODEL, c0)), _const_spec((D_MODEL, LAT_COLS)),
                  _const_spec((D_MODEL, D_MODEL)), _const_spec((D_MODEL, D_MODEL)),
                  _const_spec((3, CONV_DIM)), _const_spec((CONV_DIM, D_MODEL)),
                  _const_spec((1, Q_LORA)), _const_spec((Q_LORA, 2 * HP)),
                  _const_spec((1, KV_LORA)), _const_spec((KV_LORA, HP)),
                  _const_spec((KV_LORA, HP)), _const_spec((1, HP))],
        out_specs=[_row_spec(TS_MIX, HP)] * 5,
        out_shape=[act] * 5,
        scratch_shapes=[pltpu.VMEM((SUBLANES, CONV_DIM), F32),
                        pltpu.VMEM((TS_MIX + SUBLANES, CONV_DIM), F32)],
        compiler_params=_params(("arbitrary", "arbitrary")),
        name="mix_in",
    )(x, pos3, invf, row(g_mix_pre), w_abc, w_lat, w_ga, w_gb, conv_a_w.astype(F32),
      w_a_out.astype(BF16), row(g_q_lat), w_q, row(g_kv_lat), w_k, w_v, vone)

    group = HEADS_PER_STEP * HEAD_SLOT
    head_rows = lambda rows: pl.BlockSpec((None, rows, group), lambda b, h, i: (b, i, h))
    head_all = pl.BlockSpec((None, seq, group), lambda b, h, i: (b, 0, h))
    o = pl.pallas_call(
        _attn_kernel,
        grid=(bsz, N_HEADS // HEADS_PER_STEP, seq // TQ),
        in_specs=[head_rows(TQ), head_all, head_all],
        out_specs=head_rows(TQ),
        out_shape=act,
        scratch_shapes=[pltpu.VMEM((2, HEADS_PER_STEP, TQ, TQ), F32),
                        pltpu.VMEM((HEADS_PER_STEP, TQ, 1), F32),
                        pltpu.VMEM((HEADS_PER_STEP, TQ, HEAD_SLOT), F32)],
        compiler_params=_params(("arbitrary", "arbitrary", "arbitrary")),
        name="attn",
    )(q, k, v)

    h1 = pl.pallas_call(
        _mix_out_kernel,
        grid=(bsz, seq // TS_OUT),
        in_specs=[_row_spec(TS_OUT, D_MODEL), _row_spec(TS_OUT, HP), _row_spec(TS_OUT, HP),
                  _row_spec(TS_OUT, HP), _const_spec((HP, D_MODEL)),
                  _const_spec((D_MODEL, D_MODEL)), _const_spec((1, D_MODEL))],
        out_specs=_row_spec(TS_OUT, D_MODEL),
        out_shape=jax.ShapeDtypeStruct(x.shape, F32),
        compiler_params=_params(("arbitrary", "arbitrary")),
        name="mix_out",
    )(x, o, gya, sgb, w_b, w_o.astype(BF16), row(g_mix_post))

    max_w = max(w for _, w in FFN_CHUNKS)
    h2 = pl.pallas_call(
        _ffn_kernel,
        grid=(bsz, seq // TS_FFN),
        in_specs=[_row_spec(TS_FFN, D_MODEL), _const_spec((1, D_MODEL)),
                  _const_spec((D_MODEL, 2 * D_FF)), _const_spec((3, 2 * D_FF)),
                  _const_spec((1, 2 * D_FF)), _const_spec((D_FF, D_MODEL)),
                  _const_spec((1, D_MODEL))],
        out_specs=_row_spec(TS_FFN, D_MODEL),
        out_shape=jax.ShapeDtypeStruct(x.shape, F32),
        scratch_shapes=[pltpu.VMEM((SUBLANES, 2 * D_FF), F32),
                        pltpu.VMEM((TS_FFN + SUBLANES, max_w), F32)],
        compiler_params=_params(("arbitrary", "arbitrary")),
        name="ffn",
    )(h1, row(g_ffn_pre), w_ffn_up.astype(BF16), conv_ffn_w.astype(F32), row(b_ffn_conv),
      w_ffn_down.astype(BF16), row(g_ffn_post))

    h3 = pl.pallas_call(
        _ple_kernel,
        grid=(bsz, seq // TS_PLE),
        in_specs=[_row_spec(TS_PLE, D_MODEL), _row_spec(TS_PLE, PLE_DIM),
                  _const_spec((PLE_DIM, D_MODEL)), _const_spec((D_MODEL, D_MODEL)),
                  _const_spec((1, D_MODEL))],
        out_specs=_row_spec(TS_PLE, D_MODEL),
        out_shape=jax.ShapeDtypeStruct(x.shape, F32),
        compiler_params=_params(("arbitrary", "arbitrary")),
        name="ple",
    )(h2, p, w_ple_proj.astype(BF16), w_ple_gate.astype(BF16), row(g_ple_post))
    return h3


def kernel(x, p, positions, g_mix_pre, w_in, conv_a_w, w_a_out, g_q_lat, w_q_up, g_kv_lat, w_kv_up, w_b_out, w_o, g_mix_post, g_ffn_pre, w_ffn_up, conv_ffn_w, b_ffn_conv, w_ffn_down, g_ffn_post, w_ple_proj, w_ple_gate, g_ple_post):
    h = x
    for i in range(g_mix_pre.shape[0]):
        h = _layer(h, p[i], positions, g_mix_pre[i], w_in[i], conv_a_w[i], w_a_out[i], g_q_lat[i],
                   w_q_up[i], g_kv_lat[i], w_kv_up[i], w_b_out[i], w_o[i], g_mix_post[i],
                   g_ffn_pre[i], w_ffn_up[i], conv_ffn_w[i], b_ffn_conv[i], w_ffn_down[i],
                   g_ffn_post[i], w_ple_proj[i], w_ple_gate[i], g_ple_post[i])
    return h
```

```python
import functools
import math

import jax
import jax.numpy as jnp
from jax.experimental import pallas as pl
from jax.experimental.pallas import tpu as pltpu

D_MODEL = 1024
CHUNK = 64
CONV_DIM = 512
N_HEADS = 8
NOPE_DIM = 64
ROPE_DIM = 32
V_DIM = 64
Q_LORA = 384
KV_LORA = 256
ROPE_THETA = 10000.0
D_FF = 2816
PLE_DIM = 256
EPS = 1e-6
QK_DIM = NOPE_DIM + ROPE_DIM

LANES = 128
SUBLANES = 8
HEAD_SLOT = LANES
HP = N_HEADS * HEAD_SLOT
LAT_COLS = Q_LORA + KV_LORA + 2 * LANES
VMEM_LIMIT = 56 * 1024 * 1024

TS_MIX = 512
TQ = 512
HEADS_PER_STEP = 2
V_ROWS = 80
TS_OUT = 512
TS_FFN = 512
TS_PLE = 512
FFN_CHUNKS = ((0, 1024), (1024, 1024), (2048, 768))

NEG_BIG = -1e30
BF16 = jnp.bfloat16
F32 = jnp.float32


def _rms(x, g):
    return x * jax.lax.rsqrt(jnp.mean(x * x, axis=-1, keepdims=True) + EPS) * g


def _dot(a, b):
    return jnp.dot(a, b, preferred_element_type=F32)


def _const_spec(shape):
    return pl.BlockSpec(shape, lambda *_: (0,) * len(shape))


def _shifted_rows(stage_ref, carry_ref, pre, rows, width):
    stage_ref[0:SUBLANES, 0:width] = carry_ref[...]
    stage_ref[SUBLANES:SUBLANES + rows, 0:width] = pre
    x1 = stage_ref[SUBLANES - 1:SUBLANES - 1 + rows, 0:width]
    x2 = stage_ref[SUBLANES - 2:SUBLANES - 2 + rows, 0:width]
    carry_ref[...] = pre[rows - SUBLANES:rows, :]
    return x1, x2


def _mix_in_kernel(x_ref, pos_ref, invf_ref, g_pre_ref, w_abc_ref, w_lat_ref,
                   w_ga_ref, w_gb_ref, conv_w_ref, w_a_out_ref, g_q_ref, w_q_ref,
                   g_kv_ref, w_k_ref, w_v_ref, vone_ref,
                   q_ref, k_ref, v_ref, gya_ref, sgb_ref,
                   carry_ref, stage_ref, *, q_scale):
    ts = x_ref.shape[0]

    @pl.when(pl.program_id(1) == 0)
    def _():
        carry_ref[...] = jnp.zeros_like(carry_ref)

    u = _rms(x_ref[...], g_pre_ref[...]).astype(BF16)

    abc = _dot(u, w_abc_ref[...])
    a_b = abc[:, 0:CONV_DIM]
    ca = abc[:, CONV_DIM:2 * CONV_DIM] * abc[:, 2 * CONV_DIM:3 * CONV_DIM]
    x1, x2 = _shifted_rows(stage_ref, carry_ref, ca, ts, CONV_DIM)
    cw = conv_w_ref[...]
    cv = x2 * cw[0:1, :] + x1 * cw[1:2, :] + ca * cw[2:3, :]
    ya = _dot((a_b * cv).astype(BF16), w_a_out_ref[...])
    gya_ref[...] = (jax.nn.sigmoid(_dot(u, w_ga_ref[...])) * ya).astype(BF16)
    sgb_ref[...] = jax.nn.sigmoid(_dot(u, w_gb_ref[...])).astype(BF16)

    lat = _dot(u, w_lat_ref[...])
    q_lat = lat[:, 0:Q_LORA]
    kv_lat = lat[:, Q_LORA:Q_LORA + KV_LORA]
    kr = lat[:, Q_LORA + KV_LORA:Q_LORA + KV_LORA + LANES]
    kr_rot = lat[:, Q_LORA + KV_LORA + LANES:LAT_COLS]

    ang = pos_ref[...].astype(F32) * invf_ref[...]
    cos = jnp.cos(ang)
    sin = jnp.sin(ang)

    qn = _rms(q_lat, g_q_ref[...]).astype(BF16)
    qq = _dot(qn, w_q_ref[...])
    kvn = _rms(kv_lat, g_kv_ref[...]).astype(BF16)
    kk = _dot(kvn, w_k_ref[...])
    vt = jax.lax.dot_general(w_v_ref[...], kvn, (((1,), (1,)), ((), ())),
                             preferred_element_type=F32) + vone_ref[...]
    v_ref[...] = vt.astype(BF16)

    k_pe = kr * cos + kr_rot * sin
    cos_q = cos * q_scale
    sin_q = sin * q_scale
    for h in range(N_HEADS):
        sl = slice(h * HEAD_SLOT, (h + 1) * HEAD_SLOT)
        q_ref[:, sl] = (qq[:, sl] * cos_q
                        + qq[:, HP + h * HEAD_SLOT:HP + (h + 1) * HEAD_SLOT] * sin_q).astype(BF16)
        k_ref[:, sl] = (kk[:, sl] + k_pe).astype(BF16)


def _attn_kernel(q_ref, k_ref, vt_ref, o_ref, s_ref, m_ref, acc_ref):
    tq = q_ref.shape[0]
    qi = pl.program_id(2)
    heads = range(HEADS_PER_STEP)
    lanes = lambda h: slice(h * HEAD_SLOT, (h + 1) * HEAD_SLOT)
    v_rows = lambda h: slice(h * HEAD_SLOT, h * HEAD_SLOT + V_ROWS)

    def scores(j, buf):
        off = pl.multiple_of(j * tq, tq)
        for h in heads:
            s_ref[buf, h] = jax.lax.dot_general(
                k_ref[pl.ds(off, tq), lanes(h)], q_ref[:, lanes(h)],
                (((1,), (1,)), ((), ())), preferred_element_type=F32)

    def accumulate(j, buf, mask):
        off = pl.multiple_of(j * tq, tq)
        load = ((lambda h: s_ref[buf, h]) if mask is None
                else (lambda h: jnp.where(mask, s_ref[buf, h], NEG_BIG)))
        for h in heads:
            m = m_ref[h]
            m_new = jnp.maximum(m, jnp.max(load(h), axis=0, keepdims=True))
            p = jnp.exp2(load(h) - m_new)
            pv = _dot(vt_ref[v_rows(h), pl.ds(off, tq)], p.astype(BF16))
            acc_ref[h, 0:V_ROWS, :] = acc_ref[h, 0:V_ROWS, :] * jnp.exp2(m - m_new) + pv
            m_ref[h] = m_new

    m_ref[...] = jnp.full_like(m_ref, NEG_BIG)
    acc_ref[...] = jnp.zeros_like(acc_ref)
    scores(0, 0)

    def tile_pair(i, _):
        scores(2 * i + 1, 1)
        accumulate(2 * i, 0, None)
        scores(2 * i + 2, 0)
        accumulate(2 * i + 1, 1, None)
        return 0

    jax.lax.fori_loop(0, qi // 2, tile_pair, 0)

    ck = jax.lax.broadcasted_iota(jnp.int32, (tq, tq), 0) // CHUNK
    cq = jax.lax.broadcasted_iota(jnp.int32, (tq, tq), 1) // CHUNK

    @pl.when(qi % 2 == 0)
    def _():
        accumulate(qi, 0, ck <= cq)

    @pl.when(qi % 2 == 1)
    def _():
        scores(qi, 1)
        accumulate(qi - 1, 0, None)
        accumulate(qi, 1, ck <= cq)

    for h in heads:
        acc = acc_ref[h]
        o_ref[:, lanes(h)] = (acc / acc[V_DIM:V_DIM + 1, :]).T.astype(BF16)


def _mix_out_kernel(x_ref, o_ref, gya_ref, sgb_ref, w_b_ref, w_o_ref, g_ref, h_ref):
    yb = _dot(o_ref[...], w_b_ref[...])
    mixed = gya_ref[...].astype(F32) + sgb_ref[...].astype(F32) * yb
    mo = _dot(mixed.astype(BF16), w_o_ref[...])
    h_ref[...] = x_ref[...] + _rms(mo, g_ref[...])


def _ffn_kernel(h_ref, g_pre_ref, w_up_ref, conv_w_ref, b_ref, w_down_ref, g_post_ref,
                out_ref, carry_ref, stage_ref):
    ts = h_ref.shape[0]

    @pl.when(pl.program_id(1) == 0)
    def _():
        carry_ref[...] = jnp.zeros_like(carry_ref)

    h = h_ref[...]
    u = _rms(h, g_pre_ref[...]).astype(BF16)

    def conv_cols(col, width):
        cols = slice(col, col + width)
        pre = _dot(u, w_up_ref[:, cols])
        x1, x2 = _shifted_rows(stage_ref, carry_ref.at[:, cols], pre, ts, width)
        cw = conv_w_ref[:, cols]
        return x2 * cw[0:1, :] + x1 * cw[1:2, :] + pre * cw[2:3, :] + b_ref[:, cols]

    f = jnp.zeros((ts, D_MODEL), F32)
    for off, width in FFN_CHUNKS:
        act = jax.nn.gelu(conv_cols(off, width), approximate=True)
        val = conv_cols(D_FF + off, width)
        f = f + _dot((act * val).astype(BF16), w_down_ref[off:off + width, :])
    out_ref[...] = h + _rms(f, g_post_ref[...])


def _ple_kernel(h_ref, p_ref, w_proj_ref, w_gate_ref, g_ref, out_ref):
    h = h_ref[...]
    e = _dot(p_ref[...].astype(BF16), w_proj_ref[...])
    g = jax.nn.sigmoid(_dot(h.astype(BF16), w_gate_ref[...]))
    out_ref[...] = h + _rms(e * g, g_ref[...])


def _row_spec(ts, width):
    return pl.BlockSpec((None, ts, width), lambda b, s: (b, s, 0))


def _params(sem):
    return pltpu.CompilerParams(dimension_semantics=sem, vmem_limit_bytes=VMEM_LIMIT)


def _layer(x, p, positions, g_mix_pre, w_in, conv_a_w, w_a_out, g_q_lat, w_q_up, g_kv_lat,
           w_kv_up, w_b_out, w_o, g_mix_post, g_ffn_pre, w_ffn_up, conv_ffn_w, b_ffn_conv,
           w_ffn_down, g_ffn_post, w_ple_proj, w_ple_gate, g_ple_post):
    bsz, seq, _ = x.shape
    row = lambda v: v.reshape(1, -1).astype(F32)

    c0 = 3 * CONV_DIM
    c1 = c0 + Q_LORA
    c2 = c1 + KV_LORA
    c3 = c2 + ROPE_DIM
    c4 = c3 + D_MODEL
    half = ROPE_DIM // 2
    w_kr = w_in[:, c2:c3]
    w_kr_rot = jnp.concatenate([-w_kr[:, half:], w_kr[:, :half]], axis=1)
    pad_slot = lambda w: jnp.pad(w, ((0, 0), (NOPE_DIM, HEAD_SLOT - NOPE_DIM - ROPE_DIM)))
    w_abc = w_in[:, 0:c0].astype(BF16)
    w_lat = jnp.concatenate([w_in[:, c0:c2], pad_slot(w_kr), pad_slot(w_kr_rot)], axis=1).astype(BF16)
    w_ga = w_in[:, c3:c4].astype(BF16)
    w_gb = w_in[:, c4:].astype(BF16)

    wq = w_q_up.reshape(Q_LORA, N_HEADS, QK_DIM)
    wq_plain = jnp.pad(wq, ((0, 0), (0, 0), (0, HEAD_SLOT - QK_DIM)))
    wq_rot = jnp.concatenate([-wq[:, :, NOPE_DIM + half:], wq[:, :, NOPE_DIM:NOPE_DIM + half]], axis=2)
    wq_rot = jnp.pad(wq_rot, ((0, 0), (0, 0), (NOPE_DIM, HEAD_SLOT - QK_DIM)))
    w_q = jnp.concatenate([wq_plain.reshape(Q_LORA, HP), wq_rot.reshape(Q_LORA, HP)], axis=1).astype(BF16)

    wkv = w_kv_up.reshape(KV_LORA, N_HEADS, NOPE_DIM + V_DIM)
    w_k = jnp.pad(wkv[:, :, :NOPE_DIM], ((0, 0), (0, 0), (0, HEAD_SLOT - NOPE_DIM))).reshape(KV_LORA, HP).astype(BF16)
    w_vt = jnp.pad(wkv[:, :, NOPE_DIM:], ((0, 0), (0, 0), (0, HEAD_SLOT - V_DIM))).reshape(KV_LORA, HP).T.astype(BF16)
    vone = jnp.tile((jnp.arange(HEAD_SLOT) == V_DIM).astype(F32), N_HEADS).reshape(HP, 1)
    w_b = jnp.pad(w_b_out.reshape(N_HEADS, V_DIM, D_MODEL),
                  ((0, 0), (0, HEAD_SLOT - V_DIM), (0, 0))).reshape(HP, D_MODEL).astype(BF16)

    inv = 1.0 / (ROPE_THETA ** (jnp.arange(half, dtype=F32) * (2.0 / ROPE_DIM)))
    invf = jnp.pad(jnp.concatenate([inv, inv]), (NOPE_DIM, HEAD_SLOT - QK_DIM)).reshape(1, HEAD_SLOT)
    pos3 = positions.reshape(bsz, seq, 1)
    q_scale = (QK_DIM ** -0.5) * math.log2(math.e)

    act = jax.ShapeDtypeStruct((bsz, seq, HP), BF16)
    q, k, vt, gya, sgb = pl.pallas_call(
        functools.partial(_mix_in_kernel, q_scale=q_scale),
        grid=(bsz, seq // TS_MIX),
        in_specs=[_row_spec(TS_MIX, D_MODEL), _row_spec(TS_MIX, 1),
                  _const_spec((1, HEAD_SLOT)), _const_spec((1, D_MODEL)),
                  _const_spec((D_MODEL, c0)), _const_spec((D_MODEL, LAT_COLS)),
                  _const_spec((D_MODEL, D_MODEL)), _const_spec((D_MODEL, D_MODEL)),
                  _const_spec((3, CONV_DIM)), _const_spec((CONV_DIM, D_MODEL)),
                  _const_spec((1, Q_LORA)), _const_spec((Q_LORA, 2 * HP)),
                  _const_spec((1, KV_LORA)), _const_spec((KV_LORA, HP)),
                  _const_spec((HP, KV_LORA)), _const_spec((HP, 1))],
        out_specs=[_row_spec(TS_MIX, HP), _row_spec(TS_MIX, HP),
                   pl.BlockSpec((None, HP, TS_MIX), lambda b, s: (b, 0, s)),
                   _row_spec(TS_MIX, HP), _row_spec(TS_MIX, HP)],
        out_shape=[act, act, jax.ShapeDtypeStruct((bsz, HP, seq), BF16), act, act],
        scratch_shapes=[pltpu.VMEM((SUBLANES, CONV_DIM), F32),
                        pltpu.VMEM((TS_MIX + SUBLANES, CONV_DIM), F32)],
        compiler_params=_params(("arbitrary", "arbitrary")),
        name="mix_in",
    )(x, pos3, invf, row(g_mix_pre), w_abc, w_lat, w_ga, w_gb, conv_a_w.astype(F32),
      w_a_out.astype(BF16), row(g_q_lat), w_q, row(g_kv_lat), w_k, w_vt, vone)

    group = HEADS_PER_STEP * HEAD_SLOT
    head_rows = pl.BlockSpec((None, TQ, group), lambda b, h, i: (b, i, h))
    o = pl.pallas_call(
        _attn_kernel,
        grid=(bsz, N_HEADS // HEADS_PER_STEP, seq // TQ),
        in_specs=[head_rows,
                  pl.BlockSpec((None, seq, group), lambda b, h, i: (b, 0, h)),
                  pl.BlockSpec((None, group, seq), lambda b, h, i: (b, h, 0))],
        out_specs=head_rows,
        out_shape=act,
        scratch_shapes=[pltpu.VMEM((2, HEADS_PER_STEP, TQ, TQ), F32),
                        pltpu.VMEM((HEADS_PER_STEP, 1, TQ), F32),
                        pltpu.VMEM((HEADS_PER_STEP, HEAD_SLOT, TQ), F32)],
        compiler_params=_params(("arbitrary", "arbitrary", "arbitrary")),
        name="attn",
    )(q, k, vt)

    h1 = pl.pallas_call(
        _mix_out_kernel,
        grid=(bsz, seq // TS_OUT),
        in_specs=[_row_spec(TS_OUT, D_MODEL), _row_spec(TS_OUT, HP), _row_spec(TS_OUT, HP),
                  _row_spec(TS_OUT, HP), _const_spec((HP, D_MODEL)),
                  _const_spec((D_MODEL, D_MODEL)), _const_spec((1, D_MODEL))],
        out_specs=_row_spec(TS_OUT, D_MODEL),
        out_shape=jax.ShapeDtypeStruct(x.shape, F32),
        compiler_params=_params(("arbitrary", "arbitrary")),
        name="mix_out",
    )(x, o, gya, sgb, w_b, w_o.astype(BF16), row(g_mix_post))

    max_w = max(w for _, w in FFN_CHUNKS)
    h2 = pl.pallas_call(
        _ffn_kernel,
        grid=(bsz, seq // TS_FFN),
        in_specs=[_row_spec(TS_FFN, D_MODEL), _const_spec((1, D_MODEL)),
                  _const_spec((D_MODEL, 2 * D_FF)), _const_spec((3, 2 * D_FF)),
                  _const_spec((1, 2 * D_FF)), _const_spec((D_FF, D_MODEL)),
                  _const_spec((1, D_MODEL))],
        out_specs=_row_spec(TS_FFN, D_MODEL),
        out_shape=jax.ShapeDtypeStruct(x.shape, F32),
        scratch_shapes=[pltpu.VMEM((SUBLANES, 2 * D_FF), F32),
                        pltpu.VMEM((TS_FFN + SUBLANES, max_w), F32)],
        compiler_params=_params(("arbitrary", "arbitrary")),
        name="ffn",
    )(h1, row(g_ffn_pre), w_ffn_up.astype(BF16), conv_ffn_w.astype(F32), row(b_ffn_conv),
      w_ffn_down.astype(BF16), row(g_ffn_post))

    h3 = pl.pallas_call(
        _ple_kernel,
        grid=(bsz, seq // TS_PLE),
        in_specs=[_row_spec(TS_PLE, D_MODEL), _row_spec(TS_PLE, PLE_DIM),
                  _const_spec((PLE_DIM, D_MODEL)), _const_spec((D_MODEL, D_MODEL)),
                  _const_spec((1, D_MODEL))],
        out_specs=_row_spec(TS_PLE, D_MODEL),
        out_shape=jax.ShapeDtypeStruct(x.shape, F32),
        compiler_params=_params(("arbitrary", "arbitrary")),
        name="ple",
    )(h2, p, w_ple_proj.astype(BF16), w_ple_gate.astype(BF16), row(g_ple_post))
    return h3


def kernel(x, p, positions, g_mix_pre, w_in, conv_a_w, w_a_out, g_q_lat, w_q_up, g_kv_lat, w_kv_up, w_b_out, w_o, g_mix_post, g_ffn_pre, w_ffn_up, conv_ffn_w, b_ffn_conv, w_ffn_down, g_ffn_post, w_ple_proj, w_ple_gate, g_ple_post):
    h = x
    for i in range(g_mix_pre.shape[0]):
        h = _layer(h, p[i], positions, g_mix_pre[i], w_in[i], conv_a_w[i], w_a_out[i], g_q_lat[i],
                   w_q_up[i], g_kv_lat[i], w_kv_up[i], w_b_out[i], w_o[i], g_mix_post[i],
                   g_ffn_pre[i], w_ffn_up[i], conv_ffn_w[i], b_ffn_conv[i], w_ffn_down[i],
                   g_ffn_post[i], w_ple_proj[i], w_ple_gate[i], g_ple_post[i])
    return h
```

```python
import functools
import math

import jax
import jax.numpy as jnp
from jax.experimental import pallas as pl
from jax.experimental.pallas import tpu as pltpu

D_MODEL = 1024
CHUNK = 64
CONV_DIM = 512
N_HEADS = 8
NOPE_DIM = 64
ROPE_DIM = 32
V_DIM = 64
Q_LORA = 384
KV_LORA = 256
ROPE_THETA = 10000.0
D_FF = 2816
PLE_DIM = 256
EPS = 1e-6
QK_DIM = NOPE_DIM + ROPE_DIM

LANES = 128
SUBLANES = 8
HEAD_SLOT = LANES
HP = N_HEADS * HEAD_SLOT
LAT_COLS = Q_LORA + KV_LORA + 2 * LANES
VMEM_LIMIT = 56 * 1024 * 1024

TS_MIX = 512
TQ = 512
HEADS_PER_STEP = 2
V_ROWS = 80
TS_OUT = 512
TS_FFN = 512
TS_PLE = 512
FFN_CHUNKS = ((0, 1024), (1024, 1024), (2048, 768))

NEG_BIG = -1e30
BF16 = jnp.bfloat16
F32 = jnp.float32


def _rms(x, g):
    return x * jax.lax.rsqrt(jnp.mean(x * x, axis=-1, keepdims=True) + EPS) * g


def _dot(a, b):
    return jnp.dot(a, b, preferred_element_type=F32)


def _const_spec(shape):
    return pl.BlockSpec(shape, lambda *_: (0,) * len(shape))


def _shifted_rows(stage_ref, carry_ref, pre, rows, width):
    stage_ref[0:SUBLANES, 0:width] = carry_ref[...]
    stage_ref[SUBLANES:SUBLANES + rows, 0:width] = pre
    x1 = stage_ref[SUBLANES - 1:SUBLANES - 1 + rows, 0:width]
    x2 = stage_ref[SUBLANES - 2:SUBLANES - 2 + rows, 0:width]
    carry_ref[...] = pre[rows - SUBLANES:rows, :]
    return x1, x2


def _mix_in_kernel(x_ref, pos_ref, invf_ref, g_pre_ref, w_abc_ref, w_lat_ref,
                   w_ga_ref, w_gb_ref, conv_w_ref, w_a_out_ref, g_q_ref, w_q_ref,
                   g_kv_ref, w_k_ref, w_v_ref, vone_ref,
                   q_ref, k_ref, v_ref, gya_ref, sgb_ref,
                   carry_ref, stage_ref, *, q_scale):
    ts = x_ref.shape[0]

    @pl.when(pl.program_id(1) == 0)
    def _():
        carry_ref[...] = jnp.zeros_like(carry_ref)

    u = _rms(x_ref[...], g_pre_ref[...]).astype(BF16)

    abc = _dot(u, w_abc_ref[...])
    a_b = abc[:, 0:CONV_DIM]
    ca = abc[:, CONV_DIM:2 * CONV_DIM] * abc[:, 2 * CONV_DIM:3 * CONV_DIM]
    x1, x2 = _shifted_rows(stage_ref, carry_ref, ca, ts, CONV_DIM)
    cw = conv_w_ref[...]
    cv = x2 * cw[0:1, :] + x1 * cw[1:2, :] + ca * cw[2:3, :]
    ya = _dot((a_b * cv).astype(BF16), w_a_out_ref[...])
    gya_ref[...] = (jax.nn.sigmoid(_dot(u, w_ga_ref[...])) * ya).astype(BF16)
    sgb_ref[...] = jax.nn.sigmoid(_dot(u, w_gb_ref[...])).astype(BF16)

    lat = _dot(u, w_lat_ref[...])
    q_lat = lat[:, 0:Q_LORA]
    kv_lat = lat[:, Q_LORA:Q_LORA + KV_LORA]
    kr = lat[:, Q_LORA + KV_LORA:Q_LORA + KV_LORA + LANES]
    kr_rot = lat[:, Q_LORA + KV_LORA + LANES:LAT_COLS]

    ang = pos_ref[...].astype(F32) * invf_ref[...]
    cos = jnp.cos(ang)
    sin = jnp.sin(ang)

    qn = _rms(q_lat, g_q_ref[...]).astype(BF16)
    qq = _dot(qn, w_q_ref[...])
    kvn = _rms(kv_lat, g_kv_ref[...]).astype(BF16)
    kk = _dot(kvn, w_k_ref[...])
    vt = jax.lax.dot_general(w_v_ref[...], kvn, (((1,), (1,)), ((), ())),
                             preferred_element_type=F32) + vone_ref[...]
    v_ref[...] = vt.astype(BF16)

    k_pe = kr * cos + kr_rot * sin
    cos_q = cos * q_scale
    sin_q = sin * q_scale
    for h in range(N_HEADS):
        sl = slice(h * HEAD_SLOT, (h + 1) * HEAD_SLOT)
        q_ref[:, sl] = (qq[:, sl] * cos_q
                        + qq[:, HP + h * HEAD_SLOT:HP + (h + 1) * HEAD_SLOT] * sin_q).astype(BF16)
        k_ref[:, sl] = (kk[:, sl] + k_pe).astype(BF16)


def _attn_kernel(q_ref, k_ref, vt_ref, o_ref, s_ref, m_ref, acc_ref):
    tq = s_ref.shape[-1]
    heads = range(HEADS_PER_STEP)
    lanes = lambda h: slice(h * HEAD_SLOT, (h + 1) * HEAD_SLOT)
    v_rows = lambda h: slice(h * HEAD_SLOT, h * HEAD_SLOT + V_ROWS)
    tile = lambda i: pl.ds(pl.multiple_of(i * tq, tq), tq)

    def scores(qi, j, buf):
        for h in heads:
            s_ref[buf, h] = jax.lax.dot_general(
                k_ref[tile(j), lanes(h)], q_ref[tile(qi), lanes(h)],
                (((1,), (1,)), ((), ())), preferred_element_type=F32)

    def accumulate(j, buf, mask):
        load = ((lambda h: s_ref[buf, h]) if mask is None
                else (lambda h: jnp.where(mask, s_ref[buf, h], NEG_BIG)))
        for h in heads:
            m = m_ref[h]
            m_new = jnp.maximum(m, jnp.max(load(h), axis=0, keepdims=True))
            p = jnp.exp2(load(h) - m_new)
            pv = _dot(vt_ref[v_rows(h), tile(j)], p.astype(BF16))
            acc_ref[h, 0:V_ROWS, :] = acc_ref[h, 0:V_ROWS, :] * jnp.exp2(m - m_new) + pv
            m_ref[h] = m_new

    def query_tile(qi, _):
        m_ref[...] = jnp.full_like(m_ref, NEG_BIG)
        acc_ref[...] = jnp.zeros_like(acc_ref)
        scores(qi, 0, 0)

        def tile_pair(i, _):
            scores(qi, 2 * i + 1, 1)
            accumulate(2 * i, 0, None)
            scores(qi, 2 * i + 2, 0)
            accumulate(2 * i + 1, 1, None)
            return 0

        jax.lax.fori_loop(0, qi // 2, tile_pair, 0)

        ck = jax.lax.broadcasted_iota(jnp.int32, (tq, tq), 0) // CHUNK
        cq = jax.lax.broadcasted_iota(jnp.int32, (tq, tq), 1) // CHUNK

        @pl.when(qi % 2 == 0)
        def _():
            accumulate(qi, 0, ck <= cq)

        @pl.when(qi % 2 == 1)
        def _():
            scores(qi, qi, 1)
            accumulate(qi - 1, 0, None)
            accumulate(qi, 1, ck <= cq)

        for h in heads:
            acc = acc_ref[h]
            o_ref[tile(qi), lanes(h)] = (acc / acc[V_DIM:V_DIM + 1, :]).T.astype(BF16)
        return 0

    jax.lax.fori_loop(0, q_ref.shape[0] // tq, query_tile, 0)


def _mix_out_kernel(x_ref, o_ref, gya_ref, sgb_ref, w_b_ref, w_o_ref, g_ref, h_ref):
    yb = _dot(o_ref[...], w_b_ref[...])
    mixed = gya_ref[...].astype(F32) + sgb_ref[...].astype(F32) * yb
    mo = _dot(mixed.astype(BF16), w_o_ref[...])
    h_ref[...] = x_ref[...] + _rms(mo, g_ref[...])


def _ffn_kernel(h_ref, g_pre_ref, w_up_ref, conv_w_ref, b_ref, w_down_ref, g_post_ref,
                out_ref, carry_ref, stage_ref):
    ts = h_ref.shape[0]

    @pl.when(pl.program_id(1) == 0)
    def _():
        carry_ref[...] = jnp.zeros_like(carry_ref)

    h = h_ref[...]
    u = _rms(h, g_pre_ref[...]).astype(BF16)

    def conv_cols(col, width):
        cols = slice(col, col + width)
        pre = _dot(u, w_up_ref[:, cols])
        x1, x2 = _shifted_rows(stage_ref, carry_ref.at[:, cols], pre, ts, width)
        cw = conv_w_ref[:, cols]
        return x2 * cw[0:1, :] + x1 * cw[1:2, :] + pre * cw[2:3, :] + b_ref[:, cols]

    f = jnp.zeros((ts, D_MODEL), F32)
    for off, width in FFN_CHUNKS:
        act = jax.nn.gelu(conv_cols(off, width), approximate=True)
        val = conv_cols(D_FF + off, width)
        f = f + _dot((act * val).astype(BF16), w_down_ref[off:off + width, :])
    out_ref[...] = h + _rms(f, g_post_ref[...])


def _ple_kernel(h_ref, p_ref, w_proj_ref, w_gate_ref, g_ref, out_ref):
    h = h_ref[...]
    e = _dot(p_ref[...].astype(BF16), w_proj_ref[...])
    g = jax.nn.sigmoid(_dot(h.astype(BF16), w_gate_ref[...]))
    out_ref[...] = h + _rms(e * g, g_ref[...])


def _row_spec(ts, width):
    return pl.BlockSpec((None, ts, width), lambda b, s: (b, s, 0))


def _params(sem):
    return pltpu.CompilerParams(dimension_semantics=sem, vmem_limit_bytes=VMEM_LIMIT)


def _layer(x, p, positions, g_mix_pre, w_in, conv_a_w, w_a_out, g_q_lat, w_q_up, g_kv_lat,
           w_kv_up, w_b_out, w_o, g_mix_post, g_ffn_pre, w_ffn_up, conv_ffn_w, b_ffn_conv,
           w_ffn_down, g_ffn_post, w_ple_proj, w_ple_gate, g_ple_post):
    bsz, seq, _ = x.shape
    row = lambda v: v.reshape(1, -1).astype(F32)

    c0 = 3 * CONV_DIM
    c1 = c0 + Q_LORA
    c2 = c1 + KV_LORA
    c3 = c2 + ROPE_DIM
    c4 = c3 + D_MODEL
    half = ROPE_DIM // 2
    w_kr = w_in[:, c2:c3]
    w_kr_rot = jnp.concatenate([-w_kr[:, half:], w_kr[:, :half]], axis=1)
    pad_slot = lambda w: jnp.pad(w, ((0, 0), (NOPE_DIM, HEAD_SLOT - NOPE_DIM - ROPE_DIM)))
    w_abc = w_in[:, 0:c0].astype(BF16)
    w_lat = jnp.concatenate([w_in[:, c0:c2], pad_slot(w_kr), pad_slot(w_kr_rot)], axis=1).astype(BF16)
    w_ga = w_in[:, c3:c4].astype(BF16)
    w_gb = w_in[:, c4:].astype(BF16)

    wq = w_q_up.reshape(Q_LORA, N_HEADS, QK_DIM)
    wq_plain = jnp.pad(wq, ((0, 0), (0, 0), (0, HEAD_SLOT - QK_DIM)))
    wq_rot = jnp.concatenate([-wq[:, :, NOPE_DIM + half:], wq[:, :, NOPE_DIM:NOPE_DIM + half]], axis=2)
    wq_rot = jnp.pad(wq_rot, ((0, 0), (0, 0), (NOPE_DIM, HEAD_SLOT - QK_DIM)))
    w_q = jnp.concatenate([wq_plain.reshape(Q_LORA, HP), wq_rot.reshape(Q_LORA, HP)], axis=1).astype(BF16)

    wkv = w_kv_up.reshape(KV_LORA, N_HEADS, NOPE_DIM + V_DIM)
    w_k = jnp.pad(wkv[:, :, :NOPE_DIM], ((0, 0), (0, 0), (0, HEAD_SLOT - NOPE_DIM))).reshape(KV_LORA, HP).astype(BF16)
    w_vt = jnp.pad(wkv[:, :, NOPE_DIM:], ((0, 0), (0, 0), (0, HEAD_SLOT - V_DIM))).reshape(KV_LORA, HP).T.astype(BF16)
    vone = jnp.tile((jnp.arange(HEAD_SLOT) == V_DIM).astype(F32), N_HEADS).reshape(HP, 1)
    w_b = jnp.pad(w_b_out.reshape(N_HEADS, V_DIM, D_MODEL),
                  ((0, 0), (0, HEAD_SLOT - V_DIM), (0, 0))).reshape(HP, D_MODEL).astype(BF16)

    inv = 1.0 / (ROPE_THETA ** (jnp.arange(half, dtype=F32) * (2.0 / ROPE_DIM)))
    invf = jnp.pad(jnp.concatenate([inv, inv]), (NOPE_DIM, HEAD_SLOT - QK_DIM)).reshape(1, HEAD_SLOT)
    pos3 = positions.reshape(bsz, seq, 1)
    q_scale = (QK_DIM ** -0.5) * math.log2(math.e)

    act = jax.ShapeDtypeStruct((bsz, seq, HP), BF16)
    q, k, vt, gya, sgb = pl.pallas_call(
        functools.partial(_mix_in_kernel, q_scale=q_scale),
        grid=(bsz, seq // TS_MIX),
        in_specs=[_row_spec(TS_MIX, D_MODEL), _row_spec(TS_MIX, 1),
                  _const_spec((1, HEAD_SLOT)), _const_spec((1, D_MODEL)),
                  _const_spec((D_MODEL, c0)), _const_spec((D_MODEL, LAT_COLS)),
                  _const_spec((D_MODEL, D_MODEL)), _const_spec((D_MODEL, D_MODEL)),
                  _const_spec((3, CONV_DIM)), _const_spec((CONV_DIM, D_MODEL)),
                  _const_spec((1, Q_LORA)), _const_spec((Q_LORA, 2 * HP)),
                  _const_spec((1, KV_LORA)), _const_spec((KV_LORA, HP)),
                  _const_spec((HP, KV_LORA)), _const_spec((HP, 1))],
        out_specs=[_row_spec(TS_MIX, HP), _row_spec(TS_MIX, HP),
                   pl.BlockSpec((None, HP, TS_MIX), lambda b, s: (b, 0, s)),
                   _row_spec(TS_MIX, HP), _row_spec(TS_MIX, HP)],
        out_shape=[act, act, jax.ShapeDtypeStruct((bsz, HP, seq), BF16), act, act],
        scratch_shapes=[pltpu.VMEM((SUBLANES, CONV_DIM), F32),
                        pltpu.VMEM((TS_MIX + SUBLANES, CONV_DIM), F32)],
        compiler_params=_params(("arbitrary", "arbitrary")),
        name="mix_in",
    )(x, pos3, invf, row(g_mix_pre), w_abc, w_lat, w_ga, w_gb, conv_a_w.astype(F32),
      w_a_out.astype(BF16), row(g_q_lat), w_q, row(g_kv_lat), w_k, w_vt, vone)

    group = HEADS_PER_STEP * HEAD_SLOT
    head_cols = pl.BlockSpec((None, seq, group), lambda b, h: (b, 0, h))
    o = pl.pallas_call(
        _attn_kernel,
        grid=(bsz, N_HEADS // HEADS_PER_STEP),
        in_specs=[head_cols, head_cols,
                  pl.BlockSpec((None, group, seq), lambda b, h: (b, h, 0))],
        out_specs=head_cols,
        out_shape=act,
        scratch_shapes=[pltpu.VMEM((2, HEADS_PER_STEP, TQ, TQ), F32),
                        pltpu.VMEM((HEADS_PER_STEP, 1, TQ), F32),
                        pltpu.VMEM((HEADS_PER_STEP, HEAD_SLOT, TQ), F32)],
        compiler_params=_params(("arbitrary", "arbitrary")),
        name="attn",
    )(q, k, vt)

    h1 = pl.pallas_call(
        _mix_out_kernel,
        grid=(bsz, seq // TS_OUT),
        in_specs=[_row_spec(TS_OUT, D_MODEL), _row_spec(TS_OUT, HP), _row_spec(TS_OUT, HP),
                  _row_spec(TS_OUT, HP), _const_spec((HP, D_MODEL)),
                  _const_spec((D_MODEL, D_MODEL)), _const_spec((1, D_MODEL))],
        out_specs=_row_spec(TS_OUT, D_MODEL),
        out_shape=jax.ShapeDtypeStruct(x.shape, F32),
        compiler_params=_params(("arbitrary", "arbitrary")),
        name="mix_out",
    )(x, o, gya, sgb, w_b, w_o.astype(BF16), row(g_mix_post))

    max_w = max(w for _, w in FFN_CHUNKS)
    h2 = pl.pallas_call(
        _ffn_kernel,
        grid=(bsz, seq // TS_FFN),
        in_specs=[_row_spec(TS_FFN, D_MODEL), _const_spec((1, D_MODEL)),
                  _const_spec((D_MODEL, 2 * D_FF)), _const_spec((3, 2 * D_FF)),
                  _const_spec((1, 2 * D_FF)), _const_spec((D_FF, D_MODEL)),
                  _const_spec((1, D_MODEL))],
        out_specs=_row_spec(TS_FFN, D_MODEL),
        out_shape=jax.ShapeDtypeStruct(x.shape, F32),
        scratch_shapes=[pltpu.VMEM((SUBLANES, 2 * D_FF), F32),
                        pltpu.VMEM((TS_FFN + SUBLANES, max_w), F32)],
        compiler_params=_params(("arbitrary", "arbitrary")),
        name="ffn",
    )(h1, row(g_ffn_pre), w_ffn_up.astype(BF16), conv_ffn_w.astype(F32), row(b_ffn_conv),
      w_ffn_down.astype(BF16), row(g_ffn_post))

    h3 = pl.pallas_call(
        _ple_kernel,
        grid=(bsz, seq // TS_PLE),
        in_specs=[_row_spec(TS_PLE, D_MODEL), _row_spec(TS_PLE, PLE_DIM),
                  _const_spec((PLE_DIM, D_MODEL)), _const_spec((D_MODEL, D_MODEL)),
                  _const_spec((1, D_MODEL))],
        out_specs=_row_spec(TS_PLE, D_MODEL),
        out_shape=jax.ShapeDtypeStruct(x.shape, F32),
        compiler_params=_params(("arbitrary", "arbitrary")),
        name="ple",
    )(h2, p, w_ple_proj.astype(BF16), w_ple_gate.astype(BF16), row(g_ple_post))
    return h3


def kernel(x, p, positions, g_mix_pre, w_in, conv_a_w, w_a_out, g_q_lat, w_q_up, g_kv_lat, w_kv_up, w_b_out, w_o, g_mix_post, g_ffn_pre, w_ffn_up, conv_ffn_w, b_ffn_conv, w_ffn_down, g_ffn_post, w_ple_proj, w_ple_gate, g_ple_post):
    h = x
    for i in range(g_mix_pre.shape[0]):
        h = _layer(h, p[i], positions, g_mix_pre[i], w_in[i], conv_a_w[i], w_a_out[i], g_q_lat[i],
                   w_q_up[i], g_kv_lat[i], w_kv_up[i], w_b_out[i], w_o[i], g_mix_post[i],
                   g_ffn_pre[i], w_ffn_up[i], conv_ffn_w[i], b_ffn_conv[i], w_ffn_down[i],
                   g_ffn_post[i], w_ple_proj[i], w_ple_gate[i], g_ple_post[i])
    return h
```

```python
import functools
import math

import jax
import jax.numpy as jnp
from jax.experimental import pallas as pl
from jax.experimental.pallas import tpu as pltpu

D_MODEL = 1024
CHUNK = 64
CONV_DIM = 512
N_HEADS = 8
NOPE_DIM = 64
ROPE_DIM = 32
V_DIM = 64
Q_LORA = 384
KV_LORA = 256
ROPE_THETA = 10000.0
D_FF = 2816
PLE_DIM = 256
EPS = 1e-6
QK_DIM = NOPE_DIM + ROPE_DIM

LANES = 128
SUBLANES = 8
HEAD_SLOT = LANES
HP = N_HEADS * HEAD_SLOT
LAT_COLS = Q_LORA + KV_LORA + 2 * LANES
VMEM_LIMIT = 56 * 1024 * 1024

TS_MIX = 512
TQ = 1024
TKEY = TQ // 2
HEADS_PER_STEP = 2
V_ROWS = 80
TS_OUT = 512
TS_FFN = 512
TS_PLE = 512
FFN_CHUNKS = ((0, 1024), (1024, 1024), (2048, 768))

NEG_BIG = -1e30
BF16 = jnp.bfloat16
F32 = jnp.float32


def _rms(x, g):
    return x * jax.lax.rsqrt(jnp.mean(x * x, axis=-1, keepdims=True) + EPS) * g


def _dot(a, b):
    return jnp.dot(a, b, preferred_element_type=F32)


def _const_spec(shape):
    return pl.BlockSpec(shape, lambda *_: (0,) * len(shape))


def _shifted_rows(stage_ref, carry_ref, pre, rows, width):
    stage_ref[0:SUBLANES, 0:width] = carry_ref[...]
    stage_ref[SUBLANES:SUBLANES + rows, 0:width] = pre
    x1 = stage_ref[SUBLANES - 1:SUBLANES - 1 + rows, 0:width]
    x2 = stage_ref[SUBLANES - 2:SUBLANES - 2 + rows, 0:width]
    carry_ref[...] = pre[rows - SUBLANES:rows, :]
    return x1, x2


def _mix_in_kernel(x_ref, pos_ref, invf_ref, g_pre_ref, w_abc_ref, w_lat_ref,
                   w_ga_ref, w_gb_ref, conv_w_ref, w_a_out_ref, g_q_ref, w_q_ref,
                   g_kv_ref, w_k_ref, w_v_ref, vone_ref,
                   q_ref, k_ref, v_ref, gya_ref, sgb_ref,
                   carry_ref, stage_ref, *, q_scale):
    ts = x_ref.shape[0]

    @pl.when(pl.program_id(1) == 0)
    def _():
        carry_ref[...] = jnp.zeros_like(carry_ref)

    u = _rms(x_ref[...], g_pre_ref[...]).astype(BF16)

    abc = _dot(u, w_abc_ref[...])
    a_b = abc[:, 0:CONV_DIM]
    ca = abc[:, CONV_DIM:2 * CONV_DIM] * abc[:, 2 * CONV_DIM:3 * CONV_DIM]
    x1, x2 = _shifted_rows(stage_ref, carry_ref, ca, ts, CONV_DIM)
    cw = conv_w_ref[...]
    cv = x2 * cw[0:1, :] + x1 * cw[1:2, :] + ca * cw[2:3, :]
    ya = _dot((a_b * cv).astype(BF16), w_a_out_ref[...])
    gya_ref[...] = (jax.nn.sigmoid(_dot(u, w_ga_ref[...])) * ya).astype(BF16)
    sgb_ref[...] = jax.nn.sigmoid(_dot(u, w_gb_ref[...])).astype(BF16)

    lat = _dot(u, w_lat_ref[...])
    q_lat = lat[:, 0:Q_LORA]
    kv_lat = lat[:, Q_LORA:Q_LORA + KV_LORA]
    kr = lat[:, Q_LORA + KV_LORA:Q_LORA + KV_LORA + LANES]
    kr_rot = lat[:, Q_LORA + KV_LORA + LANES:LAT_COLS]

    ang = pos_ref[...].astype(F32) * invf_ref[...]
    cos = jnp.cos(ang)
    sin = jnp.sin(ang)

    qn = _rms(q_lat, g_q_ref[...]).astype(BF16)
    qq = _dot(qn, w_q_ref[...])
    kvn = _rms(kv_lat, g_kv_ref[...]).astype(BF16)
    kk = _dot(kvn, w_k_ref[...])
    vt = jax.lax.dot_general(w_v_ref[...], kvn, (((1,), (1,)), ((), ())),
                             preferred_element_type=F32) + vone_ref[...]
    v_ref[...] = vt.astype(BF16)

    k_pe = kr * cos + kr_rot * sin
    cos_q = cos * q_scale
    sin_q = sin * q_scale
    for h in range(N_HEADS):
        sl = slice(h * HEAD_SLOT, (h + 1) * HEAD_SLOT)
        q_ref[:, sl] = (qq[:, sl] * cos_q
                        + qq[:, HP + h * HEAD_SLOT:HP + (h + 1) * HEAD_SLOT] * sin_q).astype(BF16)
        k_ref[:, sl] = (kk[:, sl] + k_pe).astype(BF16)


def _attn_kernel(q_ref, k_ref, vt_ref, o_ref, s_ref, m_ref, acc_ref):
    tk, tq = s_ref.shape[-2:]
    assert tq == 2 * tk
    nq = q_ref.shape[0] // tq
    heads = range(HEADS_PER_STEP)
    lanes = lambda h: slice(h * HEAD_SLOT, (h + 1) * HEAD_SLOT)
    v_rows = lambda h: slice(h * HEAD_SLOT, h * HEAD_SLOT + V_ROWS)
    key_tile = lambda j: pl.ds(pl.multiple_of(j * tk, tk), tk)
    every = slice(0, tq)
    later = slice(tk, tq)

    def qk(k_rows, q_rows, h):
        return jax.lax.dot_general(k_ref[k_rows, lanes(h)], q_ref[q_rows, lanes(h)],
                                   (((1,), (1,)), ((), ())), preferred_element_type=F32)

    def scores(qi, j, buf):
        for h in heads:
            s_ref[buf, h] = qk(key_tile(j), pl.ds(pl.multiple_of(qi * tq, tq), tq), h)

    def scores_later(qi, buf):
        for h in heads:
            s_ref[buf, h, :, later] = qk(key_tile(2 * qi + 1), key_tile(2 * qi + 1), h)

    def accumulate(j, buf, cols, masked):
        width = cols.stop - cols.start
        if masked:
            ck = jax.lax.broadcasted_iota(jnp.int32, (tk, width), 0) // CHUNK
            cq = jax.lax.broadcasted_iota(jnp.int32, (tk, width), 1) // CHUNK
            load = lambda h: jnp.where(ck <= cq, s_ref[buf, h, :, cols], NEG_BIG)
        else:
            load = lambda h: s_ref[buf, h, :, cols]
        for h in heads:
            m = m_ref[h, :, cols]
            m_new = jnp.maximum(m, jnp.max(load(h), axis=0, keepdims=True))
            p = jnp.exp2(load(h) - m_new)
            pv = _dot(vt_ref[v_rows(h), key_tile(j)], p.astype(BF16))
            acc_ref[h, 0:V_ROWS, cols] = acc_ref[h, 0:V_ROWS, cols] * jnp.exp2(m - m_new) + pv
            m_ref[h, :, cols] = m_new

    def query_tile(qi, _):
        m_ref[...] = jnp.full_like(m_ref, NEG_BIG)
        acc_ref[...] = jnp.zeros_like(acc_ref)

        def tile_pair(i, _):
            scores(qi, 2 * i + 1, 1)
            accumulate(2 * i, 0, every, False)
            scores(qi, 2 * i + 2, 0)
            accumulate(2 * i + 1, 1, every, False)
            return 0

        jax.lax.fori_loop(0, qi, tile_pair, 0)

        scores_later(qi, 1)
        accumulate(2 * qi, 0, every, True)
        scores(jnp.minimum(qi + 1, nq - 1), 0, 0)
        accumulate(2 * qi + 1, 1, later, True)

        for h in heads:
            acc = acc_ref[h]
            o_ref[pl.ds(pl.multiple_of(qi * tq, tq), tq), lanes(h)] = (
                acc / acc[V_DIM:V_DIM + 1, :]).T.astype(BF16)
        return 0

    scores(0, 0, 0)
    jax.lax.fori_loop(0, nq, query_tile, 0)


def _mix_out_kernel(x_ref, o_ref, gya_ref, sgb_ref, w_b_ref, w_o_ref, g_ref, h_ref):
    yb = _dot(o_ref[...], w_b_ref[...])
    mixed = gya_ref[...].astype(F32) + sgb_ref[...].astype(F32) * yb
    mo = _dot(mixed.astype(BF16), w_o_ref[...])
    h_ref[...] = x_ref[...] + _rms(mo, g_ref[...])


def _ffn_kernel(h_ref, g_pre_ref, w_up_ref, conv_w_ref, b_ref, w_down_ref, g_post_ref,
                out_ref, carry_ref, stage_ref):
    ts = h_ref.shape[0]

    @pl.when(pl.program_id(1) == 0)
    def _():
        carry_ref[...] = jnp.zeros_like(carry_ref)

    h = h_ref[...]
    u = _rms(h, g_pre_ref[...]).astype(BF16)

    def conv_cols(col, width):
        cols = slice(col, col + width)
        pre = _dot(u, w_up_ref[:, cols])
        x1, x2 = _shifted_rows(stage_ref, carry_ref.at[:, cols], pre, ts, width)
        cw = conv_w_ref[:, cols]
        return x2 * cw[0:1, :] + x1 * cw[1:2, :] + pre * cw[2:3, :] + b_ref[:, cols]

    f = jnp.zeros((ts, D_MODEL), F32)
    for off, width in FFN_CHUNKS:
        act = jax.nn.gelu(conv_cols(off, width), approximate=True)
        val = conv_cols(D_FF + off, width)
        f = f + _dot((act * val).astype(BF16), w_down_ref[off:off + width, :])
    out_ref[...] = h + _rms(f, g_post_ref[...])


def _ple_kernel(h_ref, p_ref, w_proj_ref, w_gate_ref, g_ref, out_ref):
    h = h_ref[...]
    e = _dot(p_ref[...].astype(BF16), w_proj_ref[...])
    g = jax.nn.sigmoid(_dot(h.astype(BF16), w_gate_ref[...]))
    out_ref[...] = h + _rms(e * g, g_ref[...])


def _row_spec(ts, width):
    return pl.BlockSpec((None, ts, width), lambda b, s: (b, s, 0))


def _params(sem):
    return pltpu.CompilerParams(dimension_semantics=sem, vmem_limit_bytes=VMEM_LIMIT)


def _layer(x, p, positions, g_mix_pre, w_in, conv_a_w, w_a_out, g_q_lat, w_q_up, g_kv_lat,
           w_kv_up, w_b_out, w_o, g_mix_post, g_ffn_pre, w_ffn_up, conv_ffn_w, b_ffn_conv,
           w_ffn_down, g_ffn_post, w_ple_proj, w_ple_gate, g_ple_post):
    bsz, seq, _ = x.shape
    row = lambda v: v.reshape(1, -1).astype(F32)

    c0 = 3 * CONV_DIM
    c1 = c0 + Q_LORA
    c2 = c1 + KV_LORA
    c3 = c2 + ROPE_DIM
    c4 = c3 + D_MODEL
    half = ROPE_DIM // 2
    w_kr = w_in[:, c2:c3]
    w_kr_rot = jnp.concatenate([-w_kr[:, half:], w_kr[:, :half]], axis=1)
    pad_slot = lambda w: jnp.pad(w, ((0, 0), (NOPE_DIM, HEAD_SLOT - NOPE_DIM - ROPE_DIM)))
    w_abc = w_in[:, 0:c0].astype(BF16)
    w_lat = jnp.concatenate([w_in[:, c0:c2], pad_slot(w_kr), pad_slot(w_kr_rot)], axis=1).astype(BF16)
    w_ga = w_in[:, c3:c4].astype(BF16)
    w_gb = w_in[:, c4:].astype(BF16)

    wq = w_q_up.reshape(Q_LORA, N_HEADS, QK_DIM)
    wq_plain = jnp.pad(wq, ((0, 0), (0, 0), (0, HEAD_SLOT - QK_DIM)))
    wq_rot = jnp.concatenate([-wq[:, :, NOPE_DIM + half:], wq[:, :, NOPE_DIM:NOPE_DIM + half]], axis=2)
    wq_rot = jnp.pad(wq_rot, ((0, 0), (0, 0), (NOPE_DIM, HEAD_SLOT - QK_DIM)))
    w_q = jnp.concatenate([wq_plain.reshape(Q_LORA, HP), wq_rot.reshape(Q_LORA, HP)], axis=1).astype(BF16)

    wkv = w_kv_up.reshape(KV_LORA, N_HEADS, NOPE_DIM + V_DIM)
    w_k = jnp.pad(wkv[:, :, :NOPE_DIM], ((0, 0), (0, 0), (0, HEAD_SLOT - NOPE_DIM))).reshape(KV_LORA, HP).astype(BF16)
    w_vt = jnp.pad(wkv[:, :, NOPE_DIM:], ((0, 0), (0, 0), (0, HEAD_SLOT - V_DIM))).reshape(KV_LORA, HP).T.astype(BF16)
    vone = jnp.tile((jnp.arange(HEAD_SLOT) == V_DIM).astype(F32), N_HEADS).reshape(HP, 1)
    w_b = jnp.pad(w_b_out.reshape(N_HEADS, V_DIM, D_MODEL),
                  ((0, 0), (0, HEAD_SLOT - V_DIM), (0, 0))).reshape(HP, D_MODEL).astype(BF16)

    inv = 1.0 / (ROPE_THETA ** (jnp.arange(half, dtype=F32) * (2.0 / ROPE_DIM)))
    invf = jnp.pad(jnp.concatenate([inv, inv]), (NOPE_DIM, HEAD_SLOT - QK_DIM)).reshape(1, HEAD_SLOT)
    pos3 = positions.reshape(bsz, seq, 1)
    q_scale = (QK_DIM ** -0.5) * math.log2(math.e)

    act = jax.ShapeDtypeStruct((bsz, seq, HP), BF16)
    q, k, vt, gya, sgb = pl.pallas_call(
        functools.partial(_mix_in_kernel, q_scale=q_scale),
        grid=(bsz, seq // TS_MIX),
        in_specs=[_row_spec(TS_MIX, D_MODEL), _row_spec(TS_MIX, 1),
                  _const_spec((1, HEAD_SLOT)), _const_spec((1, D_MODEL)),
                  _const_spec((D_MODEL, c0)), _const_spec((D_MODEL, LAT_COLS)),
                  _const_spec((D_MODEL, D_MODEL)), _const_spec((D_MODEL, D_MODEL)),
                  _const_spec((3, CONV_DIM)), _const_spec((CONV_DIM, D_MODEL)),
                  _const_spec((1, Q_LORA)), _const_spec((Q_LORA, 2 * HP)),
                  _const_spec((1, KV_LORA)), _const_spec((KV_LORA, HP)),
                  _const_spec((HP, KV_LORA)), _const_spec((HP, 1))],
        out_specs=[_row_spec(TS_MIX, HP), _row_spec(TS_MIX, HP),
                   pl.BlockSpec((None, HP, TS_MIX), lambda b, s: (b, 0, s)),
                   _row_spec(TS_MIX, HP), _row_spec(TS_MIX, HP)],
        out_shape=[act, act, jax.ShapeDtypeStruct((bsz, HP, seq), BF16), act, act],
        scratch_shapes=[pltpu.VMEM((SUBLANES, CONV_DIM), F32),
                        pltpu.VMEM((TS_MIX + SUBLANES, CONV_DIM), F32)],
        compiler_params=_params(("arbitrary", "arbitrary")),
        name="mix_in",
    )(x, pos3, invf, row(g_mix_pre), w_abc, w_lat, w_ga, w_gb, conv_a_w.astype(F32),
      w_a_out.astype(BF16), row(g_q_lat), w_q, row(g_kv_lat), w_k, w_vt, vone)

    group = HEADS_PER_STEP * HEAD_SLOT
    head_cols = pl.BlockSpec((None, seq, group), lambda b, h: (b, 0, h))
    o = pl.pallas_call(
        _attn_kernel,
        grid=(bsz, N_HEADS // HEADS_PER_STEP),
        in_specs=[head_cols, head_cols,
                  pl.BlockSpec((None, group, seq), lambda b, h: (b, h, 0))],
        out_specs=head_cols,
        out_shape=act,
        scratch_shapes=[pltpu.VMEM((2, HEADS_PER_STEP, TKEY, TQ), F32),
                        pltpu.VMEM((HEADS_PER_STEP, 1, TQ), F32),
                        pltpu.VMEM((HEADS_PER_STEP, HEAD_SLOT, TQ), F32)],
        compiler_params=_params(("arbitrary", "arbitrary")),
        name="attn",
    )(q, k, vt)

    h1 = pl.pallas_call(
        _mix_out_kernel,
        grid=(bsz, seq // TS_OUT),
        in_specs=[_row_spec(TS_OUT, D_MODEL), _row_spec(TS_OUT, HP), _row_spec(TS_OUT, HP),
                  _row_spec(TS_OUT, HP), _const_spec((HP, D_MODEL)),
                  _const_spec((D_MODEL, D_MODEL)), _const_spec((1, D_MODEL))],
        out_specs=_row_spec(TS_OUT, D_MODEL),
        out_shape=jax.ShapeDtypeStruct(x.shape, F32),
        compiler_params=_params(("arbitrary", "arbitrary")),
        name="mix_out",
    )(x, o, gya, sgb, w_b, w_o.astype(BF16), row(g_mix_post))

    max_w = max(w for _, w in FFN_CHUNKS)
    h2 = pl.pallas_call(
        _ffn_kernel,
        grid=(bsz, seq // TS_FFN),
        in_specs=[_row_spec(TS_FFN, D_MODEL), _const_spec((1, D_MODEL)),
                  _const_spec((D_MODEL, 2 * D_FF)), _const_spec((3, 2 * D_FF)),
                  _const_spec((1, 2 * D_FF)), _const_spec((D_FF, D_MODEL)),
                  _const_spec((1, D_MODEL))],
        out_specs=_row_spec(TS_FFN, D_MODEL),
        out_shape=jax.ShapeDtypeStruct(x.shape, F32),
        scratch_shapes=[pltpu.VMEM((SUBLANES, 2 * D_FF), F32),
                        pltpu.VMEM((TS_FFN + SUBLANES, max_w), F32)],
        compiler_params=_params(("arbitrary", "arbitrary")),
        name="ffn",
    )(h1, row(g_ffn_pre), w_ffn_up.astype(BF16), conv_ffn_w.astype(F32), row(b_ffn_conv),
      w_ffn_down.astype(BF16), row(g_ffn_post))

    h3 = pl.pallas_call(
        _ple_kernel,
        grid=(bsz, seq // TS_PLE),
        in_specs=[_row_spec(TS_PLE, D_MODEL), _row_spec(TS_PLE, PLE_DIM),
                  _const_spec((PLE_DIM, D_MODEL)), _const_spec((D_MODEL, D_MODEL)),
                  _const_spec((1, D_MODEL))],
        out_specs=_row_spec(TS_PLE, D_MODEL),
        out_shape=jax.ShapeDtypeStruct(x.shape, F32),
        compiler_params=_params(("arbitrary", "arbitrary")),
        name="ple",
    )(h2, p, w_ple_proj.astype(BF16), w_ple_gate.astype(BF16), row(g_ple_post))
    return h3


def kernel(x, p, positions, g_mix_pre, w_in, conv_a_w, w_a_out, g_q_lat, w_q_up, g_kv_lat, w_kv_up, w_b_out, w_o, g_mix_post, g_ffn_pre, w_ffn_up, conv_ffn_w, b_ffn_conv, w_ffn_down, g_ffn_post, w_ple_proj, w_ple_gate, g_ple_post):
    h = x
    for i in range(g_mix_pre.shape[0]):
        h = _layer(h, p[i], positions, g_mix_pre[i], w_in[i], conv_a_w[i], w_a_out[i], g_q_lat[i],
                   w_q_up[i], g_kv_lat[i], w_kv_up[i], w_b_out[i], w_o[i], g_mix_post[i],
                   g_ffn_pre[i], w_ffn_up[i], conv_ffn_w[i], b_ffn_conv[i], w_ffn_down[i],
                   g_ffn_post[i], w_ple_proj[i], w_ple_gate[i], g_ple_post[i])
    return h
```

```python
import functools
import math

import jax
import jax.numpy as jnp
from jax.experimental import pallas as pl
from jax.experimental.pallas import tpu as pltpu

D_MODEL = 1024
CHUNK = 64
CONV_DIM = 512
N_HEADS = 8
NOPE_DIM = 64
ROPE_DIM = 32
V_DIM = 64
Q_LORA = 384
KV_LORA = 256
ROPE_THETA = 10000.0
D_FF = 2816
PLE_DIM = 256
EPS = 1e-6
QK_DIM = NOPE_DIM + ROPE_DIM

LANES = 128
SUBLANES = 8
HEAD_SLOT = LANES
HP = N_HEADS * HEAD_SLOT
LAT_COLS = Q_LORA + KV_LORA + 2 * LANES
VMEM_LIMIT = 56 * 1024 * 1024

TS_MIX = 512
TQ = 1024
TKEY = TQ // 2
HEADS_PER_STEP = 2
V_ROWS = 80
TS_OUT = 512
TS_FFN = 512
TS_PLE = 512
FFN_CHUNKS = ((0, 1024), (1024, 1024), (2048, 768))

NEG_BIG = -1e30
BF16 = jnp.bfloat16
F32 = jnp.float32


def _rms(x, g):
    return x * jax.lax.rsqrt(jnp.mean(x * x, axis=-1, keepdims=True) + EPS) * g


def _dot(a, b):
    return jnp.dot(a, b, preferred_element_type=F32)


def _const_spec(shape):
    return pl.BlockSpec(shape, lambda *_: (0,) * len(shape))


def _shifted_rows(stage_ref, carry_ref, pre, rows, width):
    stage_ref[0:SUBLANES, 0:width] = carry_ref[...]
    stage_ref[SUBLANES:SUBLANES + rows, 0:width] = pre
    x1 = stage_ref[SUBLANES - 1:SUBLANES - 1 + rows, 0:width]
    x2 = stage_ref[SUBLANES - 2:SUBLANES - 2 + rows, 0:width]
    carry_ref[...] = pre[rows - SUBLANES:rows, :]
    return x1, x2


def _mix_in_kernel(x_ref, pos_ref, invf_ref, g_pre_ref, w_abc_ref, w_lat_ref,
                   w_ga_ref, w_gb_ref, conv_w_ref, w_a_out_ref, g_q_ref, w_q_ref,
                   g_kv_ref, w_k_ref, w_v_ref, vone_ref,
                   q_ref, k_ref, v_ref, gya_ref, sgb_ref,
                   carry_ref, stage_ref, *, q_scale):
    ts = x_ref.shape[0]

    @pl.when(pl.program_id(1) == 0)
    def _():
        carry_ref[...] = jnp.zeros_like(carry_ref)

    u = _rms(x_ref[...], g_pre_ref[...]).astype(BF16)

    abc = _dot(u, w_abc_ref[...])
    a_b = abc[:, 0:CONV_DIM]
    ca = abc[:, CONV_DIM:2 * CONV_DIM] * abc[:, 2 * CONV_DIM:3 * CONV_DIM]
    x1, x2 = _shifted_rows(stage_ref, carry_ref, ca, ts, CONV_DIM)
    cw = conv_w_ref[...]
    cv = x2 * cw[0:1, :] + x1 * cw[1:2, :] + ca * cw[2:3, :]
    ya = _dot((a_b * cv).astype(BF16), w_a_out_ref[...])
    gya_ref[...] = (jax.nn.sigmoid(_dot(u, w_ga_ref[...])) * ya).astype(BF16)
    sgb_ref[...] = jax.nn.sigmoid(_dot(u, w_gb_ref[...])).astype(BF16)

    lat = _dot(u, w_lat_ref[...])
    q_lat = lat[:, 0:Q_LORA]
    kv_lat = lat[:, Q_LORA:Q_LORA + KV_LORA]
    kr = lat[:, Q_LORA + KV_LORA:Q_LORA + KV_LORA + LANES]
    kr_rot = lat[:, Q_LORA + KV_LORA + LANES:LAT_COLS]

    ang = pos_ref[...].astype(F32) * invf_ref[...]
    cos = jnp.cos(ang)
    sin = jnp.sin(ang)

    qn = _rms(q_lat, g_q_ref[...]).astype(BF16)
    qq = _dot(qn, w_q_ref[...])
    kvn = _rms(kv_lat, g_kv_ref[...]).astype(BF16)
    kk = _dot(kvn, w_k_ref[...])
    vt = jax.lax.dot_general(w_v_ref[...], kvn, (((1,), (1,)), ((), ())),
                             preferred_element_type=F32) + vone_ref[...]
    v_ref[...] = vt.astype(BF16)

    k_pe = kr * cos + kr_rot * sin
    cos_q = cos * q_scale
    sin_q = sin * q_scale
    for h in range(N_HEADS):
        sl = slice(h * HEAD_SLOT, (h + 1) * HEAD_SLOT)
        q_ref[:, sl] = (qq[:, sl] * cos_q
                        + qq[:, HP + h * HEAD_SLOT:HP + (h + 1) * HEAD_SLOT] * sin_q).astype(BF16)
        k_ref[:, sl] = (kk[:, sl] + k_pe).astype(BF16)


def _attn_kernel(q_ref, k_ref, vt_ref, o_ref, s_ref, m_ref, acc_ref, qt_ref):
    tk, tq = s_ref.shape[-2:]
    assert tq == 2 * tk
    nq = q_ref.shape[0] // tq
    heads = range(HEADS_PER_STEP)
    lanes = lambda h: slice(h * HEAD_SLOT, (h + 1) * HEAD_SLOT)
    v_rows = lambda h: slice(h * HEAD_SLOT, h * HEAD_SLOT + V_ROWS)
    key_tile = lambda j: pl.ds(pl.multiple_of(j * tk, tk), tk)
    every = slice(0, tq)
    later = slice(tk, tq)

    def transpose_queries(qi):
        for h in heads:
            qt_ref[qi % 2, h] = q_ref[pl.ds(pl.multiple_of(qi * tq, tq), tq), lanes(h)].T

    def scores(qi, j, buf):
        for h in heads:
            s_ref[buf, h] = _dot(k_ref[key_tile(j), lanes(h)], qt_ref[qi % 2, h])

    def scores_later(qi, buf):
        for h in heads:
            s_ref[buf, h, :, later] = _dot(k_ref[key_tile(2 * qi + 1), lanes(h)],
                                           qt_ref[qi % 2, h, :, later])

    def accumulate(j, buf, cols, masked):
        width = cols.stop - cols.start
        if masked:
            ck = jax.lax.broadcasted_iota(jnp.int32, (tk, width), 0) // CHUNK
            cq = jax.lax.broadcasted_iota(jnp.int32, (tk, width), 1) // CHUNK
            load = lambda h: jnp.where(ck <= cq, s_ref[buf, h, :, cols], NEG_BIG)
        else:
            load = lambda h: s_ref[buf, h, :, cols]
        for h in heads:
            m = m_ref[h, :, cols]
            m_new = jnp.maximum(m, jnp.max(load(h), axis=0, keepdims=True))
            p = jnp.exp2(load(h) - m_new)
            pv = _dot(vt_ref[v_rows(h), key_tile(j)], p.astype(BF16))
            acc_ref[h, 0:V_ROWS, cols] = acc_ref[h, 0:V_ROWS, cols] * jnp.exp2(m - m_new) + pv
            m_ref[h, :, cols] = m_new

    def query_tile(qi, _):
        next_qi = jnp.minimum(qi + 1, nq - 1)
        m_ref[...] = jnp.full_like(m_ref, NEG_BIG)
        acc_ref[...] = jnp.zeros_like(acc_ref)

        @pl.when(qi + 1 < nq)
        def _():
            transpose_queries(qi + 1)

        def tile_pair(i, _):
            scores(qi, 2 * i + 1, 1)
            accumulate(2 * i, 0, every, False)
            scores(qi, 2 * i + 2, 0)
            accumulate(2 * i + 1, 1, every, False)
            return 0

        jax.lax.fori_loop(0, qi, tile_pair, 0)

        scores_later(qi, 1)
        accumulate(2 * qi, 0, every, True)
        scores(next_qi, 0, 0)
        accumulate(2 * qi + 1, 1, later, True)

        for h in heads:
            acc = acc_ref[h]
            o_ref[pl.ds(pl.multiple_of(qi * tq, tq), tq), lanes(h)] = (
                acc / acc[V_DIM:V_DIM + 1, :]).T.astype(BF16)
        return 0

    transpose_queries(0)
    scores(0, 0, 0)
    jax.lax.fori_loop(0, nq, query_tile, 0)


def _mix_out_kernel(x_ref, o_ref, gya_ref, sgb_ref, w_b_ref, w_o_ref, g_ref, h_ref):
    yb = _dot(o_ref[...], w_b_ref[...])
    mixed = gya_ref[...].astype(F32) + sgb_ref[...].astype(F32) * yb
    mo = _dot(mixed.astype(BF16), w_o_ref[...])
    h_ref[...] = x_ref[...] + _rms(mo, g_ref[...])


def _ffn_kernel(h_ref, g_pre_ref, w_up_ref, conv_w_ref, b_ref, w_down_ref, g_post_ref,
                out_ref, carry_ref, stage_ref):
    ts = h_ref.shape[0]

    @pl.when(pl.program_id(1) == 0)
    def _():
        carry_ref[...] = jnp.zeros_like(carry_ref)

    h = h_ref[...]
    u = _rms(h, g_pre_ref[...]).astype(BF16)

    def conv_cols(col, width):
        cols = slice(col, col + width)
        pre = _dot(u, w_up_ref[:, cols])
        x1, x2 = _shifted_rows(stage_ref, carry_ref.at[:, cols], pre, ts, width)
        cw = conv_w_ref[:, cols]
        return x2 * cw[0:1, :] + x1 * cw[1:2, :] + pre * cw[2:3, :] + b_ref[:, cols]

    f = jnp.zeros((ts, D_MODEL), F32)
    for off, width in FFN_CHUNKS:
        act = jax.nn.gelu(conv_cols(off, width), approximate=True)
        val = conv_cols(D_FF + off, width)
        f = f + _dot((act * val).astype(BF16), w_down_ref[off:off + width, :])
    out_ref[...] = h + _rms(f, g_post_ref[...])


def _ple_kernel(h_ref, p_ref, w_proj_ref, w_gate_ref, g_ref, out_ref):
    h = h_ref[...]
    e = _dot(p_ref[...].astype(BF16), w_proj_ref[...])
    g = jax.nn.sigmoid(_dot(h.astype(BF16), w_gate_ref[...]))
    out_ref[...] = h + _rms(e * g, g_ref[...])


def _row_spec(ts, width):
    return pl.BlockSpec((None, ts, width), lambda b, s: (b, s, 0))


def _params(sem):
    return pltpu.CompilerParams(dimension_semantics=sem, vmem_limit_bytes=VMEM_LIMIT)


def _layer(x, p, positions, g_mix_pre, w_in, conv_a_w, w_a_out, g_q_lat, w_q_up, g_kv_lat,
           w_kv_up, w_b_out, w_o, g_mix_post, g_ffn_pre, w_ffn_up, conv_ffn_w, b_ffn_conv,
           w_ffn_down, g_ffn_post, w_ple_proj, w_ple_gate, g_ple_post):
    bsz, seq, _ = x.shape
    row = lambda v: v.reshape(1, -1).astype(F32)

    c0 = 3 * CONV_DIM
    c1 = c0 + Q_LORA
    c2 = c1 + KV_LORA
    c3 = c2 + ROPE_DIM
    c4 = c3 + D_MODEL
    half = ROPE_DIM // 2
    w_kr = w_in[:, c2:c3]
    w_kr_rot = jnp.concatenate([-w_kr[:, half:], w_kr[:, :half]], axis=1)
    pad_slot = lambda w: jnp.pad(w, ((0, 0), (NOPE_DIM, HEAD_SLOT - NOPE_DIM - ROPE_DIM)))
    w_abc = w_in[:, 0:c0].astype(BF16)
    w_lat = jnp.concatenate([w_in[:, c0:c2], pad_slot(w_kr), pad_slot(w_kr_rot)], axis=1).astype(BF16)
    w_ga = w_in[:, c3:c4].astype(BF16)
    w_gb = w_in[:, c4:].astype(BF16)

    wq = w_q_up.reshape(Q_LORA, N_HEADS, QK_DIM)
    wq_plain = jnp.pad(wq, ((0, 0), (0, 0), (0, HEAD_SLOT - QK_DIM)))
    wq_rot = jnp.concatenate([-wq[:, :, NOPE_DIM + half:], wq[:, :, NOPE_DIM:NOPE_DIM + half]], axis=2)
    wq_rot = jnp.pad(wq_rot, ((0, 0), (0, 0), (NOPE_DIM, HEAD_SLOT - QK_DIM)))
    w_q = jnp.concatenate([wq_plain.reshape(Q_LORA, HP), wq_rot.reshape(Q_LORA, HP)], axis=1).astype(BF16)

    wkv = w_kv_up.reshape(KV_LORA, N_HEADS, NOPE_DIM + V_DIM)
    w_k = jnp.pad(wkv[:, :, :NOPE_DIM], ((0, 0), (0, 0), (0, HEAD_SLOT - NOPE_DIM))).reshape(KV_LORA, HP).astype(BF16)
    w_vt = jnp.pad(wkv[:, :, NOPE_DIM:], ((0, 0), (0, 0), (0, HEAD_SLOT - V_DIM))).reshape(KV_LORA, HP).T.astype(BF16)
    vone = jnp.tile((jnp.arange(HEAD_SLOT) == V_DIM).astype(F32), N_HEADS).reshape(HP, 1)
    w_b = jnp.pad(w_b_out.reshape(N_HEADS, V_DIM, D_MODEL),
                  ((0, 0), (0, HEAD_SLOT - V_DIM), (0, 0))).reshape(HP, D_MODEL).astype(BF16)

    inv = 1.0 / (ROPE_THETA ** (jnp.arange(half, dtype=F32) * (2.0 / ROPE_DIM)))
    invf = jnp.pad(jnp.concatenate([inv, inv]), (NOPE_DIM, HEAD_SLOT - QK_DIM)).reshape(1, HEAD_SLOT)
    pos3 = positions.reshape(bsz, seq, 1)
    q_scale = (QK_DIM ** -0.5) * math.log2(math.e)

    act = jax.ShapeDtypeStruct((bsz, seq, HP), BF16)
    q, k, vt, gya, sgb = pl.pallas_call(
        functools.partial(_mix_in_kernel, q_scale=q_scale),
        grid=(bsz, seq // TS_MIX),
        in_specs=[_row_spec(TS_MIX, D_MODEL), _row_spec(TS_MIX, 1),
                  _const_spec((1, HEAD_SLOT)), _const_spec((1, D_MODEL)),
                  _const_spec((D_MODEL, c0)), _const_spec((D_MODEL, LAT_COLS)),
                  _const_spec((D_MODEL, D_MODEL)), _const_spec((D_MODEL, D_MODEL)),
                  _const_spec((3, CONV_DIM)), _const_spec((CONV_DIM, D_MODEL)),
                  _const_spec((1, Q_LORA)), _const_spec((Q_LORA, 2 * HP)),
                  _const_spec((1, KV_LORA)), _const_spec((KV_LORA, HP)),
                  _const_spec((HP, KV_LORA)), _const_spec((HP, 1))],
        out_specs=[_row_spec(TS_MIX, HP), _row_spec(TS_MIX, HP),
                   pl.BlockSpec((None, HP, TS_MIX), lambda b, s: (b, 0, s)),
                   _row_spec(TS_MIX, HP), _row_spec(TS_MIX, HP)],
        out_shape=[act, act, jax.ShapeDtypeStruct((bsz, HP, seq), BF16), act, act],
        scratch_shapes=[pltpu.VMEM((SUBLANES, CONV_DIM), F32),
                        pltpu.VMEM((TS_MIX + SUBLANES, CONV_DIM), F32)],
        compiler_params=_params(("arbitrary", "arbitrary")),
        name="mix_in",
    )(x, pos3, invf, row(g_mix_pre), w_abc, w_lat, w_ga, w_gb, conv_a_w.astype(F32),
      w_a_out.astype(BF16), row(g_q_lat), w_q, row(g_kv_lat), w_k, w_vt, vone)

    group = HEADS_PER_STEP * HEAD_SLOT
    head_cols = pl.BlockSpec((None, seq, group), lambda b, h: (b, 0, h))
    o = pl.pallas_call(
        _attn_kernel,
        grid=(bsz, N_HEADS // HEADS_PER_STEP),
        in_specs=[head_cols, head_cols,
                  pl.BlockSpec((None, group, seq), lambda b, h: (b, h, 0))],
        out_specs=head_cols,
        out_shape=act,
        scratch_shapes=[pltpu.VMEM((2, HEADS_PER_STEP, TKEY, TQ), F32),
                        pltpu.VMEM((HEADS_PER_STEP, 1, TQ), F32),
                        pltpu.VMEM((HEADS_PER_STEP, HEAD_SLOT, TQ), F32),
                        pltpu.VMEM((2, HEADS_PER_STEP, HEAD_SLOT, TQ), BF16)],
        compiler_params=_params(("arbitrary", "arbitrary")),
        name="attn",
    )(q, k, vt)

    h1 = pl.pallas_call(
        _mix_out_kernel,
        grid=(bsz, seq // TS_OUT),
        in_specs=[_row_spec(TS_OUT, D_MODEL), _row_spec(TS_OUT, HP), _row_spec(TS_OUT, HP),
                  _row_spec(TS_OUT, HP), _const_spec((HP, D_MODEL)),
                  _const_spec((D_MODEL, D_MODEL)), _const_spec((1, D_MODEL))],
        out_specs=_row_spec(TS_OUT, D_MODEL),
        out_shape=jax.ShapeDtypeStruct(x.shape, F32),
        compiler_params=_params(("arbitrary", "arbitrary")),
        name="mix_out",
    )(x, o, gya, sgb, w_b, w_o.astype(BF16), row(g_mix_post))

    max_w = max(w for _, w in FFN_CHUNKS)
    h2 = pl.pallas_call(
        _ffn_kernel,
        grid=(bsz, seq // TS_FFN),
        in_specs=[_row_spec(TS_FFN, D_MODEL), _const_spec((1, D_MODEL)),
                  _const_spec((D_MODEL, 2 * D_FF)), _const_spec((3, 2 * D_FF)),
                  _const_spec((1, 2 * D_FF)), _const_spec((D_FF, D_MODEL)),
                  _const_spec((1, D_MODEL))],
        out_specs=_row_spec(TS_FFN, D_MODEL),
        out_shape=jax.ShapeDtypeStruct(x.shape, F32),
        scratch_shapes=[pltpu.VMEM((SUBLANES, 2 * D_FF), F32),
                        pltpu.VMEM((TS_FFN + SUBLANES, max_w), F32)],
        compiler_params=_params(("arbitrary", "arbitrary")),
        name="ffn",
    )(h1, row(g_ffn_pre), w_ffn_up.astype(BF16), conv_ffn_w.astype(F32), row(b_ffn_conv),
      w_ffn_down.astype(BF16), row(g_ffn_post))

    h3 = pl.pallas_call(
        _ple_kernel,
        grid=(bsz, seq // TS_PLE),
        in_specs=[_row_spec(TS_PLE, D_MODEL), _row_spec(TS_PLE, PLE_DIM),
                  _const_spec((PLE_DIM, D_MODEL)), _const_spec((D_MODEL, D_MODEL)),
                  _const_spec((1, D_MODEL))],
        out_specs=_row_spec(TS_PLE, D_MODEL),
        out_shape=jax.ShapeDtypeStruct(x.shape, F32),
        compiler_params=_params(("arbitrary", "arbitrary")),
        name="ple",
    )(h2, p, w_ple_proj.astype(BF16), w_ple_gate.astype(BF16), row(g_ple_post))
    return h3


def kernel(x, p, positions, g_mix_pre, w_in, conv_a_w, w_a_out, g_q_lat, w_q_up, g_kv_lat, w_kv_up, w_b_out, w_o, g_mix_post, g_ffn_pre, w_ffn_up, conv_ffn_w, b_ffn_conv, w_ffn_down, g_ffn_post, w_ple_proj, w_ple_gate, g_ple_post):
    h = x
    for i in range(g_mix_pre.shape[0]):
        h = _layer(h, p[i], positions, g_mix_pre[i], w_in[i], conv_a_w[i], w_a_out[i], g_q_lat[i],
                   w_q_up[i], g_kv_lat[i], w_kv_up[i], w_b_out[i], w_o[i], g_mix_post[i],
                   g_ffn_pre[i], w_ffn_up[i], conv_ffn_w[i], b_ffn_conv[i], w_ffn_down[i],
                   g_ffn_post[i], w_ple_proj[i], w_ple_gate[i], g_ple_post[i])
    return h
```

```python
import functools
import math

import jax
import jax.numpy as jnp
from jax.experimental import pallas as pl
from jax.experimental.pallas import tpu as pltpu

D_MODEL = 1024
CHUNK = 64
CONV_DIM = 512
N_HEADS = 8
NOPE_DIM = 64
ROPE_DIM = 32
V_DIM = 64
Q_LORA = 384
KV_LORA = 256
ROPE_THETA = 10000.0
D_FF = 2816
PLE_DIM = 256
EPS = 1e-6
QK_DIM = NOPE_DIM + ROPE_DIM

LANES = 128
SUBLANES = 8
HEAD_SLOT = LANES
HP = N_HEADS * HEAD_SLOT
LAT_COLS = Q_LORA + KV_LORA + 2 * LANES
VMEM_LIMIT = 56 * 1024 * 1024

TS_MIX = 512
TQ = 1024
TKEY = TQ // 2
HEADS_PER_STEP = 2
V_ROWS = 80
TS_OUT = 512
TS_FFN = 512
TS_PLE = 512
FFN_CHUNKS = ((0, 1024), (1024, 1024), (2048, 768))

NEG_BIG = -1e30
BF16 = jnp.bfloat16
F32 = jnp.float32


def _rms(x, g):
    return x * jax.lax.rsqrt(jnp.mean(x * x, axis=-1, keepdims=True) + EPS) * g


def _dot(a, b):
    return jnp.dot(a, b, preferred_element_type=F32)


def _const_spec(shape):
    return pl.BlockSpec(shape, lambda *_: (0,) * len(shape))


def _shifted_rows(stage_ref, carry_ref, pre, rows, width):
    stage_ref[0:SUBLANES, 0:width] = carry_ref[...]
    stage_ref[SUBLANES:SUBLANES + rows, 0:width] = pre
    x1 = stage_ref[SUBLANES - 1:SUBLANES - 1 + rows, 0:width]
    x2 = stage_ref[SUBLANES - 2:SUBLANES - 2 + rows, 0:width]
    carry_ref[...] = pre[rows - SUBLANES:rows, :]
    return x1, x2


def _mix_in_kernel(x_ref, pos_ref, invf_ref, g_pre_ref, w_abc_ref, w_lat_ref,
                   w_ga_ref, w_gb_ref, conv_w_ref, w_a_out_ref, g_q_ref, w_q_ref,
                   g_kv_ref, w_k_ref, w_v_ref, vone_ref,
                   q_ref, k_ref, v_ref, gya_ref, sgb_ref,
                   carry_ref, stage_ref, *, q_scale):
    ts = x_ref.shape[0]

    @pl.when(pl.program_id(1) == 0)
    def _():
        carry_ref[...] = jnp.zeros_like(carry_ref)

    u = _rms(x_ref[...], g_pre_ref[...]).astype(BF16)

    abc = _dot(u, w_abc_ref[...])
    a_b = abc[:, 0:CONV_DIM]
    ca = abc[:, CONV_DIM:2 * CONV_DIM] * abc[:, 2 * CONV_DIM:3 * CONV_DIM]
    x1, x2 = _shifted_rows(stage_ref, carry_ref, ca, ts, CONV_DIM)
    cw = conv_w_ref[...]
    cv = x2 * cw[0:1, :] + x1 * cw[1:2, :] + ca * cw[2:3, :]
    ya = _dot((a_b * cv).astype(BF16), w_a_out_ref[...])
    gya_ref[...] = (jax.nn.sigmoid(_dot(u, w_ga_ref[...])) * ya).astype(BF16)
    sgb_ref[...] = jax.nn.sigmoid(_dot(u, w_gb_ref[...])).astype(BF16)

    lat = _dot(u, w_lat_ref[...])
    q_lat = lat[:, 0:Q_LORA]
    kv_lat = lat[:, Q_LORA:Q_LORA + KV_LORA]
    kr = lat[:, Q_LORA + KV_LORA:Q_LORA + KV_LORA + LANES]
    kr_rot = lat[:, Q_LORA + KV_LORA + LANES:LAT_COLS]

    ang = pos_ref[...].astype(F32) * invf_ref[...]
    cos = jnp.cos(ang)
    sin = jnp.sin(ang)

    qn = _rms(q_lat, g_q_ref[...]).astype(BF16)
    qq = _dot(qn, w_q_ref[...])
    kvn = _rms(kv_lat, g_kv_ref[...]).astype(BF16)
    kk = _dot(kvn, w_k_ref[...])
    vt = jax.lax.dot_general(w_v_ref[...], kvn, (((1,), (1,)), ((), ())),
                             preferred_element_type=F32) + vone_ref[...]
    v_ref[...] = vt.astype(BF16)

    k_pe = kr * cos + kr_rot * sin
    cos_q = cos * q_scale
    sin_q = sin * q_scale
    for h in range(N_HEADS):
        sl = slice(h * HEAD_SLOT, (h + 1) * HEAD_SLOT)
        q_ref[:, sl] = (qq[:, sl] * cos_q
                        + qq[:, HP + h * HEAD_SLOT:HP + (h + 1) * HEAD_SLOT] * sin_q).astype(BF16)
        k_ref[:, sl] = (kk[:, sl] + k_pe).astype(BF16)


def _attn_kernel(q_ref, k_ref, vt_ref, o_ref, s_ref, smax_ref, m_ref, acc_ref, qt_ref):
    tk, tq = s_ref.shape[-2:]
    assert tq == 2 * tk
    nq = q_ref.shape[0] // tq
    heads = range(HEADS_PER_STEP)
    lanes = lambda h: slice(h * HEAD_SLOT, (h + 1) * HEAD_SLOT)
    v_rows = lambda h: slice(h * HEAD_SLOT, h * HEAD_SLOT + V_ROWS)
    key_tile = lambda j: pl.ds(pl.multiple_of(j * tk, tk), tk)
    every = slice(0, tq)
    later = slice(tk, tq)

    def transpose_queries(qi):
        for h in heads:
            qt_ref[qi % 2, h] = q_ref[pl.ds(pl.multiple_of(qi * tq, tq), tq), lanes(h)].T

    def scores(qi, j, buf):
        for h in heads:
            sc = _dot(k_ref[key_tile(j), lanes(h)], qt_ref[qi % 2, h])
            s_ref[buf, h] = sc
            smax_ref[buf, h] = jnp.max(sc, axis=0, keepdims=True)

    def scores_later(qi, buf):
        for h in heads:
            s_ref[buf, h, :, later] = _dot(k_ref[key_tile(2 * qi + 1), lanes(h)],
                                           qt_ref[qi % 2, h, :, later])

    def accumulate(j, buf, cols, masked):
        width = cols.stop - cols.start
        if masked:
            ck = jax.lax.broadcasted_iota(jnp.int32, (tk, width), 0) // CHUNK
            cq = jax.lax.broadcasted_iota(jnp.int32, (tk, width), 1) // CHUNK
            load = lambda h: jnp.where(ck <= cq, s_ref[buf, h, :, cols], NEG_BIG)
            tile_max = lambda h: jnp.max(load(h), axis=0, keepdims=True)
        else:
            load = lambda h: s_ref[buf, h, :, cols]
            tile_max = lambda h: smax_ref[buf, h, :, cols]
        for h in heads:
            m = m_ref[h, :, cols]
            m_new = jnp.maximum(m, tile_max(h))
            p = jnp.exp2(load(h) - m_new)
            pv = _dot(vt_ref[v_rows(h), key_tile(j)], p.astype(BF16))
            acc_ref[h, 0:V_ROWS, cols] = acc_ref[h, 0:V_ROWS, cols] * jnp.exp2(m - m_new) + pv
            m_ref[h, :, cols] = m_new

    def query_tile(qi, _):
        next_qi = jnp.minimum(qi + 1, nq - 1)
        m_ref[...] = jnp.full_like(m_ref, NEG_BIG)
        acc_ref[...] = jnp.zeros_like(acc_ref)

        @pl.when(qi + 1 < nq)
        def _():
            transpose_queries(qi + 1)

        def tile_pair(i, _):
            scores(qi, 2 * i + 1, 1)
            accumulate(2 * i, 0, every, False)
            scores(qi, 2 * i + 2, 0)
            accumulate(2 * i + 1, 1, every, False)
            return 0

        jax.lax.fori_loop(0, qi, tile_pair, 0)

        scores_later(qi, 1)
        accumulate(2 * qi, 0, every, True)
        scores(next_qi, 0, 0)
        accumulate(2 * qi + 1, 1, later, True)

        for h in heads:
            acc = acc_ref[h]
            o_ref[pl.ds(pl.multiple_of(qi * tq, tq), tq), lanes(h)] = (
                acc / acc[V_DIM:V_DIM + 1, :]).T.astype(BF16)
        return 0

    transpose_queries(0)
    scores(0, 0, 0)
    jax.lax.fori_loop(0, nq, query_tile, 0)


def _mix_out_kernel(x_ref, o_ref, gya_ref, sgb_ref, w_b_ref, w_o_ref, g_ref, h_ref):
    yb = _dot(o_ref[...], w_b_ref[...])
    mixed = gya_ref[...].astype(F32) + sgb_ref[...].astype(F32) * yb
    mo = _dot(mixed.astype(BF16), w_o_ref[...])
    h_ref[...] = x_ref[...] + _rms(mo, g_ref[...])


def _ffn_kernel(h_ref, g_pre_ref, w_up_ref, conv_w_ref, b_ref, w_down_ref, g_post_ref,
                out_ref, carry_ref, stage_ref):
    ts = h_ref.shape[0]

    @pl.when(pl.program_id(1) == 0)
    def _():
        carry_ref[...] = jnp.zeros_like(carry_ref)

    h = h_ref[...]
    u = _rms(h, g_pre_ref[...]).astype(BF16)

    def conv_cols(col, width):
        cols = slice(col, col + width)
        pre = _dot(u, w_up_ref[:, cols])
        x1, x2 = _shifted_rows(stage_ref, carry_ref.at[:, cols], pre, ts, width)
        cw = conv_w_ref[:, cols]
        return x2 * cw[0:1, :] + x1 * cw[1:2, :] + pre * cw[2:3, :] + b_ref[:, cols]

    f = jnp.zeros((ts, D_MODEL), F32)
    for off, width in FFN_CHUNKS:
        act = jax.nn.gelu(conv_cols(off, width), approximate=True)
        val = conv_cols(D_FF + off, width)
        f = f + _dot((act * val).astype(BF16), w_down_ref[off:off + width, :])
    out_ref[...] = h + _rms(f, g_post_ref[...])


def _ple_kernel(h_ref, p_ref, w_proj_ref, w_gate_ref, g_ref, out_ref):
    h = h_ref[...]
    e = _dot(p_ref[...].astype(BF16), w_proj_ref[...])
    g = jax.nn.sigmoid(_dot(h.astype(BF16), w_gate_ref[...]))
    out_ref[...] = h + _rms(e * g, g_ref[...])


def _row_spec(ts, width):
    return pl.BlockSpec((None, ts, width), lambda b, s: (b, s, 0))


def _params(sem):
    return pltpu.CompilerParams(dimension_semantics=sem, vmem_limit_bytes=VMEM_LIMIT)


def _layer(x, p, positions, g_mix_pre, w_in, conv_a_w, w_a_out, g_q_lat, w_q_up, g_kv_lat,
           w_kv_up, w_b_out, w_o, g_mix_post, g_ffn_pre, w_ffn_up, conv_ffn_w, b_ffn_conv,
           w_ffn_down, g_ffn_post, w_ple_proj, w_ple_gate, g_ple_post):
    bsz, seq, _ = x.shape
    row = lambda v: v.reshape(1, -1).astype(F32)

    c0 = 3 * CONV_DIM
    c1 = c0 + Q_LORA
    c2 = c1 + KV_LORA
    c3 = c2 + ROPE_DIM
    c4 = c3 + D_MODEL
    half = ROPE_DIM // 2
    w_kr = w_in[:, c2:c3]
    w_kr_rot = jnp.concatenate([-w_kr[:, half:], w_kr[:, :half]], axis=1)
    pad_slot = lambda w: jnp.pad(w, ((0, 0), (NOPE_DIM, HEAD_SLOT - NOPE_DIM - ROPE_DIM)))
    w_abc = w_in[:, 0:c0].astype(BF16)
    w_lat = jnp.concatenate([w_in[:, c0:c2], pad_slot(w_kr), pad_slot(w_kr_rot)], axis=1).astype(BF16)
    w_ga = w_in[:, c3:c4].astype(BF16)
    w_gb = w_in[:, c4:].astype(BF16)

    wq = w_q_up.reshape(Q_LORA, N_HEADS, QK_DIM)
    wq_plain = jnp.pad(wq, ((0, 0), (0, 0), (0, HEAD_SLOT - QK_DIM)))
    wq_rot = jnp.concatenate([-wq[:, :, NOPE_DIM + half:], wq[:, :, NOPE_DIM:NOPE_DIM + half]], axis=2)
    wq_rot = jnp.pad(wq_rot, ((0, 0), (0, 0), (NOPE_DIM, HEAD_SLOT - QK_DIM)))
    w_q = jnp.concatenate([wq_plain.reshape(Q_LORA, HP), wq_rot.reshape(Q_LORA, HP)], axis=1).astype(BF16)

    wkv = w_kv_up.reshape(KV_LORA, N_HEADS, NOPE_DIM + V_DIM)
    w_k = jnp.pad(wkv[:, :, :NOPE_DIM], ((0, 0), (0, 0), (0, HEAD_SLOT - NOPE_DIM))).reshape(KV_LORA, HP).astype(BF16)
    w_vt = jnp.pad(wkv[:, :, NOPE_DIM:], ((0, 0), (0, 0), (0, HEAD_SLOT - V_DIM))).reshape(KV_LORA, HP).T.astype(BF16)
    vone = jnp.tile((jnp.arange(HEAD_SLOT) == V_DIM).astype(F32), N_HEADS).reshape(HP, 1)
    w_b = jnp.pad(w_b_out.reshape(N_HEADS, V_DIM, D_MODEL),
                  ((0, 0), (0, HEAD_SLOT - V_DIM), (0, 0))).reshape(HP, D_MODEL).astype(BF16)

    inv = 1.0 / (ROPE_THETA ** (jnp.arange(half, dtype=F32) * (2.0 / ROPE_DIM)))
    invf = jnp.pad(jnp.concatenate([inv, inv]), (NOPE_DIM, HEAD_SLOT - QK_DIM)).reshape(1, HEAD_SLOT)
    pos3 = positions.reshape(bsz, seq, 1)
    q_scale = (QK_DIM ** -0.5) * math.log2(math.e)

    act = jax.ShapeDtypeStruct((bsz, seq, HP), BF16)
    q, k, vt, gya, sgb = pl.pallas_call(
        functools.partial(_mix_in_kernel, q_scale=q_scale),
        grid=(bsz, seq // TS_MIX),
        in_specs=[_row_spec(TS_MIX, D_MODEL), _row_spec(TS_MIX, 1),
                  _const_spec((1, HEAD_SLOT)), _const_spec((1, D_MODEL)),
                  _const_spec((D_MODEL, c0)), _const_spec((D_MODEL, LAT_COLS)),
                  _const_spec((D_MODEL, D_MODEL)), _const_spec((D_MODEL, D_MODEL)),
                  _const_spec((3, CONV_DIM)), _const_spec((CONV_DIM, D_MODEL)),
                  _const_spec((1, Q_LORA)), _const_spec((Q_LORA, 2 * HP)),
                  _const_spec((1, KV_LORA)), _const_spec((KV_LORA, HP)),
                  _const_spec((HP, KV_LORA)), _const_spec((HP, 1))],
        out_specs=[_row_spec(TS_MIX, HP), _row_spec(TS_MIX, HP),
                   pl.BlockSpec((None, HP, TS_MIX), lambda b, s: (b, 0, s)),
                   _row_spec(TS_MIX, HP), _row_spec(TS_MIX, HP)],
        out_shape=[act, act, jax.ShapeDtypeStruct((bsz, HP, seq), BF16), act, act],
        scratch_shapes=[pltpu.VMEM((SUBLANES, CONV_DIM), F32),
                        pltpu.VMEM((TS_MIX + SUBLANES, CONV_DIM), F32)],
        compiler_params=_params(("arbitrary", "arbitrary")),
        name="mix_in",
    )(x, pos3, invf, row(g_mix_pre), w_abc, w_lat, w_ga, w_gb, conv_a_w.astype(F32),
      w_a_out.astype(BF16), row(g_q_lat), w_q, row(g_kv_lat), w_k, w_vt, vone)

    group = HEADS_PER_STEP * HEAD_SLOT
    head_cols = pl.BlockSpec((None, seq, group), lambda b, h: (b, 0, h))
    o = pl.pallas_call(
        _attn_kernel,
        grid=(bsz, N_HEADS // HEADS_PER_STEP),
        in_specs=[head_cols, head_cols,
                  pl.BlockSpec((None, group, seq), lambda b, h: (b, h, 0))],
        out_specs=head_cols,
        out_shape=act,
        scratch_shapes=[pltpu.VMEM((2, HEADS_PER_STEP, TKEY, TQ), F32),
                        pltpu.VMEM((2, HEADS_PER_STEP, 1, TQ), F32),
                        pltpu.VMEM((HEADS_PER_STEP, 1, TQ), F32),
                        pltpu.VMEM((HEADS_PER_STEP, HEAD_SLOT, TQ), F32),
                        pltpu.VMEM((2, HEADS_PER_STEP, HEAD_SLOT, TQ), BF16)],
        compiler_params=_params(("arbitrary", "arbitrary")),
        name="attn",
    )(q, k, vt)

    h1 = pl.pallas_call(
        _mix_out_kernel,
        grid=(bsz, seq // TS_OUT),
        in_specs=[_row_spec(TS_OUT, D_MODEL), _row_spec(TS_OUT, HP), _row_spec(TS_OUT, HP),
                  _row_spec(TS_OUT, HP), _const_spec((HP, D_MODEL)),
                  _const_spec((D_MODEL, D_MODEL)), _const_spec((1, D_MODEL))],
        out_specs=_row_spec(TS_OUT, D_MODEL),
        out_shape=jax.ShapeDtypeStruct(x.shape, F32),
        compiler_params=_params(("arbitrary", "arbitrary")),
        name="mix_out",
    )(x, o, gya, sgb, w_b, w_o.astype(BF16), row(g_mix_post))

    max_w = max(w for _, w in FFN_CHUNKS)
    h2 = pl.pallas_call(
        _ffn_kernel,
        grid=(bsz, seq // TS_FFN),
        in_specs=[_row_spec(TS_FFN, D_MODEL), _const_spec((1, D_MODEL)),
                  _const_spec((D_MODEL, 2 * D_FF)), _const_spec((3, 2 * D_FF)),
                  _const_spec((1, 2 * D_FF)), _const_spec((D_FF, D_MODEL)),
                  _const_spec((1, D_MODEL))],
        out_specs=_row_spec(TS_FFN, D_MODEL),
        out_shape=jax.ShapeDtypeStruct(x.shape, F32),
        scratch_shapes=[pltpu.VMEM((SUBLANES, 2 * D_FF), F32),
                        pltpu.VMEM((TS_FFN + SUBLANES, max_w), F32)],
        compiler_params=_params(("arbitrary", "arbitrary")),
        name="ffn",
    )(h1, row(g_ffn_pre), w_ffn_up.astype(BF16), conv_ffn_w.astype(F32), row(b_ffn_conv),
      w_ffn_down.astype(BF16), row(g_ffn_post))

    h3 = pl.pallas_call(
        _ple_kernel,
        grid=(bsz, seq // TS_PLE),
        in_specs=[_row_spec(TS_PLE, D_MODEL), _row_spec(TS_PLE, PLE_DIM),
                  _const_spec((PLE_DIM, D_MODEL)), _const_spec((D_MODEL, D_MODEL)),
                  _const_spec((1, D_MODEL))],
        out_specs=_row_spec(TS_PLE, D_MODEL),
        out_shape=jax.ShapeDtypeStruct(x.shape, F32),
        compiler_params=_params(("arbitrary", "arbitrary")),
        name="ple",
    )(h2, p, w_ple_proj.astype(BF16), w_ple_gate.astype(BF16), row(g_ple_post))
    return h3


def kernel(x, p, positions, g_mix_pre, w_in, conv_a_w, w_a_out, g_q_lat, w_q_up, g_kv_lat, w_kv_up, w_b_out, w_o, g_mix_post, g_ffn_pre, w_ffn_up, conv_ffn_w, b_ffn_conv, w_ffn_down, g_ffn_post, w_ple_proj, w_ple_gate, g_ple_post):
    h = x
    for i in range(g_mix_pre.shape[0]):
        h = _layer(h, p[i], positions, g_mix_pre[i], w_in[i], conv_a_w[i], w_a_out[i], g_q_lat[i],
                   w_q_up[i], g_kv_lat[i], w_kv_up[i], w_b_out[i], w_o[i], g_mix_post[i],
                   g_ffn_pre[i], w_ffn_up[i], conv_ffn_w[i], b_ffn_conv[i], w_ffn_down[i],
                   g_ffn_post[i], w_ple_proj[i], w_ple_gate[i], g_ple_post[i])
    return h
```

```python
import functools
import math

import jax
import jax.numpy as jnp
from jax.experimental import pallas as pl
from jax.experimental.pallas import tpu as pltpu

D_MODEL = 1024
CHUNK = 64
CONV_DIM = 512
N_HEADS = 8
NOPE_DIM = 64
ROPE_DIM = 32
V_DIM = 64
Q_LORA = 384
KV_LORA = 256
ROPE_THETA = 10000.0
D_FF = 2816
PLE_DIM = 256
EPS = 1e-6
QK_DIM = NOPE_DIM + ROPE_DIM

LANES = 128
SUBLANES = 8
HEAD_SLOT = LANES
HP = N_HEADS * HEAD_SLOT
LAT_COLS = Q_LORA + KV_LORA + LANES
VMEM_LIMIT = 56 * 1024 * 1024

TS_MIX = 512
TQ = 1024
TKEY = TQ // 2
HEADS_PER_STEP = 2
V_ROWS = 80
TS_OUT = 512
TS_FFN = 512
TS_PLE = 512
FFN_CHUNKS = ((0, 1024), (1024, 1024), (2048, 768))

NEG_BIG = -1e30
BF16 = jnp.bfloat16
F32 = jnp.float32


def _rms(x, g):
    return x * jax.lax.rsqrt(jnp.mean(x * x, axis=-1, keepdims=True) + EPS) * g


def _dot(a, b):
    return jnp.dot(a, b, preferred_element_type=F32)


def _const_spec(shape):
    return pl.BlockSpec(shape, lambda *_: (0,) * len(shape))


def _shifted_rows(stage_ref, carry_ref, pre, rows, width):
    stage_ref[0:SUBLANES, 0:width] = carry_ref[...]
    stage_ref[SUBLANES:SUBLANES + rows, 0:width] = pre
    x1 = stage_ref[SUBLANES - 1:SUBLANES - 1 + rows, 0:width]
    x2 = stage_ref[SUBLANES - 2:SUBLANES - 2 + rows, 0:width]
    carry_ref[...] = pre[rows - SUBLANES:rows, :]
    return x1, x2


def _mix_in_kernel(x_ref, pos_ref, invf_ref, g_pre_ref, w_abc_ref, w_lat_ref,
                   w_ga_ref, w_gb_ref, conv_w_ref, w_a_out_ref, g_q_ref, w_q_ref,
                   g_kv_ref, w_k_ref, w_v_ref, vone_ref,
                   q_ref, k_ref, v_ref, gya_ref, sgb_ref,
                   carry_ref, stage_ref, *, q_scale):
    ts = x_ref.shape[0]

    @pl.when(pl.program_id(1) == 0)
    def _():
        carry_ref[...] = jnp.zeros_like(carry_ref)

    u = _rms(x_ref[...], g_pre_ref[...]).astype(BF16)

    abc = _dot(u, w_abc_ref[...])
    a_b = abc[:, 0:CONV_DIM]
    ca = abc[:, CONV_DIM:2 * CONV_DIM] * abc[:, 2 * CONV_DIM:3 * CONV_DIM]
    x1, x2 = _shifted_rows(stage_ref, carry_ref, ca, ts, CONV_DIM)
    cw = conv_w_ref[...]
    cv = x2 * cw[0:1, :] + x1 * cw[1:2, :] + ca * cw[2:3, :]
    ya = _dot((a_b * cv).astype(BF16), w_a_out_ref[...])
    gya_ref[...] = (jax.nn.sigmoid(_dot(u, w_ga_ref[...])) * ya).astype(BF16)
    sgb_ref[...] = jax.nn.sigmoid(_dot(u, w_gb_ref[...])).astype(BF16)

    lat = _dot(u, w_lat_ref[...])
    q_lat = lat[:, 0:Q_LORA]
    kv_lat = lat[:, Q_LORA:Q_LORA + KV_LORA]
    kr = lat[:, Q_LORA + KV_LORA:LAT_COLS]

    half = ROPE_DIM // 2
    ang_t = invf_ref[...] * pos_ref[...].astype(F32)
    sin_t = jnp.sin(ang_t)
    rows = lambda n: jnp.zeros((n, ts), F32)
    tail = HEAD_SLOT - QK_DIM
    cos = jnp.concatenate([jnp.ones((NOPE_DIM, ts), F32), jnp.cos(ang_t), rows(tail)], axis=0).T
    sin_dn = jnp.concatenate([rows(NOPE_DIM), -sin_t[0:half], rows(half + tail)], axis=0).T
    sin_up = jnp.concatenate([rows(NOPE_DIM + half), sin_t[half:], rows(tail)], axis=0).T

    def rope(t, c, s_dn, s_up):
        return (t * c + pltpu.roll(t, HEAD_SLOT - half, 1) * s_dn + pltpu.roll(t, half, 1) * s_up)

    qn = _rms(q_lat, g_q_ref[...]).astype(BF16)
    qq = _dot(qn, w_q_ref[...])
    kvn = _rms(kv_lat, g_kv_ref[...]).astype(BF16)
    kk = _dot(kvn, w_k_ref[...])
    vt = jax.lax.dot_general(w_v_ref[...], kvn, (((1,), (1,)), ((), ())),
                             preferred_element_type=F32) + vone_ref[...]
    v_ref[...] = vt.astype(BF16)

    k_pe = rope(kr, cos, sin_dn, sin_up)
    q_tables = (cos * q_scale, sin_dn * q_scale, sin_up * q_scale)
    for h in range(N_HEADS):
        sl = slice(h * HEAD_SLOT, (h + 1) * HEAD_SLOT)
        q_ref[:, sl] = rope(qq[:, sl], *q_tables).astype(BF16)
        k_ref[:, sl] = (kk[:, sl] + k_pe).astype(BF16)


def _attn_kernel(q_ref, k_ref, vt_ref, o_ref, s_ref, smax_ref, m_ref, acc_ref, qt_ref):
    tk, tq = s_ref.shape[-2:]
    assert tq == 2 * tk
    nq = q_ref.shape[0] // tq
    heads = range(HEADS_PER_STEP)
    lanes = lambda h: slice(h * HEAD_SLOT, (h + 1) * HEAD_SLOT)
    v_rows = lambda h: slice(h * HEAD_SLOT, h * HEAD_SLOT + V_ROWS)
    key_tile = lambda j: pl.ds(pl.multiple_of(j * tk, tk), tk)
    every = slice(0, tq)
    later = slice(tk, tq)

    def transpose_queries(qi):
        for h in heads:
            qt_ref[qi % 2, h] = q_ref[pl.ds(pl.multiple_of(qi * tq, tq), tq), lanes(h)].T

    def scores(qi, j, buf):
        for h in heads:
            sc = _dot(k_ref[key_tile(j), lanes(h)], qt_ref[qi % 2, h])
            s_ref[buf, h] = sc
            smax_ref[buf, h] = jnp.max(sc, axis=0, keepdims=True)

    def scores_later(qi, buf):
        for h in heads:
            s_ref[buf, h, :, later] = _dot(k_ref[key_tile(2 * qi + 1), lanes(h)],
                                           qt_ref[qi % 2, h, :, later])

    def accumulate(j, buf, cols, masked):
        width = cols.stop - cols.start
        if masked:
            ck = jax.lax.broadcasted_iota(jnp.int32, (tk, width), 0) // CHUNK
            cq = jax.lax.broadcasted_iota(jnp.int32, (tk, width), 1) // CHUNK
            load = lambda h: jnp.where(ck <= cq, s_ref[buf, h, :, cols], NEG_BIG)
            tile_max = lambda h: jnp.max(load(h), axis=0, keepdims=True)
        else:
            load = lambda h: s_ref[buf, h, :, cols]
            tile_max = lambda h: smax_ref[buf, h, :, cols]
        for h in heads:
            m = m_ref[h, :, cols]
            m_new = jnp.maximum(m, tile_max(h))
            p = jnp.exp2(load(h) - m_new)
            pv = _dot(vt_ref[v_rows(h), key_tile(j)], p.astype(BF16))
            acc_ref[h, 0:V_ROWS, cols] = acc_ref[h, 0:V_ROWS, cols] * jnp.exp2(m - m_new) + pv
            m_ref[h, :, cols] = m_new

    def query_tile(qi, _):
        next_qi = jnp.minimum(qi + 1, nq - 1)
        m_ref[...] = jnp.full_like(m_ref, NEG_BIG)
        acc_ref[...] = jnp.zeros_like(acc_ref)

        @pl.when(qi + 1 < nq)
        def _():
            transpose_queries(qi + 1)

        def tile_pair(i, _):
            scores(qi, 2 * i + 1, 1)
            accumulate(2 * i, 0, every, False)
            scores(qi, 2 * i + 2, 0)
            accumulate(2 * i + 1, 1, every, False)
            return 0

        jax.lax.fori_loop(0, qi, tile_pair, 0)

        scores_later(qi, 1)
        accumulate(2 * qi, 0, every, True)
        scores(next_qi, 0, 0)
        accumulate(2 * qi + 1, 1, later, True)

        for h in heads:
            acc = acc_ref[h]
            o_ref[pl.ds(pl.multiple_of(qi * tq, tq), tq), lanes(h)] = (
                acc / acc[V_DIM:V_DIM + 1, :]).T.astype(BF16)
        return 0

    transpose_queries(0)
    scores(0, 0, 0)
    jax.lax.fori_loop(0, nq, query_tile, 0)


def _mix_out_kernel(x_ref, o_ref, gya_ref, sgb_ref, w_b_ref, w_o_ref, g_ref, h_ref):
    yb = _dot(o_ref[...], w_b_ref[...])
    mixed = gya_ref[...].astype(F32) + sgb_ref[...].astype(F32) * yb
    mo = _dot(mixed.astype(BF16), w_o_ref[...])
    h_ref[...] = x_ref[...] + _rms(mo, g_ref[...])


def _ffn_kernel(h_ref, g_pre_ref, w_up_ref, conv_w_ref, b_ref, w_down_ref, g_post_ref,
                out_ref, carry_ref, stage_ref):
    ts = h_ref.shape[0]

    @pl.when(pl.program_id(1) == 0)
    def _():
        carry_ref[...] = jnp.zeros_like(carry_ref)

    h = h_ref[...]
    u = _rms(h, g_pre_ref[...]).astype(BF16)

    def conv_cols(col, width):
        cols = slice(col, col + width)
        pre = _dot(u, w_up_ref[:, cols])
        x1, x2 = _shifted_rows(stage_ref, carry_ref.at[:, cols], pre, ts, width)
        cw = conv_w_ref[:, cols]
        return x2 * cw[0:1, :] + x1 * cw[1:2, :] + pre * cw[2:3, :] + b_ref[:, cols]

    f = jnp.zeros((ts, D_MODEL), F32)
    for off, width in FFN_CHUNKS:
        act = jax.nn.gelu(conv_cols(off, width), approximate=True)
        val = conv_cols(D_FF + off, width)
        f = f + _dot((act * val).astype(BF16), w_down_ref[off:off + width, :])
    out_ref[...] = h + _rms(f, g_post_ref[...])


def _ple_kernel(h_ref, p_ref, w_proj_ref, w_gate_ref, g_ref, out_ref):
    h = h_ref[...]
    e = _dot(p_ref[...].astype(BF16), w_proj_ref[...])
    g = jax.nn.sigmoid(_dot(h.astype(BF16), w_gate_ref[...]))
    out_ref[...] = h + _rms(e * g, g_ref[...])


def _row_spec(ts, width):
    return pl.BlockSpec((None, ts, width), lambda b, s: (b, s, 0))


def _params(sem):
    return pltpu.CompilerParams(dimension_semantics=sem, vmem_limit_bytes=VMEM_LIMIT)


def _layer(x, p, positions, g_mix_pre, w_in, conv_a_w, w_a_out, g_q_lat, w_q_up, g_kv_lat,
           w_kv_up, w_b_out, w_o, g_mix_post, g_ffn_pre, w_ffn_up, conv_ffn_w, b_ffn_conv,
           w_ffn_down, g_ffn_post, w_ple_proj, w_ple_gate, g_ple_post):
    bsz, seq, _ = x.shape
    row = lambda v: v.reshape(1, -1).astype(F32)

    c0 = 3 * CONV_DIM
    c1 = c0 + Q_LORA
    c2 = c1 + KV_LORA
    c3 = c2 + ROPE_DIM
    c4 = c3 + D_MODEL
    half = ROPE_DIM // 2
    w_kr = jnp.pad(w_in[:, c2:c3], ((0, 0), (NOPE_DIM, HEAD_SLOT - QK_DIM)))
    w_abc = w_in[:, 0:c0].astype(BF16)
    w_lat = jnp.concatenate([w_in[:, c0:c2], w_kr], axis=1).astype(BF16)
    w_ga = w_in[:, c3:c4].astype(BF16)
    w_gb = w_in[:, c4:].astype(BF16)

    wq = w_q_up.reshape(Q_LORA, N_HEADS, QK_DIM)
    w_q = jnp.pad(wq, ((0, 0), (0, 0), (0, HEAD_SLOT - QK_DIM))).reshape(Q_LORA, HP).astype(BF16)

    wkv = w_kv_up.reshape(KV_LORA, N_HEADS, NOPE_DIM + V_DIM)
    w_k = jnp.pad(wkv[:, :, :NOPE_DIM], ((0, 0), (0, 0), (0, HEAD_SLOT - NOPE_DIM))).reshape(KV_LORA, HP).astype(BF16)
    w_vt = jnp.pad(wkv[:, :, NOPE_DIM:], ((0, 0), (0, 0), (0, HEAD_SLOT - V_DIM))).reshape(KV_LORA, HP).T.astype(BF16)
    vone = jnp.tile((jnp.arange(HEAD_SLOT) == V_DIM).astype(F32), N_HEADS).reshape(HP, 1)
    w_b = jnp.pad(w_b_out.reshape(N_HEADS, V_DIM, D_MODEL),
                  ((0, 0), (0, HEAD_SLOT - V_DIM), (0, 0))).reshape(HP, D_MODEL).astype(BF16)

    inv = 1.0 / (ROPE_THETA ** (jnp.arange(half, dtype=F32) * (2.0 / ROPE_DIM)))
    invf = jnp.concatenate([inv, inv]).reshape(ROPE_DIM, 1)
    pos3 = positions.reshape(bsz, 1, seq)
    q_scale = (QK_DIM ** -0.5) * math.log2(math.e)

    act = jax.ShapeDtypeStruct((bsz, seq, HP), BF16)
    q, k, vt, gya, sgb = pl.pallas_call(
        functools.partial(_mix_in_kernel, q_scale=q_scale),
        grid=(bsz, seq // TS_MIX),
        in_specs=[_row_spec(TS_MIX, D_MODEL),
                  pl.BlockSpec((None, 1, TS_MIX), lambda b, s: (b, 0, s)),
                  _const_spec((ROPE_DIM, 1)), _const_spec((1, D_MODEL)),
                  _const_spec((D_MODEL, c0)), _const_spec((D_MODEL, LAT_COLS)),
                  _const_spec((D_MODEL, D_MODEL)), _const_spec((D_MODEL, D_MODEL)),
                  _const_spec((3, CONV_DIM)), _const_spec((CONV_DIM, D_MODEL)),
                  _const_spec((1, Q_LORA)), _const_spec((Q_LORA, HP)),
                  _const_spec((1, KV_LORA)), _const_spec((KV_LORA, HP)),
                  _const_spec((HP, KV_LORA)), _const_spec((HP, 1))],
        out_specs=[_row_spec(TS_MIX, HP), _row_spec(TS_MIX, HP),
                   pl.BlockSpec((None, HP, TS_MIX), lambda b, s: (b, 0, s)),
                   _row_spec(TS_MIX, HP), _row_spec(TS_MIX, HP)],
        out_shape=[act, act, jax.ShapeDtypeStruct((bsz, HP, seq), BF16), act, act],
        scratch_shapes=[pltpu.VMEM((SUBLANES, CONV_DIM), F32),
                        pltpu.VMEM((TS_MIX + SUBLANES, CONV_DIM), F32)],
        compiler_params=_params(("arbitrary", "arbitrary")),
        name="mix_in",
    )(x, pos3, invf, row(g_mix_pre), w_abc, w_lat, w_ga, w_gb, conv_a_w.astype(F32),
      w_a_out.astype(BF16), row(g_q_lat), w_q, row(g_kv_lat), w_k, w_vt, vone)

    group = HEADS_PER_STEP * HEAD_SLOT
    head_cols = pl.BlockSpec((None, seq, group), lambda b, h: (b, 0, h))
    o = pl.pallas_call(
        _attn_kernel,
        grid=(bsz, N_HEADS // HEADS_PER_STEP),
        in_specs=[head_cols, head_cols,
                  pl.BlockSpec((None, group, seq), lambda b, h: (b, h, 0))],
        out_specs=head_cols,
        out_shape=act,
        scratch_shapes=[pltpu.VMEM((2, HEADS_PER_STEP, TKEY, TQ), F32),
                        pltpu.VMEM((2, HEADS_PER_STEP, 1, TQ), F32),
                        pltpu.VMEM((HEADS_PER_STEP, 1, TQ), F32),
                        pltpu.VMEM((HEADS_PER_STEP, HEAD_SLOT, TQ), F32),
                        pltpu.VMEM((2, HEADS_PER_STEP, HEAD_SLOT, TQ), BF16)],
        compiler_params=_params(("arbitrary", "arbitrary")),
        name="attn",
    )(q, k, vt)

    h1 = pl.pallas_call(
        _mix_out_kernel,
        grid=(bsz, seq // TS_OUT),
        in_specs=[_row_spec(TS_OUT, D_MODEL), _row_spec(TS_OUT, HP), _row_spec(TS_OUT, HP),
                  _row_spec(TS_OUT, HP), _const_spec((HP, D_MODEL)),
                  _const_spec((D_MODEL, D_MODEL)), _const_spec((1, D_MODEL))],
        out_specs=_row_spec(TS_OUT, D_MODEL),
        out_shape=jax.ShapeDtypeStruct(x.shape, F32),
        compiler_params=_params(("arbitrary", "arbitrary")),
        name="mix_out",
    )(x, o, gya, sgb, w_b, w_o.astype(BF16), row(g_mix_post))

    max_w = max(w for _, w in FFN_CHUNKS)
    h2 = pl.pallas_call(
        _ffn_kernel,
        grid=(bsz, seq // TS_FFN),
        in_specs=[_row_spec(TS_FFN, D_MODEL), _const_spec((1, D_MODEL)),
                  _const_spec((D_MODEL, 2 * D_FF)), _const_spec((3, 2 * D_FF)),
                  _const_spec((1, 2 * D_FF)), _const_spec((D_FF, D_MODEL)),
                  _const_spec((1, D_MODEL))],
        out_specs=_row_spec(TS_FFN, D_MODEL),
        out_shape=jax.ShapeDtypeStruct(x.shape, F32),
        scratch_shapes=[pltpu.VMEM((SUBLANES, 2 * D_FF), F32),
                        pltpu.VMEM((TS_FFN + SUBLANES, max_w), F32)],
        compiler_params=_params(("arbitrary", "arbitrary")),
        name="ffn",
    )(h1, row(g_ffn_pre), w_ffn_up.astype(BF16), conv_ffn_w.astype(F32), row(b_ffn_conv),
      w_ffn_down.astype(BF16), row(g_ffn_post))

    h3 = pl.pallas_call(
        _ple_kernel,
        grid=(bsz, seq // TS_PLE),
        in_specs=[_row_spec(TS_PLE, D_MODEL), _row_spec(TS_PLE, PLE_DIM),
                  _const_spec((PLE_DIM, D_MODEL)), _const_spec((D_MODEL, D_MODEL)),
                  _const_spec((1, D_MODEL))],
        out_specs=_row_spec(TS_PLE, D_MODEL),
        out_shape=jax.ShapeDtypeStruct(x.shape, F32),
        compiler_params=_params(("arbitrary", "arbitrary")),
        name="ple",
    )(h2, p, w_ple_proj.astype(BF16), w_ple_gate.astype(BF16), row(g_ple_post))
    return h3


def kernel(x, p, positions, g_mix_pre, w_in, conv_a_w, w_a_out, g_q_lat, w_q_up, g_kv_lat, w_kv_up, w_b_out, w_o, g_mix_post, g_ffn_pre, w_ffn_up, conv_ffn_w, b_ffn_conv, w_ffn_down, g_ffn_post, w_ple_proj, w_ple_gate, g_ple_post):
    h = x
    for i in range(g_mix_pre.shape[0]):
        h = _layer(h, p[i], positions, g_mix_pre[i], w_in[i], conv_a_w[i], w_a_out[i], g_q_lat[i],
                   w_q_up[i], g_kv_lat[i], w_kv_up[i], w_b_out[i], w_o[i], g_mix_post[i],
                   g_ffn_pre[i], w_ffn_up[i], conv_ffn_w[i], b_ffn_conv[i], w_ffn_down[i],
                   g_ffn_post[i], w_ple_proj[i], w_ple_gate[i], g_ple_post[i])
    return h
```

```python
import functools
import math

import jax
import jax.numpy as jnp
from jax.experimental import pallas as pl
from jax.experimental.pallas import tpu as pltpu

D_MODEL = 1024
CHUNK = 64
CONV_DIM = 512
N_HEADS = 8
NOPE_DIM = 64
ROPE_DIM = 32
V_DIM = 64
Q_LORA = 384
KV_LORA = 256
ROPE_THETA = 10000.0
D_FF = 2816
PLE_DIM = 256
EPS = 1e-6
QK_DIM = NOPE_DIM + ROPE_DIM

LANES = 128
SUBLANES = 8
HEAD_SLOT = LANES
HP = N_HEADS * HEAD_SLOT
LAT_COLS = Q_LORA + KV_LORA + LANES
VMEM_LIMIT = 56 * 1024 * 1024

TS_MIX = 512
TQ = 1024
TKEY = TQ // 2
HEADS_PER_STEP = 2
V_ROWS = 80
TS_POST = 512
FFN_CHUNKS = ((0, 1024), (1024, 1024), (2048, 768))

GELU_C = math.sqrt(2.0 / math.pi)
NEG_BIG = -1e30
BF16 = jnp.bfloat16
F32 = jnp.float32


def _rms(x, g):
    return x * jax.lax.rsqrt(jnp.mean(x * x, axis=-1, keepdims=True) + EPS) * g


def _dot(a, b):
    return jnp.dot(a, b, preferred_element_type=F32)


def _const_spec(shape):
    return pl.BlockSpec(shape, lambda *_: (0,) * len(shape))


def _shifted_rows(carry_ref, pre):
    rows = pre.shape[0]
    ext = jnp.concatenate([carry_ref[...], pre], axis=0)
    carry_ref[...] = pre[rows - SUBLANES:rows, :]
    return pltpu.roll(ext, 1, 0)[SUBLANES:], pltpu.roll(ext, 2, 0)[SUBLANES:]


def _mix_in_kernel(x_ref, pos_ref, invf_ref, g_pre_ref, w_abc_ref, w_lat_ref,
                   w_ga_ref, w_gb_ref, conv_w_ref, w_a_out_ref, g_q_ref, w_q_ref,
                   g_kv_ref, w_k_ref, w_v_ref, vone_ref,
                   q_ref, k_ref, v_ref, gya_ref, sgb_ref,
                   carry_ref, *, q_scale):
    ts = x_ref.shape[0]

    @pl.when(pl.program_id(1) == 0)
    def _():
        carry_ref[...] = jnp.zeros_like(carry_ref)

    u = _rms(x_ref[...], g_pre_ref[...]).astype(BF16)

    abc = _dot(u, w_abc_ref[...])
    a_b = abc[:, 0:CONV_DIM]
    ca = abc[:, CONV_DIM:2 * CONV_DIM] * abc[:, 2 * CONV_DIM:3 * CONV_DIM]
    x1, x2 = _shifted_rows(carry_ref, ca)
    cw = conv_w_ref[...]
    cv = x2 * cw[0:1, :] + x1 * cw[1:2, :] + ca * cw[2:3, :]
    ya = _dot((a_b * cv).astype(BF16), w_a_out_ref[...])
    gya_ref[...] = (jax.nn.sigmoid(_dot(u, w_ga_ref[...])) * ya).astype(BF16)
    sgb_ref[...] = jax.nn.sigmoid(_dot(u, w_gb_ref[...])).astype(BF16)

    lat = _dot(u, w_lat_ref[...])
    q_lat = lat[:, 0:Q_LORA]
    kv_lat = lat[:, Q_LORA:Q_LORA + KV_LORA]
    kr = lat[:, Q_LORA + KV_LORA:LAT_COLS]

    half = ROPE_DIM // 2
    ang_t = invf_ref[...] * pos_ref[...].astype(F32)
    sin_t = jnp.sin(ang_t)
    rows = lambda n: jnp.zeros((n, ts), F32)
    tail = HEAD_SLOT - QK_DIM
    cos = jnp.concatenate([jnp.ones((NOPE_DIM, ts), F32), jnp.cos(ang_t), rows(tail)], axis=0).T
    sin_dn = jnp.concatenate([rows(NOPE_DIM), -sin_t[0:half], rows(half + tail)], axis=0).T
    sin_up = jnp.concatenate([rows(NOPE_DIM + half), sin_t[half:], rows(tail)], axis=0).T

    def rope(t, c, s_dn, s_up):
        return (t * c + pltpu.roll(t, HEAD_SLOT - half, 1) * s_dn + pltpu.roll(t, half, 1) * s_up)

    qn = _rms(q_lat, g_q_ref[...]).astype(BF16)
    qq = _dot(qn, w_q_ref[...])
    kvn = _rms(kv_lat, g_kv_ref[...]).astype(BF16)
    kk = _dot(kvn, w_k_ref[...])
    vt = jax.lax.dot_general(w_v_ref[...], kvn, (((1,), (1,)), ((), ())),
                             preferred_element_type=F32) + vone_ref[...]
    v_ref[...] = vt.astype(BF16)

    k_pe = rope(kr, cos, sin_dn, sin_up)
    q_tables = (cos * q_scale, sin_dn * q_scale, sin_up * q_scale)
    for h in range(N_HEADS):
        sl = slice(h * HEAD_SLOT, (h + 1) * HEAD_SLOT)
        q_ref[:, sl] = rope(qq[:, sl], *q_tables).astype(BF16)
        k_ref[:, sl] = (kk[:, sl] + k_pe).astype(BF16)


def _attn_kernel(q_ref, k_ref, vt_ref, o_ref, s_ref, smax_ref, m_ref, acc_ref, qt_ref):
    tk, tq = s_ref.shape[-2:]
    assert tq == 2 * tk
    nq = q_ref.shape[0] // tq
    heads = range(HEADS_PER_STEP)
    lanes = lambda h: slice(h * HEAD_SLOT, (h + 1) * HEAD_SLOT)
    v_rows = lambda h: slice(h * HEAD_SLOT, h * HEAD_SLOT + V_ROWS)
    key_tile = lambda j: pl.ds(pl.multiple_of(j * tk, tk), tk)
    every = slice(0, tq)
    later = slice(tk, tq)

    def transpose_queries(qi):
        for h in heads:
            qt_ref[qi % 2, h] = q_ref[pl.ds(pl.multiple_of(qi * tq, tq), tq), lanes(h)].T

    def scores(qi, j, buf):
        for h in heads:
            sc = _dot(k_ref[key_tile(j), lanes(h)], qt_ref[qi % 2, h])
            s_ref[buf, h] = sc
            smax_ref[buf, h] = jnp.max(sc, axis=0, keepdims=True)

    def scores_later(qi, buf):
        for h in heads:
            s_ref[buf, h, :, later] = _dot(k_ref[key_tile(2 * qi + 1), lanes(h)],
                                           qt_ref[qi % 2, h, :, later])

    def accumulate(j, buf, cols, masked):
        width = cols.stop - cols.start
        if masked:
            ck = jax.lax.broadcasted_iota(jnp.int32, (tk, width), 0) // CHUNK
            cq = jax.lax.broadcasted_iota(jnp.int32, (tk, width), 1) // CHUNK
            load = lambda h: jnp.where(ck <= cq, s_ref[buf, h, :, cols], NEG_BIG)
            tile_max = lambda h: jnp.max(load(h), axis=0, keepdims=True)
        else:
            load = lambda h: s_ref[buf, h, :, cols]
            tile_max = lambda h: smax_ref[buf, h, :, cols]
        for h in heads:
            m = m_ref[h, :, cols]
            m_new = jnp.maximum(m, tile_max(h))
            p = jnp.exp2(load(h) - m_new)
            pv = _dot(vt_ref[v_rows(h), key_tile(j)], p.astype(BF16))
            acc_ref[h, 0:V_ROWS, cols] = acc_ref[h, 0:V_ROWS, cols] * jnp.exp2(m - m_new) + pv
            m_ref[h, :, cols] = m_new

    def query_tile(qi, _):
        next_qi = jnp.minimum(qi + 1, nq - 1)
        m_ref[...] = jnp.full_like(m_ref, NEG_BIG)
        acc_ref[...] = jnp.zeros_like(acc_ref)

        @pl.when(qi + 1 < nq)
        def _():
            transpose_queries(qi + 1)

        def tile_pair(i, _):
            scores(qi, 2 * i + 1, 1)
            accumulate(2 * i, 0, every, False)
            scores(qi, 2 * i + 2, 0)
            accumulate(2 * i + 1, 1, every, False)
            return 0

        jax.lax.fori_loop(0, qi, tile_pair, 0)

        scores_later(qi, 1)
        accumulate(2 * qi, 0, every, True)
        scores(next_qi, 0, 0)
        accumulate(2 * qi + 1, 1, later, True)

        for h in heads:
            acc = acc_ref[h]
            o_ref[pl.ds(pl.multiple_of(qi * tq, tq), tq), lanes(h)] = (
                acc / acc[V_DIM:V_DIM + 1, :]).T.astype(BF16)
        return 0

    transpose_queries(0)
    scores(0, 0, 0)
    jax.lax.fori_loop(0, nq, query_tile, 0)


def _post_kernel(x_ref, o_ref, gya_ref, sgb_ref, p_ref, w_b_ref, w_o_ref, g_mix_ref,
                 g_pre_ref, w_up_ref, conv_w_ref, b_ref, w_down_ref, g_post_ref,
                 w_proj_ref, w_gate_ref, g_ple_ref, out_ref, carry_ref):
    ts = x_ref.shape[0]

    @pl.when(pl.program_id(1) == 0)
    def _():
        carry_ref[...] = jnp.zeros_like(carry_ref)

    yb = _dot(o_ref[...], w_b_ref[...])
    mixed = gya_ref[...].astype(F32) + sgb_ref[...].astype(F32) * yb
    h = x_ref[...] + _rms(_dot(mixed.astype(BF16), w_o_ref[...]), g_mix_ref[...])

    u = _rms(h, g_pre_ref[...]).astype(BF16)

    def conv_cols(col, width):
        cols = slice(col, col + width)
        pre = _dot(u, w_up_ref[:, cols])
        x1, x2 = _shifted_rows(carry_ref.at[:, cols], pre)
        cw = conv_w_ref[:, cols]
        return x2 * cw[0:1, :] + x1 * cw[1:2, :] + pre * cw[2:3, :] + b_ref[:, cols]

    f = jnp.zeros((ts, D_MODEL), F32)
    for off, width in FFN_CHUNKS:
        gate = conv_cols(off, width)
        half_gate = 0.5 * gate
        act = half_gate + half_gate * jnp.tanh(gate * (GELU_C + (GELU_C * 0.044715) * (gate * gate)))
        val = conv_cols(D_FF + off, width)
        f = f + _dot((act * val).astype(BF16), w_down_ref[off:off + width, :])
    h = h + _rms(f, g_post_ref[...])

    e = _dot(p_ref[...].astype(BF16), w_proj_ref[...])
    g = jax.nn.sigmoid(_dot(h.astype(BF16), w_gate_ref[...]))
    out_ref[...] = h + _rms(e * g, g_ple_ref[...])


def _row_spec(ts, width):
    return pl.BlockSpec((None, ts, width), lambda b, s: (b, s, 0))


def _params(sem):
    return pltpu.CompilerParams(dimension_semantics=sem, vmem_limit_bytes=VMEM_LIMIT)


def _layer(x, p, positions, g_mix_pre, w_in, conv_a_w, w_a_out, g_q_lat, w_q_up, g_kv_lat,
           w_kv_up, w_b_out, w_o, g_mix_post, g_ffn_pre, w_ffn_up, conv_ffn_w, b_ffn_conv,
           w_ffn_down, g_ffn_post, w_ple_proj, w_ple_gate, g_ple_post):
    bsz, seq, _ = x.shape
    row = lambda v: v.reshape(1, -1).astype(F32)

    c0 = 3 * CONV_DIM
    c1 = c0 + Q_LORA
    c2 = c1 + KV_LORA
    c3 = c2 + ROPE_DIM
    c4 = c3 + D_MODEL
    half = ROPE_DIM // 2
    w_kr = jnp.pad(w_in[:, c2:c3], ((0, 0), (NOPE_DIM, HEAD_SLOT - QK_DIM)))
    w_abc = w_in[:, 0:c0].astype(BF16)
    w_lat = jnp.concatenate([w_in[:, c0:c2], w_kr], axis=1).astype(BF16)
    w_ga = w_in[:, c3:c4].astype(BF16)
    w_gb = w_in[:, c4:].astype(BF16)

    wq = w_q_up.reshape(Q_LORA, N_HEADS, QK_DIM)
    w_q = jnp.pad(wq, ((0, 0), (0, 0), (0, HEAD_SLOT - QK_DIM))).reshape(Q_LORA, HP).astype(BF16)

    wkv = w_kv_up.reshape(KV_LORA, N_HEADS, NOPE_DIM + V_DIM)
    w_k = jnp.pad(wkv[:, :, :NOPE_DIM], ((0, 0), (0, 0), (0, HEAD_SLOT - NOPE_DIM))).reshape(KV_LORA, HP).astype(BF16)
    w_vt = jnp.pad(wkv[:, :, NOPE_DIM:], ((0, 0), (0, 0), (0, HEAD_SLOT - V_DIM))).reshape(KV_LORA, HP).T.astype(BF16)
    vone = jnp.tile((jnp.arange(HEAD_SLOT) == V_DIM).astype(F32), N_HEADS).reshape(HP, 1)
    w_b = jnp.pad(w_b_out.reshape(N_HEADS, V_DIM, D_MODEL),
                  ((0, 0), (0, HEAD_SLOT - V_DIM), (0, 0))).reshape(HP, D_MODEL).astype(BF16)

    inv = 1.0 / (ROPE_THETA ** (jnp.arange(half, dtype=F32) * (2.0 / ROPE_DIM)))
    invf = jnp.concatenate([inv, inv]).reshape(ROPE_DIM, 1)
    pos3 = positions.reshape(bsz, 1, seq)
    q_scale = (QK_DIM ** -0.5) * math.log2(math.e)

    act = jax.ShapeDtypeStruct((bsz, seq, HP), BF16)
    q, k, vt, gya, sgb = pl.pallas_call(
        functools.partial(_mix_in_kernel, q_scale=q_scale),
        grid=(bsz, seq // TS_MIX),
        in_specs=[_row_spec(TS_MIX, D_MODEL),
                  pl.BlockSpec((None, 1, TS_MIX), lambda b, s: (b, 0, s)),
                  _const_spec((ROPE_DIM, 1)), _const_spec((1, D_MODEL)),
                  _const_spec((D_MODEL, c0)), _const_spec((D_MODEL, LAT_COLS)),
                  _const_spec((D_MODEL, D_MODEL)), _const_spec((D_MODEL, D_MODEL)),
                  _const_spec((3, CONV_DIM)), _const_spec((CONV_DIM, D_MODEL)),
                  _const_spec((1, Q_LORA)), _const_spec((Q_LORA, HP)),
                  _const_spec((1, KV_LORA)), _const_spec((KV_LORA, HP)),
                  _const_spec((HP, KV_LORA)), _const_spec((HP, 1))],
        out_specs=[_row_spec(TS_MIX, HP), _row_spec(TS_MIX, HP),
                   pl.BlockSpec((None, HP, TS_MIX), lambda b, s: (b, 0, s)),
                   _row_spec(TS_MIX, HP), _row_spec(TS_MIX, HP)],
        out_shape=[act, act, jax.ShapeDtypeStruct((bsz, HP, seq), BF16), act, act],
        scratch_shapes=[pltpu.VMEM((SUBLANES, CONV_DIM), F32)],
        compiler_params=_params(("arbitrary", "arbitrary")),
        name="mix_in",
    )(x, pos3, invf, row(g_mix_pre), w_abc, w_lat, w_ga, w_gb, conv_a_w.astype(F32),
      w_a_out.astype(BF16), row(g_q_lat), w_q, row(g_kv_lat), w_k, w_vt, vone)

    group = HEADS_PER_STEP * HEAD_SLOT
    head_cols = pl.BlockSpec((None, seq, group), lambda b, h: (b, 0, h))
    o = pl.pallas_call(
        _attn_kernel,
        grid=(bsz, N_HEADS // HEADS_PER_STEP),
        in_specs=[head_cols, head_cols,
                  pl.BlockSpec((None, group, seq), lambda b, h: (b, h, 0))],
        out_specs=head_cols,
        out_shape=act,
        scratch_shapes=[pltpu.VMEM((2, HEADS_PER_STEP, TKEY, TQ), F32),
                        pltpu.VMEM((2, HEADS_PER_STEP, 1, TQ), F32),
                        pltpu.VMEM((HEADS_PER_STEP, 1, TQ), F32),
                        pltpu.VMEM((HEADS_PER_STEP, HEAD_SLOT, TQ), F32),
                        pltpu.VMEM((2, HEADS_PER_STEP, HEAD_SLOT, TQ), BF16)],
        compiler_params=_params(("arbitrary", "arbitrary")),
        name="attn",
    )(q, k, vt)

    h3 = pl.pallas_call(
        _post_kernel,
        grid=(bsz, seq // TS_POST),
        in_specs=[_row_spec(TS_POST, D_MODEL), _row_spec(TS_POST, HP), _row_spec(TS_POST, HP),
                  _row_spec(TS_POST, HP), _row_spec(TS_POST, PLE_DIM),
                  _const_spec((HP, D_MODEL)), _const_spec((D_MODEL, D_MODEL)),
                  _const_spec((1, D_MODEL)), _const_spec((1, D_MODEL)),
                  _const_spec((D_MODEL, 2 * D_FF)), _const_spec((3, 2 * D_FF)),
                  _const_spec((1, 2 * D_FF)), _const_spec((D_FF, D_MODEL)),
                  _const_spec((1, D_MODEL)), _const_spec((PLE_DIM, D_MODEL)),
                  _const_spec((D_MODEL, D_MODEL)), _const_spec((1, D_MODEL))],
        out_specs=_row_spec(TS_POST, D_MODEL),
        out_shape=jax.ShapeDtypeStruct(x.shape, F32),
        scratch_shapes=[pltpu.VMEM((SUBLANES, 2 * D_FF), F32)],
        compiler_params=_params(("arbitrary", "arbitrary")),
        name="post",
    )(x, o, gya, sgb, p, w_b, w_o.astype(BF16), row(g_mix_post), row(g_ffn_pre),
      w_ffn_up.astype(BF16), conv_ffn_w.astype(F32), row(b_ffn_conv), w_ffn_down.astype(BF16),
      row(g_ffn_post), w_ple_proj.astype(BF16), w_ple_gate.astype(BF16), row(g_ple_post))
    return h3


def kernel(x, p, positions, g_mix_pre, w_in, conv_a_w, w_a_out, g_q_lat, w_q_up, g_kv_lat, w_kv_up, w_b_out, w_o, g_mix_post, g_ffn_pre, w_ffn_up, conv_ffn_w, b_ffn_conv, w_ffn_down, g_ffn_post, w_ple_proj, w_ple_gate, g_ple_post):
    h = x
    for i in range(g_mix_pre.shape[0]):
        h = _layer(h, p[i], positions, g_mix_pre[i], w_in[i], conv_a_w[i], w_a_out[i], g_q_lat[i],
                   w_q_up[i], g_kv_lat[i], w_kv_up[i], w_b_out[i], w_o[i], g_mix_post[i],
                   g_ffn_pre[i], w_ffn_up[i], conv_ffn_w[i], b_ffn_conv[i], w_ffn_down[i],
                   g_ffn_post[i], w_ple_proj[i], w_ple_gate[i], g_ple_post[i])
    return h
```

```python
import functools
import math

import jax
import jax.numpy as jnp
from jax.experimental import pallas as pl
from jax.experimental.pallas import tpu as pltpu

D_MODEL = 1024
CHUNK = 64
CONV_DIM = 512
N_HEADS = 8
NOPE_DIM = 64
ROPE_DIM = 32
V_DIM = 64
Q_LORA = 384
KV_LORA = 256
ROPE_THETA = 10000.0
D_FF = 2816
PLE_DIM = 256
EPS = 1e-6
QK_DIM = NOPE_DIM + ROPE_DIM

LANES = 128
SUBLANES = 8
HEAD_SLOT = LANES
HP = N_HEADS * HEAD_SLOT
LAT_COLS = Q_LORA + KV_LORA + LANES
VMEM_LIMIT = 56 * 1024 * 1024

TS_MIX = 1024
TQ = 1024
TKEY = TQ // 2
HEADS_PER_STEP = 2
V_ROWS = 80
TS_POST = 512
FFN_CHUNKS = ((0, 1024), (1024, 1024), (2048, 768))

GELU_C = math.sqrt(2.0 / math.pi)
NEG_BIG = -1e30
BF16 = jnp.bfloat16
F32 = jnp.float32


def _rms(x, g):
    return x * jax.lax.rsqrt(jnp.mean(x * x, axis=-1, keepdims=True) + EPS) * g


def _dot(a, b):
    return jnp.dot(a, b, preferred_element_type=F32)


def _const_spec(shape):
    return pl.BlockSpec(shape, lambda *_: (0,) * len(shape))


def _shifted_rows(carry_ref, pre):
    rows = pre.shape[0]
    ext = jnp.concatenate([carry_ref[...], pre], axis=0)
    carry_ref[...] = pre[rows - SUBLANES:rows, :]
    return pltpu.roll(ext, 1, 0)[SUBLANES:], pltpu.roll(ext, 2, 0)[SUBLANES:]


def _mix_in_kernel(x_ref, pos_ref, invf_ref, g_pre_ref, w_abc_ref, w_lat_ref,
                   w_ga_ref, w_gb_ref, conv_w_ref, w_a_out_ref, g_q_ref, w_q_ref,
                   g_kv_ref, w_k_ref, w_v_ref, vone_ref,
                   q_ref, k_ref, v_ref, gya_ref, sgb_ref,
                   carry_ref, *, q_scale):
    ts = x_ref.shape[0]

    @pl.when(pl.program_id(1) == 0)
    def _():
        carry_ref[...] = jnp.zeros_like(carry_ref)

    u = _rms(x_ref[...], g_pre_ref[...]).astype(BF16)

    abc = _dot(u, w_abc_ref[...])
    a_b = abc[:, 0:CONV_DIM]
    ca = abc[:, CONV_DIM:2 * CONV_DIM] * abc[:, 2 * CONV_DIM:3 * CONV_DIM]
    x1, x2 = _shifted_rows(carry_ref, ca)
    cw = conv_w_ref[...]
    cv = x2 * cw[0:1, :] + x1 * cw[1:2, :] + ca * cw[2:3, :]
    ya = _dot((a_b * cv).astype(BF16), w_a_out_ref[...])
    gya_ref[...] = (jax.nn.sigmoid(_dot(u, w_ga_ref[...])) * ya).astype(BF16)
    sgb_ref[...] = jax.nn.sigmoid(_dot(u, w_gb_ref[...])).astype(BF16)

    lat = _dot(u, w_lat_ref[...])
    q_lat = lat[:, 0:Q_LORA]
    kv_lat = lat[:, Q_LORA:Q_LORA + KV_LORA]
    kr = lat[:, Q_LORA + KV_LORA:LAT_COLS]

    half = ROPE_DIM // 2
    ang_t = invf_ref[...] * pos_ref[...].astype(F32)
    sin_t = jnp.sin(ang_t)
    rows = lambda n: jnp.zeros((n, ts), F32)
    tail = HEAD_SLOT - QK_DIM
    cos = jnp.concatenate([jnp.ones((NOPE_DIM, ts), F32), jnp.cos(ang_t), rows(tail)], axis=0).T
    sin_dn = jnp.concatenate([rows(NOPE_DIM), -sin_t[0:half], rows(half + tail)], axis=0).T
    sin_up = jnp.concatenate([rows(NOPE_DIM + half), sin_t[half:], rows(tail)], axis=0).T

    def rope(t, c, s_dn, s_up):
        return (t * c + pltpu.roll(t, HEAD_SLOT - half, 1) * s_dn + pltpu.roll(t, half, 1) * s_up)

    qn = _rms(q_lat, g_q_ref[...]).astype(BF16)
    qq = _dot(qn, w_q_ref[...])
    kvn = _rms(kv_lat, g_kv_ref[...]).astype(BF16)
    kk = _dot(kvn, w_k_ref[...])
    vt = jax.lax.dot_general(w_v_ref[...], kvn, (((1,), (1,)), ((), ())),
                             preferred_element_type=F32) + vone_ref[...]
    v_ref[...] = vt.astype(BF16)

    k_pe = rope(kr, cos, sin_dn, sin_up)
    q_tables = (cos * q_scale, sin_dn * q_scale, sin_up * q_scale)
    for h in range(N_HEADS):
        sl = slice(h * HEAD_SLOT, (h + 1) * HEAD_SLOT)
        q_ref[:, sl] = rope(qq[:, sl], *q_tables).astype(BF16)
        k_ref[:, sl] = (kk[:, sl] + k_pe).astype(BF16)


def _attn_kernel(q_ref, k_ref, vt_ref, o_ref, s_ref, smax_ref, m_ref, acc_ref, qt_ref):
    tk, tq = s_ref.shape[-2:]
    assert tq == 2 * tk
    nq = q_ref.shape[0] // tq
    heads = range(HEADS_PER_STEP)
    lanes = lambda h: slice(h * HEAD_SLOT, (h + 1) * HEAD_SLOT)
    v_rows = lambda h: slice(h * HEAD_SLOT, h * HEAD_SLOT + V_ROWS)
    key_tile = lambda j: pl.ds(pl.multiple_of(j * tk, tk), tk)
    every = slice(0, tq)
    later = slice(tk, tq)

    def transpose_queries(qi):
        for h in heads:
            qt_ref[qi % 2, h] = q_ref[pl.ds(pl.multiple_of(qi * tq, tq), tq), lanes(h)].T

    def scores(qi, j, buf):
        for h in heads:
            sc = _dot(k_ref[key_tile(j), lanes(h)], qt_ref[qi % 2, h])
            s_ref[buf, h] = sc
            smax_ref[buf, h] = jnp.max(sc, axis=0, keepdims=True)

    def scores_later(qi, buf):
        for h in heads:
            s_ref[buf, h, :, later] = _dot(k_ref[key_tile(2 * qi + 1), lanes(h)],
                                           qt_ref[qi % 2, h, :, later])

    def accumulate(j, buf, cols, masked):
        width = cols.stop - cols.start
        if masked:
            ck = jax.lax.broadcasted_iota(jnp.int32, (tk, width), 0) // CHUNK
            cq = jax.lax.broadcasted_iota(jnp.int32, (tk, width), 1) // CHUNK
            load = lambda h: jnp.where(ck <= cq, s_ref[buf, h, :, cols], NEG_BIG)
            tile_max = lambda h: jnp.max(load(h), axis=0, keepdims=True)
        else:
            load = lambda h: s_ref[buf, h, :, cols]
            tile_max = lambda h: smax_ref[buf, h, :, cols]
        for h in heads:
            m = m_ref[h, :, cols]
            m_new = jnp.maximum(m, tile_max(h))
            p = jnp.exp2(load(h) - m_new)
            pv = _dot(vt_ref[v_rows(h), key_tile(j)], p.astype(BF16))
            acc_ref[h, 0:V_ROWS, cols] = acc_ref[h, 0:V_ROWS, cols] * jnp.exp2(m - m_new) + pv
            m_ref[h, :, cols] = m_new

    def query_tile(qi, _):
        next_qi = jnp.minimum(qi + 1, nq - 1)
        m_ref[...] = jnp.full_like(m_ref, NEG_BIG)
        acc_ref[...] = jnp.zeros_like(acc_ref)

        @pl.when(qi + 1 < nq)
        def _():
            transpose_queries(qi + 1)

        def tile_pair(i, _):
            scores(qi, 2 * i + 1, 1)
            accumulate(2 * i, 0, every, False)
            scores(qi, 2 * i + 2, 0)
            accumulate(2 * i + 1, 1, every, False)
            return 0

        jax.lax.fori_loop(0, qi, tile_pair, 0)

        scores_later(qi, 1)
        accumulate(2 * qi, 0, every, True)
        scores(next_qi, 0, 0)
        accumulate(2 * qi + 1, 1, later, True)

        for h in heads:
            acc = acc_ref[h]
            o_ref[pl.ds(pl.multiple_of(qi * tq, tq), tq), lanes(h)] = (
                acc / acc[V_DIM:V_DIM + 1, :]).T.astype(BF16)
        return 0

    transpose_queries(0)
    scores(0, 0, 0)
    jax.lax.fori_loop(0, nq, query_tile, 0)


def _post_kernel(x_ref, o_ref, gya_ref, sgb_ref, p_ref, w_b_ref, w_o_ref, g_mix_ref,
                 g_pre_ref, w_up_ref, conv_w_ref, b_ref, w_down_ref, g_post_ref,
                 w_proj_ref, w_gate_ref, g_ple_ref, out_ref, carry_ref):
    ts = x_ref.shape[0]

    @pl.when(pl.program_id(1) == 0)
    def _():
        carry_ref[...] = jnp.zeros_like(carry_ref)

    yb = _dot(o_ref[...], w_b_ref[...])
    mixed = gya_ref[...].astype(F32) + sgb_ref[...].astype(F32) * yb
    h = x_ref[...] + _rms(_dot(mixed.astype(BF16), w_o_ref[...]), g_mix_ref[...])

    u = _rms(h, g_pre_ref[...]).astype(BF16)

    def conv_cols(col, width):
        cols = slice(col, col + width)
        pre = _dot(u, w_up_ref[:, cols])
        x1, x2 = _shifted_rows(carry_ref.at[:, cols], pre)
        cw = conv_w_ref[:, cols]
        return x2 * cw[0:1, :] + x1 * cw[1:2, :] + pre * cw[2:3, :] + b_ref[:, cols]

    f = jnp.zeros((ts, D_MODEL), F32)
    for off, width in FFN_CHUNKS:
        gate = conv_cols(off, width)
        half_gate = 0.5 * gate
        act = half_gate + half_gate * jnp.tanh(gate * (GELU_C + (GELU_C * 0.044715) * (gate * gate)))
        val = conv_cols(D_FF + off, width)
        f = f + _dot((act * val).astype(BF16), w_down_ref[off:off + width, :])
    h = h + _rms(f, g_post_ref[...])

    e = _dot(p_ref[...].astype(BF16), w_proj_ref[...])
    g = jax.nn.sigmoid(_dot(h.astype(BF16), w_gate_ref[...]))
    out_ref[...] = h + _rms(e * g, g_ple_ref[...])


def _row_spec(ts, width):
    return pl.BlockSpec((None, ts, width), lambda b, s: (b, s, 0))


def _params(sem):
    return pltpu.CompilerParams(dimension_semantics=sem, vmem_limit_bytes=VMEM_LIMIT)


def _layer(x, p, positions, g_mix_pre, w_in, conv_a_w, w_a_out, g_q_lat, w_q_up, g_kv_lat,
           w_kv_up, w_b_out, w_o, g_mix_post, g_ffn_pre, w_ffn_up, conv_ffn_w, b_ffn_conv,
           w_ffn_down, g_ffn_post, w_ple_proj, w_ple_gate, g_ple_post):
    bsz, seq, _ = x.shape
    row = lambda v: v.reshape(1, -1).astype(F32)

    c0 = 3 * CONV_DIM
    c1 = c0 + Q_LORA
    c2 = c1 + KV_LORA
    c3 = c2 + ROPE_DIM
    c4 = c3 + D_MODEL
    half = ROPE_DIM // 2
    w_kr = jnp.pad(w_in[:, c2:c3], ((0, 0), (NOPE_DIM, HEAD_SLOT - QK_DIM)))
    w_abc = w_in[:, 0:c0].astype(BF16)
    w_lat = jnp.concatenate([w_in[:, c0:c2], w_kr], axis=1).astype(BF16)
    w_ga = w_in[:, c3:c4].astype(BF16)
    w_gb = w_in[:, c4:].astype(BF16)

    wq = w_q_up.reshape(Q_LORA, N_HEADS, QK_DIM)
    w_q = jnp.pad(wq, ((0, 0), (0, 0), (0, HEAD_SLOT - QK_DIM))).reshape(Q_LORA, HP).astype(BF16)

    wkv = w_kv_up.reshape(KV_LORA, N_HEADS, NOPE_DIM + V_DIM)
    w_k = jnp.pad(wkv[:, :, :NOPE_DIM], ((0, 0), (0, 0), (0, HEAD_SLOT - NOPE_DIM))).reshape(KV_LORA, HP).astype(BF16)
    w_vt = jnp.pad(wkv[:, :, NOPE_DIM:], ((0, 0), (0, 0), (0, HEAD_SLOT - V_DIM))).reshape(KV_LORA, HP).T.astype(BF16)
    vone = jnp.tile((jnp.arange(HEAD_SLOT) == V_DIM).astype(F32), N_HEADS).reshape(HP, 1)
    w_b = jnp.pad(w_b_out.reshape(N_HEADS, V_DIM, D_MODEL),
                  ((0, 0), (0, HEAD_SLOT - V_DIM), (0, 0))).reshape(HP, D_MODEL).astype(BF16)

    inv = 1.0 / (ROPE_THETA ** (jnp.arange(half, dtype=F32) * (2.0 / ROPE_DIM)))
    invf = jnp.concatenate([inv, inv]).reshape(ROPE_DIM, 1)
    pos3 = positions.reshape(bsz, 1, seq)
    q_scale = (QK_DIM ** -0.5) * math.log2(math.e)

    act = jax.ShapeDtypeStruct((bsz, seq, HP), BF16)
    q, k, vt, gya, sgb = pl.pallas_call(
        functools.partial(_mix_in_kernel, q_scale=q_scale),
        grid=(bsz, seq // TS_MIX),
        in_specs=[_row_spec(TS_MIX, D_MODEL),
                  pl.BlockSpec((None, 1, TS_MIX), lambda b, s: (b, 0, s)),
                  _const_spec((ROPE_DIM, 1)), _const_spec((1, D_MODEL)),
                  _const_spec((D_MODEL, c0)), _const_spec((D_MODEL, LAT_COLS)),
                  _const_spec((D_MODEL, D_MODEL)), _const_spec((D_MODEL, D_MODEL)),
                  _const_spec((3, CONV_DIM)), _const_spec((CONV_DIM, D_MODEL)),
                  _const_spec((1, Q_LORA)), _const_spec((Q_LORA, HP)),
                  _const_spec((1, KV_LORA)), _const_spec((KV_LORA, HP)),
                  _const_spec((HP, KV_LORA)), _const_spec((HP, 1))],
        out_specs=[_row_spec(TS_MIX, HP), _row_spec(TS_MIX, HP),
                   pl.BlockSpec((None, HP, TS_MIX), lambda b, s: (b, 0, s)),
                   _row_spec(TS_MIX, HP), _row_spec(TS_MIX, HP)],
        out_shape=[act, act, jax.ShapeDtypeStruct((bsz, HP, seq), BF16), act, act],
        scratch_shapes=[pltpu.VMEM((SUBLANES, CONV_DIM), F32)],
        compiler_params=_params(("arbitrary", "arbitrary")),
        name="mix_in",
    )(x, pos3, invf, row(g_mix_pre), w_abc, w_lat, w_ga, w_gb, conv_a_w.astype(F32),
      w_a_out.astype(BF16), row(g_q_lat), w_q, row(g_kv_lat), w_k, w_vt, vone)

    group = HEADS_PER_STEP * HEAD_SLOT
    head_cols = pl.BlockSpec((None, seq, group), lambda b, h: (b, 0, h))
    o = pl.pallas_call(
        _attn_kernel,
        grid=(bsz, N_HEADS // HEADS_PER_STEP),
        in_specs=[head_cols, head_cols,
                  pl.BlockSpec((None, group, seq), lambda b, h: (b, h, 0))],
        out_specs=head_cols,
        out_shape=act,
        scratch_shapes=[pltpu.VMEM((2, HEADS_PER_STEP, TKEY, TQ), F32),
                        pltpu.VMEM((2, HEADS_PER_STEP, 1, TQ), F32),
                        pltpu.VMEM((HEADS_PER_STEP, 1, TQ), F32),
                        pltpu.VMEM((HEADS_PER_STEP, HEAD_SLOT, TQ), F32),
                        pltpu.VMEM((2, HEADS_PER_STEP, HEAD_SLOT, TQ), BF16)],
        compiler_params=_params(("arbitrary", "arbitrary")),
        name="attn",
    )(q, k, vt)

    h3 = pl.pallas_call(
        _post_kernel,
        grid=(bsz, seq // TS_POST),
        in_specs=[_row_spec(TS_POST, D_MODEL), _row_spec(TS_POST, HP), _row_spec(TS_POST, HP),
                  _row_spec(TS_POST, HP), _row_spec(TS_POST, PLE_DIM),
                  _const_spec((HP, D_MODEL)), _const_spec((D_MODEL, D_MODEL)),
                  _const_spec((1, D_MODEL)), _const_spec((1, D_MODEL)),
                  _const_spec((D_MODEL, 2 * D_FF)), _const_spec((3, 2 * D_FF)),
                  _const_spec((1, 2 * D_FF)), _const_spec((D_FF, D_MODEL)),
                  _const_spec((1, D_MODEL)), _const_spec((PLE_DIM, D_MODEL)),
                  _const_spec((D_MODEL, D_MODEL)), _const_spec((1, D_MODEL))],
        out_specs=_row_spec(TS_POST, D_MODEL),
        out_shape=jax.ShapeDtypeStruct(x.shape, F32),
        scratch_shapes=[pltpu.VMEM((SUBLANES, 2 * D_FF), F32)],
        compiler_params=_params(("arbitrary", "arbitrary")),
        name="post",
    )(x, o, gya, sgb, p, w_b, w_o.astype(BF16), row(g_mix_post), row(g_ffn_pre),
      w_ffn_up.astype(BF16), conv_ffn_w.astype(F32), row(b_ffn_conv), w_ffn_down.astype(BF16),
      row(g_ffn_post), w_ple_proj.astype(BF16), w_ple_gate.astype(BF16), row(g_ple_post))
    return h3


def kernel(x, p, positions, g_mix_pre, w_in, conv_a_w, w_a_out, g_q_lat, w_q_up, g_kv_lat, w_kv_up, w_b_out, w_o, g_mix_post, g_ffn_pre, w_ffn_up, conv_ffn_w, b_ffn_conv, w_ffn_down, g_ffn_post, w_ple_proj, w_ple_gate, g_ple_post):
    h = x
    for i in range(g_mix_pre.shape[0]):
        h = _layer(h, p[i], positions, g_mix_pre[i], w_in[i], conv_a_w[i], w_a_out[i], g_q_lat[i],
                   w_q_up[i], g_kv_lat[i], w_kv_up[i], w_b_out[i], w_o[i], g_mix_post[i],
                   g_ffn_pre[i], w_ffn_up[i], conv_ffn_w[i], b_ffn_conv[i], w_ffn_down[i],
                   g_ffn_post[i], w_ple_proj[i], w_ple_gate[i], g_ple_post[i])
    return h
```

```python
import functools
import math

import jax
import jax.numpy as jnp
from jax.experimental import pallas as pl
from jax.experimental.pallas import tpu as pltpu

D_MODEL = 1024
CHUNK = 64
CONV_DIM = 512
N_HEADS = 8
NOPE_DIM = 64
ROPE_DIM = 32
V_DIM = 64
Q_LORA = 384
KV_LORA = 256
ROPE_THETA = 10000.0
D_FF = 2816
PLE_DIM = 256
EPS = 1e-6
QK_DIM = NOPE_DIM + ROPE_DIM

LANES = 128
SUBLANES = 8
HEAD_SLOT = LANES
HP = N_HEADS * HEAD_SLOT
LAT_COLS = Q_LORA + KV_LORA + LANES
VMEM_LIMIT = 56 * 1024 * 1024

TS_MIX = 1024
TQ = 1024
TKEY = TQ // 2
HEADS_PER_STEP = 2
V_ROWS = 80
TS_POST = 512

GELU_C = math.sqrt(2.0 / math.pi)
NEG_BIG = -1e30
BF16 = jnp.bfloat16
F32 = jnp.float32


def _rms(x, g):
    return x * jax.lax.rsqrt(jnp.mean(x * x, axis=-1, keepdims=True) + EPS) * g


def _dot(a, b):
    return jnp.dot(a, b, preferred_element_type=F32)


def _const_spec(shape):
    return pl.BlockSpec(shape, lambda *_: (0,) * len(shape))


def _shifted_rows(carry_ref, pre):
    rows = pre.shape[0]
    ext = jnp.concatenate([carry_ref[...], pre], axis=0)
    carry_ref[...] = pre[rows - SUBLANES:rows, :]
    return pltpu.roll(ext, 1, 0)[SUBLANES:], pltpu.roll(ext, 2, 0)[SUBLANES:]


def _mix_in_kernel(x_ref, pos_ref, invf_ref, g_pre_ref, w_abc_ref, w_lat_ref,
                   w_ga_ref, w_gb_ref, conv_w_ref, w_a_out_ref, g_q_ref, w_q_ref,
                   g_kv_ref, w_k_ref, w_v_ref, vone_ref,
                   q_ref, k_ref, v_ref, gya_ref, sgb_ref,
                   carry_ref, *, q_scale):
    ts = x_ref.shape[0]

    @pl.when(pl.program_id(1) == 0)
    def _():
        carry_ref[...] = jnp.zeros_like(carry_ref)

    u = _rms(x_ref[...], g_pre_ref[...]).astype(BF16)

    abc = _dot(u, w_abc_ref[...])
    a_b = abc[:, 0:CONV_DIM]
    ca = abc[:, CONV_DIM:2 * CONV_DIM] * abc[:, 2 * CONV_DIM:3 * CONV_DIM]
    x1, x2 = _shifted_rows(carry_ref, ca)
    cw = conv_w_ref[...]
    cv = x2 * cw[0:1, :] + x1 * cw[1:2, :] + ca * cw[2:3, :]
    ya = _dot((a_b * cv).astype(BF16), w_a_out_ref[...])
    gya_ref[...] = (jax.nn.sigmoid(_dot(u, w_ga_ref[...])) * ya).astype(BF16)
    sgb_ref[...] = jax.nn.sigmoid(_dot(u, w_gb_ref[...])).astype(BF16)

    lat = _dot(u, w_lat_ref[...])
    q_lat = lat[:, 0:Q_LORA]
    kv_lat = lat[:, Q_LORA:Q_LORA + KV_LORA]
    kr = lat[:, Q_LORA + KV_LORA:LAT_COLS]

    half = ROPE_DIM // 2
    ang_t = invf_ref[...] * pos_ref[...].astype(F32)
    sin_t = jnp.sin(ang_t)
    rows = lambda n: jnp.zeros((n, ts), F32)
    tail = HEAD_SLOT - QK_DIM
    cos = jnp.concatenate([jnp.ones((NOPE_DIM, ts), F32), jnp.cos(ang_t), rows(tail)], axis=0).T
    sin_dn = jnp.concatenate([rows(NOPE_DIM), -sin_t[0:half], rows(half + tail)], axis=0).T
    sin_up = jnp.concatenate([rows(NOPE_DIM + half), sin_t[half:], rows(tail)], axis=0).T

    def rope(t, c, s_dn, s_up):
        return (t * c + pltpu.roll(t, HEAD_SLOT - half, 1) * s_dn + pltpu.roll(t, half, 1) * s_up)

    qn = _rms(q_lat, g_q_ref[...]).astype(BF16)
    qq = _dot(qn, w_q_ref[...])
    kvn = _rms(kv_lat, g_kv_ref[...]).astype(BF16)
    kk = _dot(kvn, w_k_ref[...])
    vt = jax.lax.dot_general(w_v_ref[...], kvn, (((1,), (1,)), ((), ())),
                             preferred_element_type=F32) + vone_ref[...]
    v_ref[...] = vt.astype(BF16)

    k_pe = rope(kr, cos, sin_dn, sin_up)
    q_tables = (cos * q_scale, sin_dn * q_scale, sin_up * q_scale)
    for h in range(N_HEADS):
        sl = slice(h * HEAD_SLOT, (h + 1) * HEAD_SLOT)
        q_ref[:, sl] = rope(qq[:, sl], *q_tables).astype(BF16)
        k_ref[:, sl] = (kk[:, sl] + k_pe).astype(BF16)


def _attn_kernel(q_ref, k_ref, vt_ref, o_ref, s_ref, smax_ref, m_ref, acc_ref, qt_ref):
    tk, tq = s_ref.shape[-2:]
    assert tq == 2 * tk
    nq = q_ref.shape[0] // tq
    heads = range(HEADS_PER_STEP)
    lanes = lambda h: slice(h * HEAD_SLOT, (h + 1) * HEAD_SLOT)
    v_rows = lambda h: slice(h * HEAD_SLOT, h * HEAD_SLOT + V_ROWS)
    key_tile = lambda j: pl.ds(pl.multiple_of(j * tk, tk), tk)
    every = slice(0, tq)
    later = slice(tk, tq)

    def transpose_queries(qi):
        for h in heads:
            qt_ref[qi % 2, h] = q_ref[pl.ds(pl.multiple_of(qi * tq, tq), tq), lanes(h)].T

    def scores(qi, j, buf):
        for h in heads:
            sc = _dot(k_ref[key_tile(j), lanes(h)], qt_ref[qi % 2, h])
            s_ref[buf, h] = sc
            smax_ref[buf, h] = jnp.max(sc, axis=0, keepdims=True)

    def scores_later(qi, buf):
        for h in heads:
            s_ref[buf, h, :, later] = _dot(k_ref[key_tile(2 * qi + 1), lanes(h)],
                                           qt_ref[qi % 2, h, :, later])

    def accumulate(j, buf, cols, masked):
        width = cols.stop - cols.start
        if masked:
            ck = jax.lax.broadcasted_iota(jnp.int32, (tk, width), 0) // CHUNK
            cq = jax.lax.broadcasted_iota(jnp.int32, (tk, width), 1) // CHUNK
            load = lambda h: jnp.where(ck <= cq, s_ref[buf, h, :, cols], NEG_BIG)
            tile_max = lambda h: jnp.max(load(h), axis=0, keepdims=True)
        else:
            load = lambda h: s_ref[buf, h, :, cols]
            tile_max = lambda h: smax_ref[buf, h, :, cols]
        for h in heads:
            m = m_ref[h, :, cols]
            m_new = jnp.maximum(m, tile_max(h))
            p = jnp.exp2(load(h) - m_new)
            pv = _dot(vt_ref[v_rows(h), key_tile(j)], p.astype(BF16))
            acc_ref[h, 0:V_ROWS, cols] = acc_ref[h, 0:V_ROWS, cols] * jnp.exp2(m - m_new) + pv
            m_ref[h, :, cols] = m_new

    def query_tile(qi, _):
        next_qi = jnp.minimum(qi + 1, nq - 1)
        m_ref[...] = jnp.full_like(m_ref, NEG_BIG)
        acc_ref[...] = jnp.zeros_like(acc_ref)

        @pl.when(qi + 1 < nq)
        def _():
            transpose_queries(qi + 1)

        def tile_pair(i):
            scores(qi, 2 * i + 1, 1)
            accumulate(2 * i, 0, every, False)
            scores(qi, 2 * i + 2, 0)
            accumulate(2 * i + 1, 1, every, False)

        def two_pairs(i, _):
            tile_pair(2 * i)
            tile_pair(2 * i + 1)
            return 0

        jax.lax.fori_loop(0, qi // 2, two_pairs, 0)

        @pl.when(qi % 2 == 1)
        def _():
            tile_pair(qi - 1)

        scores_later(qi, 1)
        accumulate(2 * qi, 0, every, True)
        scores(next_qi, 0, 0)
        accumulate(2 * qi + 1, 1, later, True)

        for h in heads:
            acc = acc_ref[h]
            o_ref[pl.ds(pl.multiple_of(qi * tq, tq), tq), lanes(h)] = (
                acc / acc[V_DIM:V_DIM + 1, :]).T.astype(BF16)
        return 0

    transpose_queries(0)
    scores(0, 0, 0)
    jax.lax.fori_loop(0, nq, query_tile, 0)


def _post_kernel(x_ref, o_ref, gya_ref, sgb_ref, p_ref, w_b_ref, w_o_ref, g_mix_ref,
                 g_pre_ref, w_up_ref, conv_w_ref, b_ref, w_down_ref, g_post_ref,
                 w_proj_ref, w_gate_ref, g_ple_ref, out_ref, carry_ref):
    @pl.when(pl.program_id(1) == 0)
    def _():
        carry_ref[...] = jnp.zeros_like(carry_ref)

    yb = _dot(o_ref[...], w_b_ref[...])
    mixed = gya_ref[...].astype(F32) + sgb_ref[...].astype(F32) * yb
    h = x_ref[...] + _rms(_dot(mixed.astype(BF16), w_o_ref[...]), g_mix_ref[...])

    u = _rms(h, g_pre_ref[...]).astype(BF16)

    def conv_half(col):
        cols = slice(col, col + D_FF)
        pre = _dot(u, w_up_ref[:, cols])
        x1, x2 = _shifted_rows(carry_ref.at[:, cols], pre)
        cw = conv_w_ref[:, cols]
        return x2 * cw[0:1, :] + x1 * cw[1:2, :] + pre * cw[2:3, :] + b_ref[:, cols]

    gate = conv_half(0)
    half_gate = 0.5 * gate
    act = half_gate + half_gate * jnp.tanh(gate * (GELU_C + (GELU_C * 0.044715) * (gate * gate)))
    f = _dot((act * conv_half(D_FF)).astype(BF16), w_down_ref[...])
    h = h + _rms(f, g_post_ref[...])

    e = _dot(p_ref[...].astype(BF16), w_proj_ref[...])
    g = jax.nn.sigmoid(_dot(h.astype(BF16), w_gate_ref[...]))
    out_ref[...] = h + _rms(e * g, g_ple_ref[...])


def _row_spec(ts, width):
    return pl.BlockSpec((None, ts, width), lambda b, s: (b, s, 0))


def _params(sem):
    return pltpu.CompilerParams(dimension_semantics=sem, vmem_limit_bytes=VMEM_LIMIT)


def _layer(x, p, positions, g_mix_pre, w_in, conv_a_w, w_a_out, g_q_lat, w_q_up, g_kv_lat,
           w_kv_up, w_b_out, w_o, g_mix_post, g_ffn_pre, w_ffn_up, conv_ffn_w, b_ffn_conv,
           w_ffn_down, g_ffn_post, w_ple_proj, w_ple_gate, g_ple_post):
    bsz, seq, _ = x.shape
    row = lambda v: v.reshape(1, -1).astype(F32)

    c0 = 3 * CONV_DIM
    c1 = c0 + Q_LORA
    c2 = c1 + KV_LORA
    c3 = c2 + ROPE_DIM
    c4 = c3 + D_MODEL
    half = ROPE_DIM // 2
    w_kr = jnp.pad(w_in[:, c2:c3], ((0, 0), (NOPE_DIM, HEAD_SLOT - QK_DIM)))
    w_abc = w_in[:, 0:c0].astype(BF16)
    w_lat = jnp.concatenate([w_in[:, c0:c2], w_kr], axis=1).astype(BF16)
    w_ga = w_in[:, c3:c4].astype(BF16)
    w_gb = w_in[:, c4:].astype(BF16)

    wq = w_q_up.reshape(Q_LORA, N_HEADS, QK_DIM)
    w_q = jnp.pad(wq, ((0, 0), (0, 0), (0, HEAD_SLOT - QK_DIM))).reshape(Q_LORA, HP).astype(BF16)

    wkv = w_kv_up.reshape(KV_LORA, N_HEADS, NOPE_DIM + V_DIM)
    w_k = jnp.pad(wkv[:, :, :NOPE_DIM], ((0, 0), (0, 0), (0, HEAD_SLOT - NOPE_DIM))).reshape(KV_LORA, HP).astype(BF16)
    w_vt = jnp.pad(wkv[:, :, NOPE_DIM:], ((0, 0), (0, 0), (0, HEAD_SLOT - V_DIM))).reshape(KV_LORA, HP).T.astype(BF16)
    vone = jnp.tile((jnp.arange(HEAD_SLOT) == V_DIM).astype(F32), N_HEADS).reshape(HP, 1)
    w_b = jnp.pad(w_b_out.reshape(N_HEADS, V_DIM, D_MODEL),
                  ((0, 0), (0, HEAD_SLOT - V_DIM), (0, 0))).reshape(HP, D_MODEL).astype(BF16)

    inv = 1.0 / (ROPE_THETA ** (jnp.arange(half, dtype=F32) * (2.0 / ROPE_DIM)))
    invf = jnp.concatenate([inv, inv]).reshape(ROPE_DIM, 1)
    pos3 = positions.reshape(bsz, 1, seq)
    q_scale = (QK_DIM ** -0.5) * math.log2(math.e)

    act = jax.ShapeDtypeStruct((bsz, seq, HP), BF16)
    q, k, vt, gya, sgb = pl.pallas_call(
        functools.partial(_mix_in_kernel, q_scale=q_scale),
        grid=(bsz, seq // TS_MIX),
        in_specs=[_row_spec(TS_MIX, D_MODEL),
                  pl.BlockSpec((None, 1, TS_MIX), lambda b, s: (b, 0, s)),
                  _const_spec((ROPE_DIM, 1)), _const_spec((1, D_MODEL)),
                  _const_spec((D_MODEL, c0)), _const_spec((D_MODEL, LAT_COLS)),
                  _const_spec((D_MODEL, D_MODEL)), _const_spec((D_MODEL, D_MODEL)),
                  _const_spec((3, CONV_DIM)), _const_spec((CONV_DIM, D_MODEL)),
                  _const_spec((1, Q_LORA)), _const_spec((Q_LORA, HP)),
                  _const_spec((1, KV_LORA)), _const_spec((KV_LORA, HP)),
                  _const_spec((HP, KV_LORA)), _const_spec((HP, 1))],
        out_specs=[_row_spec(TS_MIX, HP), _row_spec(TS_MIX, HP),
                   pl.BlockSpec((None, HP, TS_MIX), lambda b, s: (b, 0, s)),
                   _row_spec(TS_MIX, HP), _row_spec(TS_MIX, HP)],
        out_shape=[act, act, jax.ShapeDtypeStruct((bsz, HP, seq), BF16), act, act],
        scratch_shapes=[pltpu.VMEM((SUBLANES, CONV_DIM), F32)],
        compiler_params=_params(("arbitrary", "arbitrary")),
        name="mix_in",
    )(x, pos3, invf, row(g_mix_pre), w_abc, w_lat, w_ga, w_gb, conv_a_w.astype(F32),
      w_a_out.astype(BF16), row(g_q_lat), w_q, row(g_kv_lat), w_k, w_vt, vone)

    group = HEADS_PER_STEP * HEAD_SLOT
    head_cols = pl.BlockSpec((None, seq, group), lambda b, h: (b, 0, h))
    o = pl.pallas_call(
        _attn_kernel,
        grid=(bsz, N_HEADS // HEADS_PER_STEP),
        in_specs=[head_cols, head_cols,
                  pl.BlockSpec((None, group, seq), lambda b, h: (b, h, 0))],
        out_specs=head_cols,
        out_shape=act,
        scratch_shapes=[pltpu.VMEM((2, HEADS_PER_STEP, TKEY, TQ), F32),
                        pltpu.VMEM((2, HEADS_PER_STEP, 1, TQ), F32),
                        pltpu.VMEM((HEADS_PER_STEP, 1, TQ), F32),
                        pltpu.VMEM((HEADS_PER_STEP, HEAD_SLOT, TQ), F32),
                        pltpu.VMEM((2, HEADS_PER_STEP, HEAD_SLOT, TQ), BF16)],
        compiler_params=_params(("arbitrary", "arbitrary")),
        name="attn",
    )(q, k, vt)

    h3 = pl.pallas_call(
        _post_kernel,
        grid=(bsz, seq // TS_POST),
        in_specs=[_row_spec(TS_POST, D_MODEL), _row_spec(TS_POST, HP), _row_spec(TS_POST, HP),
                  _row_spec(TS_POST, HP), _row_spec(TS_POST, PLE_DIM),
                  _const_spec((HP, D_MODEL)), _const_spec((D_MODEL, D_MODEL)),
                  _const_spec((1, D_MODEL)), _const_spec((1, D_MODEL)),
                  _const_spec((D_MODEL, 2 * D_FF)), _const_spec((3, 2 * D_FF)),
                  _const_spec((1, 2 * D_FF)), _const_spec((D_FF, D_MODEL)),
                  _const_spec((1, D_MODEL)), _const_spec((PLE_DIM, D_MODEL)),
                  _const_spec((D_MODEL, D_MODEL)), _const_spec((1, D_MODEL))],
        out_specs=_row_spec(TS_POST, D_MODEL),
        out_shape=jax.ShapeDtypeStruct(x.shape, F32),
        scratch_shapes=[pltpu.VMEM((SUBLANES, 2 * D_FF), F32)],
        compiler_params=_params(("arbitrary", "arbitrary")),
        name="post",
    )(x, o, gya, sgb, p, w_b, w_o.astype(BF16), row(g_mix_post), row(g_ffn_pre),
      w_ffn_up.astype(BF16), conv_ffn_w.astype(F32), row(b_ffn_conv), w_ffn_down.astype(BF16),
      row(g_ffn_post), w_ple_proj.astype(BF16), w_ple_gate.astype(BF16), row(g_ple_post))
    return h3


def kernel(x, p, positions, g_mix_pre, w_in, conv_a_w, w_a_out, g_q_lat, w_q_up, g_kv_lat, w_kv_up, w_b_out, w_o, g_mix_post, g_ffn_pre, w_ffn_up, conv_ffn_w, b_ffn_conv, w_ffn_down, g_ffn_post, w_ple_proj, w_ple_gate, g_ple_post):
    h = x
    for i in range(g_mix_pre.shape[0]):
        h = _layer(h, p[i], positions, g_mix_pre[i], w_in[i], conv_a_w[i], w_a_out[i], g_q_lat[i],
                   w_q_up[i], g_kv_lat[i], w_kv_up[i], w_b_out[i], w_o[i], g_mix_post[i],
                   g_ffn_pre[i], w_ffn_up[i], conv_ffn_w[i], b_ffn_conv[i], w_ffn_down[i],
                   g_ffn_post[i], w_ple_proj[i], w_ple_gate[i], g_ple_post[i])
    return h
```

```python
import functools
import math

import jax
import jax.numpy as jnp
from jax.experimental import pallas as pl
from jax.experimental.pallas import tpu as pltpu

D_MODEL = 1024
CHUNK = 64
CONV_DIM = 512
N_HEADS = 8
NOPE_DIM = 64
ROPE_DIM = 32
V_DIM = 64
Q_LORA = 384
KV_LORA = 256
ROPE_THETA = 10000.0
D_FF = 2816
PLE_DIM = 256
EPS = 1e-6
QK_DIM = NOPE_DIM + ROPE_DIM

LANES = 128
SUBLANES = 8
HEAD_SLOT = LANES
HP = N_HEADS * HEAD_SLOT
LAT_COLS = Q_LORA + KV_LORA + LANES
VMEM_LIMIT = 56 * 1024 * 1024

TS_MIX = 1024
TQ = 1024
TKEY = TQ // 2
HEADS_PER_STEP = 2
V_ROWS = 80
TS_POST = 512

GELU_C = math.sqrt(2.0 / math.pi)
NEG_BIG = -1e30
BF16 = jnp.bfloat16
F32 = jnp.float32


def _rms(x, g):
    return x * jax.lax.rsqrt(jnp.mean(x * x, axis=-1, keepdims=True) + EPS) * g


def _dot(a, b):
    return jnp.dot(a, b, preferred_element_type=F32)


def _const_spec(shape):
    return pl.BlockSpec(shape, lambda *_: (0,) * len(shape))


def _shifted_rows(carry_ref, pre):
    rows = pre.shape[0]
    ext = jnp.concatenate([carry_ref[...], pre], axis=0)
    carry_ref[...] = pre[rows - SUBLANES:rows, :]
    return pltpu.roll(ext, 1, 0)[SUBLANES:], pltpu.roll(ext, 2, 0)[SUBLANES:]


def _mix_in_kernel(x_ref, pos_ref, invf_ref, g_pre_ref, w_abc_ref, w_lat_ref,
                   w_ga_ref, w_gb_ref, conv_w_ref, w_a_out_ref, g_q_ref, w_q_ref,
                   g_kv_ref, w_k_ref, w_v_ref, vone_ref,
                   q_ref, k_ref, v_ref, gya_ref, sgb_ref,
                   carry_ref, *, q_scale):
    ts = x_ref.shape[0]

    @pl.when(pl.program_id(1) == 0)
    def _():
        carry_ref[...] = jnp.zeros_like(carry_ref)

    u = _rms(x_ref[...], g_pre_ref[...]).astype(BF16)

    lat = _dot(u, w_lat_ref[...])
    q_lat = lat[:, 0:Q_LORA]
    kv_lat = lat[:, Q_LORA:Q_LORA + KV_LORA]
    kr = lat[:, Q_LORA + KV_LORA:LAT_COLS]

    half = ROPE_DIM // 2
    ang_t = invf_ref[...] * pos_ref[...].astype(F32)
    sin_t = jnp.sin(ang_t)
    rows = lambda n: jnp.zeros((n, ts), F32)
    tail = HEAD_SLOT - QK_DIM
    cos = jnp.concatenate([jnp.ones((NOPE_DIM, ts), F32), jnp.cos(ang_t), rows(tail)], axis=0).T
    sin_dn = jnp.concatenate([rows(NOPE_DIM), -sin_t[0:half], rows(half + tail)], axis=0).T
    sin_up = jnp.concatenate([rows(NOPE_DIM + half), sin_t[half:], rows(tail)], axis=0).T

    def rope(t, c, s_dn, s_up):
        return (t * c + pltpu.roll(t, HEAD_SLOT - half, 1) * s_dn + pltpu.roll(t, half, 1) * s_up)

    qn = _rms(q_lat, g_q_ref[...]).astype(BF16)
    qq = _dot(qn, w_q_ref[...])
    kvn = _rms(kv_lat, g_kv_ref[...]).astype(BF16)
    kk = _dot(kvn, w_k_ref[...])
    vt = jax.lax.dot_general(w_v_ref[...], kvn, (((1,), (1,)), ((), ())),
                             preferred_element_type=F32) + vone_ref[...]
    v_ref[...] = vt.astype(BF16)

    k_pe = rope(kr, cos, sin_dn, sin_up)
    q_tables = (cos * q_scale, sin_dn * q_scale, sin_up * q_scale)
    for h in range(N_HEADS):
        sl = slice(h * HEAD_SLOT, (h + 1) * HEAD_SLOT)
        q_ref[:, sl] = rope(qq[:, sl], *q_tables).astype(BF16)
        k_ref[:, sl] = (kk[:, sl] + k_pe).astype(BF16)

    abc = _dot(u, w_abc_ref[...])
    a_b = abc[:, 0:CONV_DIM]
    ca = abc[:, CONV_DIM:2 * CONV_DIM] * abc[:, 2 * CONV_DIM:3 * CONV_DIM]
    x1, x2 = _shifted_rows(carry_ref, ca)
    cw = conv_w_ref[...]
    cv = x2 * cw[0:1, :] + x1 * cw[1:2, :] + ca * cw[2:3, :]
    ya = _dot((a_b * cv).astype(BF16), w_a_out_ref[...])
    gya_ref[...] = (jax.nn.sigmoid(_dot(u, w_ga_ref[...])) * ya).astype(BF16)
    sgb_ref[...] = jax.nn.sigmoid(_dot(u, w_gb_ref[...])).astype(BF16)


def _attn_kernel(q_ref, k_ref, vt_ref, o_ref, s_ref, smax_ref, m_ref, acc_ref, qt_ref):
    tk, tq = s_ref.shape[-2:]
    assert tq == 2 * tk
    nq = q_ref.shape[0] // tq
    heads = range(HEADS_PER_STEP)
    lanes = lambda h: slice(h * HEAD_SLOT, (h + 1) * HEAD_SLOT)
    v_rows = lambda h: slice(h * HEAD_SLOT, h * HEAD_SLOT + V_ROWS)
    key_tile = lambda j: pl.ds(pl.multiple_of(j * tk, tk), tk)
    every = slice(0, tq)
    later = slice(tk, tq)

    def transpose_queries(qi):
        for h in heads:
            qt_ref[qi % 2, h] = q_ref[pl.ds(pl.multiple_of(qi * tq, tq), tq), lanes(h)].T

    def scores(qi, j, buf, hs=heads):
        for h in hs:
            sc = _dot(k_ref[key_tile(j), lanes(h)], qt_ref[qi % 2, h])
            s_ref[buf, h] = sc
            smax_ref[buf, h] = jnp.max(sc, axis=0, keepdims=True)

    def scores_later(qi, buf, hs=heads):
        for h in hs:
            s_ref[buf, h, :, later] = _dot(k_ref[key_tile(2 * qi + 1), lanes(h)],
                                           qt_ref[qi % 2, h, :, later])

    def accumulate(j, buf, cols, masked, hs=heads):
        width = cols.stop - cols.start
        if masked:
            ck = jax.lax.broadcasted_iota(jnp.int32, (tk, width), 0) // CHUNK
            cq = jax.lax.broadcasted_iota(jnp.int32, (tk, width), 1) // CHUNK
            load = lambda h: jnp.where(ck <= cq, s_ref[buf, h, :, cols], NEG_BIG)
            tile_max = lambda h: jnp.max(load(h), axis=0, keepdims=True)
        else:
            load = lambda h: s_ref[buf, h, :, cols]
            tile_max = lambda h: smax_ref[buf, h, :, cols]
        for h in hs:
            m = m_ref[h, :, cols]
            m_new = jnp.maximum(m, tile_max(h))
            p = jnp.exp2(load(h) - m_new)
            pv = _dot(vt_ref[v_rows(h), key_tile(j)], p.astype(BF16))
            acc_ref[h, 0:V_ROWS, cols] = acc_ref[h, 0:V_ROWS, cols] * jnp.exp2(m - m_new) + pv
            m_ref[h, :, cols] = m_new

    def query_tile(qi, _):
        next_qi = jnp.minimum(qi + 1, nq - 1)
        m_ref[...] = jnp.full_like(m_ref, NEG_BIG)
        acc_ref[...] = jnp.zeros_like(acc_ref)

        @pl.when(qi + 1 < nq)
        def _():
            transpose_queries(qi + 1)

        def tile_pair(i):
            for h in heads:
                scores(qi, 2 * i + 1, 1, (h,))
                accumulate(2 * i, 0, every, False, (h,))
            for h in heads:
                scores(qi, 2 * i + 2, 0, (h,))
                accumulate(2 * i + 1, 1, every, False, (h,))

        def two_pairs(i, _):
            tile_pair(2 * i)
            tile_pair(2 * i + 1)
            return 0

        jax.lax.fori_loop(0, qi // 2, two_pairs, 0)

        @pl.when(qi % 2 == 1)
        def _():
            tile_pair(qi - 1)

        for h in heads:
            scores_later(qi, 1, (h,))
            accumulate(2 * qi, 0, every, True, (h,))
        for h in heads:
            scores(next_qi, 0, 0, (h,))
            accumulate(2 * qi + 1, 1, later, True, (h,))

        for h in heads:
            acc = acc_ref[h]
            o_ref[pl.ds(pl.multiple_of(qi * tq, tq), tq), lanes(h)] = (
                acc / acc[V_DIM:V_DIM + 1, :]).T.astype(BF16)
        return 0

    transpose_queries(0)
    scores(0, 0, 0)
    jax.lax.fori_loop(0, nq, query_tile, 0)


def _post_kernel(x_ref, o_ref, gya_ref, sgb_ref, p_ref, w_b_ref, w_o_ref, g_mix_ref,
                 g_pre_ref, w_up_ref, conv_w_ref, b_ref, w_down_ref, g_post_ref,
                 w_proj_ref, w_gate_ref, g_ple_ref, out_ref, carry_ref):
    @pl.when(pl.program_id(1) == 0)
    def _():
        carry_ref[...] = jnp.zeros_like(carry_ref)

    yb = _dot(o_ref[...], w_b_ref[...])
    mixed = gya_ref[...].astype(F32) + sgb_ref[...].astype(F32) * yb
    h = x_ref[...] + _rms(_dot(mixed.astype(BF16), w_o_ref[...]), g_mix_ref[...])

    u = _rms(h, g_pre_ref[...]).astype(BF16)

    def conv_half(col):
        cols = slice(col, col + D_FF)
        pre = _dot(u, w_up_ref[:, cols])
        x1, x2 = _shifted_rows(carry_ref.at[:, cols], pre)
        cw = conv_w_ref[:, cols]
        return x2 * cw[0:1, :] + x1 * cw[1:2, :] + pre * cw[2:3, :] + b_ref[:, cols]

    gate = conv_half(0)
    half_gate = 0.5 * gate
    act = half_gate + half_gate * jnp.tanh(gate * (GELU_C + (GELU_C * 0.044715) * (gate * gate)))
    f = _dot((act * conv_half(D_FF)).astype(BF16), w_down_ref[...])
    h = h + _rms(f, g_post_ref[...])

    e = _dot(p_ref[...].astype(BF16), w_proj_ref[...])
    g = jax.nn.sigmoid(_dot(h.astype(BF16), w_gate_ref[...]))
    out_ref[...] = h + _rms(e * g, g_ple_ref[...])


def _row_spec(ts, width):
    return pl.BlockSpec((None, ts, width), lambda b, s: (b, s, 0))


def _params(sem):
    return pltpu.CompilerParams(dimension_semantics=sem, vmem_limit_bytes=VMEM_LIMIT)


def _layer(x, p, positions, g_mix_pre, w_in, conv_a_w, w_a_out, g_q_lat, w_q_up, g_kv_lat,
           w_kv_up, w_b_out, w_o, g_mix_post, g_ffn_pre, w_ffn_up, conv_ffn_w, b_ffn_conv,
           w_ffn_down, g_ffn_post, w_ple_proj, w_ple_gate, g_ple_post):
    bsz, seq, _ = x.shape
    row = lambda v: v.reshape(1, -1).astype(F32)

    c0 = 3 * CONV_DIM
    c1 = c0 + Q_LORA
    c2 = c1 + KV_LORA
    c3 = c2 + ROPE_DIM
    c4 = c3 + D_MODEL
    half = ROPE_DIM // 2
    w_kr = jnp.pad(w_in[:, c2:c3], ((0, 0), (NOPE_DIM, HEAD_SLOT - QK_DIM)))
    w_abc = w_in[:, 0:c0].astype(BF16)
    w_lat = jnp.concatenate([w_in[:, c0:c2], w_kr], axis=1).astype(BF16)
    w_ga = w_in[:, c3:c4].astype(BF16)
    w_gb = w_in[:, c4:].astype(BF16)

    wq = w_q_up.reshape(Q_LORA, N_HEADS, QK_DIM)
    w_q = jnp.pad(wq, ((0, 0), (0, 0), (0, HEAD_SLOT - QK_DIM))).reshape(Q_LORA, HP).astype(BF16)

    wkv = w_kv_up.reshape(KV_LORA, N_HEADS, NOPE_DIM + V_DIM)
    w_k = jnp.pad(wkv[:, :, :NOPE_DIM], ((0, 0), (0, 0), (0, HEAD_SLOT - NOPE_DIM))).reshape(KV_LORA, HP).astype(BF16)
    w_vt = jnp.pad(wkv[:, :, NOPE_DIM:], ((0, 0), (0, 0), (0, HEAD_SLOT - V_DIM))).reshape(KV_LORA, HP).T.astype(BF16)
    vone = jnp.tile((jnp.arange(HEAD_SLOT) == V_DIM).astype(F32), N_HEADS).reshape(HP, 1)
    w_b = jnp.pad(w_b_out.reshape(N_HEADS, V_DIM, D_MODEL),
                  ((0, 0), (0, HEAD_SLOT - V_DIM), (0, 0))).reshape(HP, D_MODEL).astype(BF16)

    inv = 1.0 / (ROPE_THETA ** (jnp.arange(half, dtype=F32) * (2.0 / ROPE_DIM)))
    invf = jnp.concatenate([inv, inv]).reshape(ROPE_DIM, 1)
    pos3 = positions.reshape(bsz, 1, seq)
    q_scale = (QK_DIM ** -0.5) * math.log2(math.e)

    act = jax.ShapeDtypeStruct((bsz, seq, HP), BF16)
    q, k, vt, gya, sgb = pl.pallas_call(
        functools.partial(_mix_in_kernel, q_scale=q_scale),
        grid=(bsz, seq // TS_MIX),
        in_specs=[_row_spec(TS_MIX, D_MODEL),
                  pl.BlockSpec((None, 1, TS_MIX), lambda b, s: (b, 0, s)),
                  _const_spec((ROPE_DIM, 1)), _const_spec((1, D_MODEL)),
                  _const_spec((D_MODEL, c0)), _const_spec((D_MODEL, LAT_COLS)),
                  _const_spec((D_MODEL, D_MODEL)), _const_spec((D_MODEL, D_MODEL)),
                  _const_spec((3, CONV_DIM)), _const_spec((CONV_DIM, D_MODEL)),
                  _const_spec((1, Q_LORA)), _const_spec((Q_LORA, HP)),
                  _const_spec((1, KV_LORA)), _const_spec((KV_LORA, HP)),
                  _const_spec((HP, KV_LORA)), _const_spec((HP, 1))],
        out_specs=[_row_spec(TS_MIX, HP), _row_spec(TS_MIX, HP),
                   pl.BlockSpec((None, HP, TS_MIX), lambda b, s: (b, 0, s)),
                   _row_spec(TS_MIX, HP), _row_spec(TS_MIX, HP)],
        out_shape=[act, act, jax.ShapeDtypeStruct((bsz, HP, seq), BF16), act, act],
        scratch_shapes=[pltpu.VMEM((SUBLANES, CONV_DIM), F32)],
        compiler_params=_params(("arbitrary", "arbitrary")),
        name="mix_in",
    )(x, pos3, invf, row(g_mix_pre), w_abc, w_lat, w_ga, w_gb, conv_a_w.astype(F32),
      w_a_out.astype(BF16), row(g_q_lat), w_q, row(g_kv_lat), w_k, w_vt, vone)

    group = HEADS_PER_STEP * HEAD_SLOT
    head_cols = pl.BlockSpec((None, seq, group), lambda b, h: (b, 0, h))
    o = pl.pallas_call(
        _attn_kernel,
        grid=(bsz, N_HEADS // HEADS_PER_STEP),
        in_specs=[head_cols, head_cols,
                  pl.BlockSpec((None, group, seq), lambda b, h: (b, h, 0))],
        out_specs=head_cols,
        out_shape=act,
        scratch_shapes=[pltpu.VMEM((2, HEADS_PER_STEP, TKEY, TQ), F32),
                        pltpu.VMEM((2, HEADS_PER_STEP, 1, TQ), F32),
                        pltpu.VMEM((HEADS_PER_STEP, 1, TQ), F32),
                        pltpu.VMEM((HEADS_PER_STEP, HEAD_SLOT, TQ), F32),
                        pltpu.VMEM((2, HEADS_PER_STEP, HEAD_SLOT, TQ), BF16)],
        compiler_params=_params(("arbitrary", "arbitrary")),
        name="attn",
    )(q, k, vt)

    h3 = pl.pallas_call(
        _post_kernel,
        grid=(bsz, seq // TS_POST),
        in_specs=[_row_spec(TS_POST, D_MODEL), _row_spec(TS_POST, HP), _row_spec(TS_POST, HP),
                  _row_spec(TS_POST, HP), _row_spec(TS_POST, PLE_DIM),
                  _const_spec((HP, D_MODEL)), _const_spec((D_MODEL, D_MODEL)),
                  _const_spec((1, D_MODEL)), _const_spec((1, D_MODEL)),
                  _const_spec((D_MODEL, 2 * D_FF)), _const_spec((3, 2 * D_FF)),
                  _const_spec((1, 2 * D_FF)), _const_spec((D_FF, D_MODEL)),
                  _const_spec((1, D_MODEL)), _const_spec((PLE_DIM, D_MODEL)),
                  _const_spec((D_MODEL, D_MODEL)), _const_spec((1, D_MODEL))],
        out_specs=_row_spec(TS_POST, D_MODEL),
        out_shape=jax.ShapeDtypeStruct(x.shape, F32),
        scratch_shapes=[pltpu.VMEM((SUBLANES, 2 * D_FF), F32)],
        compiler_params=_params(("arbitrary", "arbitrary")),
        name="post",
    )(x, o, gya, sgb, p, w_b, w_o.astype(BF16), row(g_mix_post), row(g_ffn_pre),
      w_ffn_up.astype(BF16), conv_ffn_w.astype(F32), row(b_ffn_conv), w_ffn_down.astype(BF16),
      row(g_ffn_post), w_ple_proj.astype(BF16), w_ple_gate.astype(BF16), row(g_ple_post))
    return h3


def kernel(x, p, positions, g_mix_pre, w_in, conv_a_w, w_a_out, g_q_lat, w_q_up, g_kv_lat, w_kv_up, w_b_out, w_o, g_mix_post, g_ffn_pre, w_ffn_up, conv_ffn_w, b_ffn_conv, w_ffn_down, g_ffn_post, w_ple_proj, w_ple_gate, g_ple_post):
    h = x
    for i in range(g_mix_pre.shape[0]):
        h = _layer(h, p[i], positions, g_mix_pre[i], w_in[i], conv_a_w[i], w_a_out[i], g_q_lat[i],
                   w_q_up[i], g_kv_lat[i], w_kv_up[i], w_b_out[i], w_o[i], g_mix_post[i],
                   g_ffn_pre[i], w_ffn_up[i], conv_ffn_w[i], b_ffn_conv[i], w_ffn_down[i],
                   g_ffn_post[i], w_ple_proj[i], w_ple_gate[i], g_ple_post[i])
    return h
```

```python
import functools
import math

import jax
import jax.numpy as jnp
from jax.experimental import pallas as pl
from jax.experimental.pallas import tpu as pltpu

D_MODEL = 1024
CHUNK = 64
CONV_DIM = 512
N_HEADS = 8
NOPE_DIM = 64
ROPE_DIM = 32
V_DIM = 64
Q_LORA = 384
KV_LORA = 256
ROPE_THETA = 10000.0
D_FF = 2816
PLE_DIM = 256
EPS = 1e-6
QK_DIM = NOPE_DIM + ROPE_DIM

LANES = 128
SUBLANES = 8
BF16_SUBLANES = 16
N_LATER_WEIGHTS = 5
HEAD_SLOT = LANES
HP = N_HEADS * HEAD_SLOT
LAT_COLS = Q_LORA + KV_LORA + LANES
VMEM_LIMIT = 56 * 1024 * 1024

TS_MIX = 512
TQ = 1024
TKEY = TQ // 2
HEADS_PER_STEP = 2
V_ROWS = 80
TS_POST = 512

GELU_C = math.sqrt(2.0 / math.pi)
NEG_BIG = -1e30
BF16 = jnp.bfloat16
F32 = jnp.float32


def _rms(x, g):
    return x * jax.lax.rsqrt(jnp.mean(x * x, axis=-1, keepdims=True) + EPS) * g


def _dot(a, b):
    return jnp.dot(a, b, preferred_element_type=F32)


def _const_spec(shape):
    return pl.BlockSpec(shape, lambda *_: (0,) * len(shape))


def _shifted_rows(carry_ref, pre):
    rows = pre.shape[0]
    ext = jnp.concatenate([carry_ref[...], pre], axis=0)
    carry_ref[...] = pre[rows - SUBLANES:rows, :]
    return pltpu.roll(ext, 1, 0)[SUBLANES:], pltpu.roll(ext, 2, 0)[SUBLANES:]


def _mix_in_kernel(x_ref, pos_ref, invf_ref, g_pre_ref, w_abc_ref, w_lat_ref,
                   w_ga_ref, w_gb_ref, conv_w_ref, w_a_out_ref, g_q_ref, w_q_ref,
                   g_kv_ref, w_k_ref, w_v_ref, vone_ref, *rest, q_scale):
    later_f32 = rest[:N_LATER_WEIGHTS]
    q_ref, k_ref, v_ref, gya_ref, sgb_ref = rest[N_LATER_WEIGHTS:N_LATER_WEIGHTS + 5]
    later_bf16 = rest[N_LATER_WEIGHTS + 5:-1]
    carry_ref = rest[-1]
    ts = x_ref.shape[0]

    for src, dst in zip(later_f32, later_bf16):
        dst[...] = src[...].astype(BF16)

    @pl.when(pl.program_id(1) == 0)
    def _():
        carry_ref[...] = jnp.zeros_like(carry_ref)

    u = _rms(x_ref[...], g_pre_ref[...]).astype(BF16)

    lat = _dot(u, w_lat_ref[...])
    q_lat = lat[:, 0:Q_LORA]
    kv_lat = lat[:, Q_LORA:Q_LORA + KV_LORA]
    kr = lat[:, Q_LORA + KV_LORA:LAT_COLS]

    half = ROPE_DIM // 2
    ang_t = invf_ref[...] * pos_ref[...].astype(F32)
    sin_t = jnp.sin(ang_t)
    rows = lambda n: jnp.zeros((n, ts), F32)
    tail = HEAD_SLOT - QK_DIM
    cos = jnp.concatenate([jnp.ones((NOPE_DIM, ts), F32), jnp.cos(ang_t), rows(tail)], axis=0).T
    sin_dn = jnp.concatenate([rows(NOPE_DIM), -sin_t[0:half], rows(half + tail)], axis=0).T
    sin_up = jnp.concatenate([rows(NOPE_DIM + half), sin_t[half:], rows(tail)], axis=0).T

    def rope(t, c, s_dn, s_up):
        return (t * c + pltpu.roll(t, HEAD_SLOT - half, 1) * s_dn + pltpu.roll(t, half, 1) * s_up)

    qn = _rms(q_lat, g_q_ref[...]).astype(BF16)
    qq = _dot(qn, w_q_ref[...])
    kvn = _rms(kv_lat, g_kv_ref[...]).astype(BF16)
    kk = _dot(kvn, w_k_ref[...])
    vt = jax.lax.dot_general(w_v_ref[...], kvn, (((1,), (1,)), ((), ())),
                             preferred_element_type=F32) + vone_ref[...]
    v_ref[...] = vt.astype(BF16)

    k_pe = rope(kr, cos, sin_dn, sin_up)
    q_tables = (cos * q_scale, sin_dn * q_scale, sin_up * q_scale)
    for h in range(N_HEADS):
        sl = slice(h * HEAD_SLOT, (h + 1) * HEAD_SLOT)
        q_ref[:, sl] = rope(qq[:, sl], *q_tables).astype(BF16)
        k_ref[:, sl] = (kk[:, sl] + k_pe).astype(BF16)

    abc = _dot(u, w_abc_ref[...])
    a_b = abc[:, 0:CONV_DIM]
    ca = abc[:, CONV_DIM:2 * CONV_DIM] * abc[:, 2 * CONV_DIM:3 * CONV_DIM]
    x1, x2 = _shifted_rows(carry_ref, ca)
    cw = conv_w_ref[...]
    cv = x2 * cw[0:1, :] + x1 * cw[1:2, :] + ca * cw[2:3, :]
    ya = _dot((a_b * cv).astype(BF16), w_a_out_ref[...])
    gya_ref[...] = (jax.nn.sigmoid(_dot(u, w_ga_ref[...])) * ya).astype(BF16)
    sgb_ref[...] = jax.nn.sigmoid(_dot(u, w_gb_ref[...])).astype(BF16)


def _attn_kernel(q_ref, k_ref, vt_ref, o_ref, s_ref, smax_ref, m_ref, acc_ref, qt_ref):
    tk, tq = s_ref.shape[-2:]
    assert tq == 2 * tk
    nq = q_ref.shape[0] // tq
    heads = range(HEADS_PER_STEP)
    lanes = lambda h: slice(h * HEAD_SLOT, (h + 1) * HEAD_SLOT)
    v_rows = lambda h: slice(h * HEAD_SLOT, h * HEAD_SLOT + V_ROWS)
    key_tile = lambda j: pl.ds(pl.multiple_of(j * tk, tk), tk)
    every = slice(0, tq)
    later = slice(tk, tq)

    def transpose_queries(qi):
        for h in heads:
            qt_ref[qi % 2, h] = q_ref[pl.ds(pl.multiple_of(qi * tq, tq), tq), lanes(h)].T

    def scores(qi, j, buf, hs=heads):
        for h in hs:
            sc = _dot(k_ref[key_tile(j), lanes(h)], qt_ref[qi % 2, h])
            s_ref[buf, h] = sc
            smax_ref[buf, h] = jnp.max(sc, axis=0, keepdims=True)

    def scores_later(qi, buf, hs=heads):
        for h in hs:
            s_ref[buf, h, :, later] = _dot(k_ref[key_tile(2 * qi + 1), lanes(h)],
                                           qt_ref[qi % 2, h, :, later])

    def accumulate(j, buf, cols, masked, hs=heads):
        width = cols.stop - cols.start
        if masked:
            ck = jax.lax.broadcasted_iota(jnp.int32, (tk, width), 0) // CHUNK
            cq = jax.lax.broadcasted_iota(jnp.int32, (tk, width), 1) // CHUNK
            load = lambda h: jnp.where(ck <= cq, s_ref[buf, h, :, cols], NEG_BIG)
            tile_max = lambda h: jnp.max(load(h), axis=0, keepdims=True)
        else:
            load = lambda h: s_ref[buf, h, :, cols]
            tile_max = lambda h: smax_ref[buf, h, :, cols]
        for h in hs:
            m = m_ref[h, :, cols]
            m_new = jnp.maximum(m, tile_max(h))
            p = jnp.exp2(load(h) - m_new)
            pv = _dot(vt_ref[v_rows(h), key_tile(j)], p.astype(BF16))
            acc_ref[h, 0:V_ROWS, cols] = acc_ref[h, 0:V_ROWS, cols] * jnp.exp2(m - m_new) + pv
            m_ref[h, :, cols] = m_new

    def query_tile(qi, _):
        next_qi = jnp.minimum(qi + 1, nq - 1)
        m_ref[...] = jnp.full_like(m_ref, NEG_BIG)
        acc_ref[...] = jnp.zeros_like(acc_ref)

        @pl.when(qi + 1 < nq)
        def _():
            transpose_queries(qi + 1)

        def tile_pair(i):
            for h in heads:
                scores(qi, 2 * i + 1, 1, (h,))
                accumulate(2 * i, 0, every, False, (h,))
            for h in heads:
                scores(qi, 2 * i + 2, 0, (h,))
                accumulate(2 * i + 1, 1, every, False, (h,))

        def two_pairs(i, _):
            tile_pair(2 * i)
            tile_pair(2 * i + 1)
            return 0

        jax.lax.fori_loop(0, qi // 2, two_pairs, 0)

        @pl.when(qi % 2 == 1)
        def _():
            tile_pair(qi - 1)

        for h in heads:
            scores_later(qi, 1, (h,))
            accumulate(2 * qi, 0, every, True, (h,))
        for h in heads:
            scores(next_qi, 0, 0, (h,))
            accumulate(2 * qi + 1, 1, later, True, (h,))

        for h in heads:
            acc = acc_ref[h]
            o_ref[pl.ds(pl.multiple_of(qi * tq, tq), tq), lanes(h)] = (
                acc / acc[V_DIM:V_DIM + 1, :]).T.astype(BF16)
        return 0

    transpose_queries(0)
    scores(0, 0, 0)
    jax.lax.fori_loop(0, nq, query_tile, 0)


def _post_kernel(x_ref, o_ref, gya_ref, sgb_ref, p_ref, w_b_ref, w_o_ref, g_mix_ref,
                 g_pre_ref, w_up_ref, conv_w_ref, b_ref, w_down_ref, g_post_ref,
                 w_proj_ref, w_gate_ref, g_ple_ref, out_ref, carry_ref):
    @pl.when(pl.program_id(1) == 0)
    def _():
        carry_ref[...] = jnp.zeros_like(carry_ref)

    yb = _dot(o_ref[...], w_b_ref[...])
    mixed = gya_ref[...].astype(F32) + sgb_ref[...].astype(F32) * yb
    h = x_ref[...] + _rms(_dot(mixed.astype(BF16), w_o_ref[...]), g_mix_ref[...])

    u = _rms(h, g_pre_ref[...]).astype(BF16)

    def conv_half(col):
        cols = slice(col, col + D_FF)
        pre = _dot(u, w_up_ref[:, cols])
        x1, x2 = _shifted_rows(carry_ref.at[:, cols], pre)
        cw = conv_w_ref[:, cols]
        return x2 * cw[0:1, :] + x1 * cw[1:2, :] + pre * cw[2:3, :] + b_ref[:, cols]

    gate = conv_half(0)
    half_gate = 0.5 * gate
    act = half_gate + half_gate * jnp.tanh(gate * (GELU_C + (GELU_C * 0.044715) * (gate * gate)))
    f = _dot((act * conv_half(D_FF)).astype(BF16), w_down_ref[...])
    h = h + _rms(f, g_post_ref[...])

    e = _dot(p_ref[...].astype(BF16), w_proj_ref[...])
    g = jax.nn.sigmoid(_dot(h.astype(BF16), w_gate_ref[...]))
    out_ref[...] = h + _rms(e * g, g_ple_ref[...])


def _row_spec(ts, width):
    return pl.BlockSpec((None, ts, width), lambda b, s: (b, s, 0))


def _params(sem):
    return pltpu.CompilerParams(dimension_semantics=sem, vmem_limit_bytes=VMEM_LIMIT)


def _layer(x, p, positions, g_mix_pre, w_in, conv_a_w, w_a_out, g_q_lat, w_q_up, g_kv_lat,
           w_kv_up, w_b_out, w_o, g_mix_post, g_ffn_pre, w_ffn_up, conv_ffn_w, b_ffn_conv,
           w_ffn_down, g_ffn_post, w_ple_proj, w_ple_gate, g_ple_post):
    bsz, seq, _ = x.shape
    row = lambda v: v.reshape(1, -1).astype(F32)

    c0 = 3 * CONV_DIM
    c1 = c0 + Q_LORA
    c2 = c1 + KV_LORA
    c3 = c2 + ROPE_DIM
    c4 = c3 + D_MODEL
    half = ROPE_DIM // 2
    w_kr = jnp.pad(w_in[:, c2:c3], ((0, 0), (NOPE_DIM, HEAD_SLOT - QK_DIM)))
    w_abc = w_in[:, 0:c0].astype(BF16)
    w_lat = jnp.concatenate([w_in[:, c0:c2], w_kr], axis=1).astype(BF16)
    w_ga = w_in[:, c3:c4].astype(BF16)
    w_gb = w_in[:, c4:].astype(BF16)

    wq = w_q_up.reshape(Q_LORA, N_HEADS, QK_DIM)
    w_q = jnp.pad(wq, ((0, 0), (0, 0), (0, HEAD_SLOT - QK_DIM))).reshape(Q_LORA, HP).astype(BF16)

    wkv = w_kv_up.reshape(KV_LORA, N_HEADS, NOPE_DIM + V_DIM)
    w_k = jnp.pad(wkv[:, :, :NOPE_DIM], ((0, 0), (0, 0), (0, HEAD_SLOT - NOPE_DIM))).reshape(KV_LORA, HP).astype(BF16)
    w_vt = jnp.pad(wkv[:, :, NOPE_DIM:], ((0, 0), (0, 0), (0, HEAD_SLOT - V_DIM))).reshape(KV_LORA, HP).T.astype(BF16)
    vone = jnp.tile((jnp.arange(HEAD_SLOT) == V_DIM).astype(F32), N_HEADS).reshape(HP, 1)
    w_b = jnp.pad(w_b_out.reshape(N_HEADS, V_DIM, D_MODEL),
                  ((0, 0), (0, HEAD_SLOT - V_DIM), (0, 0))).reshape(HP, D_MODEL).astype(BF16)

    inv = 1.0 / (ROPE_THETA ** (jnp.arange(half, dtype=F32) * (2.0 / ROPE_DIM)))
    invf = jnp.concatenate([inv, inv]).reshape(ROPE_DIM, 1)
    pos3 = positions.reshape(bsz, 1, seq)
    q_scale = (QK_DIM ** -0.5) * math.log2(math.e)

    act = jax.ShapeDtypeStruct((bsz, seq, HP), BF16)
    steps_per_row = seq // TS_MIX
    later = (w_o, w_ffn_up, w_ffn_down, w_ple_proj, w_ple_gate)
    assert len(later) == N_LATER_WEIGHTS

    def later_spec(w):
        rows, cols = w.shape
        for col_blocks in (1, 2, 4, 8):
            row_blocks = bsz * steps_per_row // col_blocks
            if rows % (row_blocks * BF16_SUBLANES) == 0 and cols % (col_blocks * LANES) == 0:
                break
        else:
            raise ValueError(f"no tile-aligned split of {w.shape} over the mix_in grid")
        return pl.BlockSpec(
            (rows // row_blocks, cols // col_blocks),
            lambda b, s: ((b * steps_per_row + s) // col_blocks, (b * steps_per_row + s) % col_blocks))

    later_specs = [later_spec(w) for w in later]
    q, k, vt, gya, sgb, w_o16, w_up16, w_down16, w_proj16, w_gate16 = pl.pallas_call(
        functools.partial(_mix_in_kernel, q_scale=q_scale),
        grid=(bsz, seq // TS_MIX),
        in_specs=[_row_spec(TS_MIX, D_MODEL),
                  pl.BlockSpec((None, 1, TS_MIX), lambda b, s: (b, 0, s)),
                  _const_spec((ROPE_DIM, 1)), _const_spec((1, D_MODEL)),
                  _const_spec((D_MODEL, c0)), _const_spec((D_MODEL, LAT_COLS)),
                  _const_spec((D_MODEL, D_MODEL)), _const_spec((D_MODEL, D_MODEL)),
                  _const_spec((3, CONV_DIM)), _const_spec((CONV_DIM, D_MODEL)),
                  _const_spec((1, Q_LORA)), _const_spec((Q_LORA, HP)),
                  _const_spec((1, KV_LORA)), _const_spec((KV_LORA, HP)),
                  _const_spec((HP, KV_LORA)), _const_spec((HP, 1))] + later_specs,
        out_specs=[_row_spec(TS_MIX, HP), _row_spec(TS_MIX, HP),
                   pl.BlockSpec((None, HP, TS_MIX), lambda b, s: (b, 0, s)),
                   _row_spec(TS_MIX, HP), _row_spec(TS_MIX, HP)] + later_specs,
        out_shape=[act, act, jax.ShapeDtypeStruct((bsz, HP, seq), BF16), act, act]
        + [jax.ShapeDtypeStruct(w.shape, BF16) for w in later],
        scratch_shapes=[pltpu.VMEM((SUBLANES, CONV_DIM), F32)],
        compiler_params=_params(("arbitrary", "arbitrary")),
        name="mix_in",
    )(x, pos3, invf, row(g_mix_pre), w_abc, w_lat, w_ga, w_gb, conv_a_w.astype(F32),
      w_a_out.astype(BF16), row(g_q_lat), w_q, row(g_kv_lat), w_k, w_vt, vone, *later)

    group = HEADS_PER_STEP * HEAD_SLOT
    head_cols = pl.BlockSpec((None, seq, group), lambda b, h: (b, 0, h))
    o = pl.pallas_call(
        _attn_kernel,
        grid=(bsz, N_HEADS // HEADS_PER_STEP),
        in_specs=[head_cols, head_cols,
                  pl.BlockSpec((None, group, seq), lambda b, h: (b, h, 0))],
        out_specs=head_cols,
        out_shape=act,
        scratch_shapes=[pltpu.VMEM((2, HEADS_PER_STEP, TKEY, TQ), F32),
                        pltpu.VMEM((2, HEADS_PER_STEP, 1, TQ), F32),
                        pltpu.VMEM((HEADS_PER_STEP, 1, TQ), F32),
                        pltpu.VMEM((HEADS_PER_STEP, HEAD_SLOT, TQ), F32),
                        pltpu.VMEM((2, HEADS_PER_STEP, HEAD_SLOT, TQ), BF16)],
        compiler_params=_params(("arbitrary", "arbitrary")),
        name="attn",
    )(q, k, vt)

    h3 = pl.pallas_call(
        _post_kernel,
        grid=(bsz, seq // TS_POST),
        in_specs=[_row_spec(TS_POST, D_MODEL), _row_spec(TS_POST, HP), _row_spec(TS_POST, HP),
                  _row_spec(TS_POST, HP), _row_spec(TS_POST, PLE_DIM),
                  _const_spec((HP, D_MODEL)), _const_spec((D_MODEL, D_MODEL)),
                  _const_spec((1, D_MODEL)), _const_spec((1, D_MODEL)),
                  _const_spec((D_MODEL, 2 * D_FF)), _const_spec((3, 2 * D_FF)),
                  _const_spec((1, 2 * D_FF)), _const_spec((D_FF, D_MODEL)),
                  _const_spec((1, D_MODEL)), _const_spec((PLE_DIM, D_MODEL)),
                  _const_spec((D_MODEL, D_MODEL)), _const_spec((1, D_MODEL))],
        out_specs=_row_spec(TS_POST, D_MODEL),
        out_shape=jax.ShapeDtypeStruct(x.shape, F32),
        scratch_shapes=[pltpu.VMEM((SUBLANES, 2 * D_FF), F32)],
        compiler_params=_params(("arbitrary", "arbitrary")),
        name="post",
    )(x, o, gya, sgb, p, w_b, w_o16, row(g_mix_post), row(g_ffn_pre),
      w_up16, conv_ffn_w.astype(F32), row(b_ffn_conv), w_down16,
      row(g_ffn_post), w_proj16, w_gate16, row(g_ple_post))
    return h3


def kernel(x, p, positions, g_mix_pre, w_in, conv_a_w, w_a_out, g_q_lat, w_q_up, g_kv_lat, w_kv_up, w_b_out, w_o, g_mix_post, g_ffn_pre, w_ffn_up, conv_ffn_w, b_ffn_conv, w_ffn_down, g_ffn_post, w_ple_proj, w_ple_gate, g_ple_post):
    h = x
    for i in range(g_mix_pre.shape[0]):
        h = _layer(h, p[i], positions, g_mix_pre[i], w_in[i], conv_a_w[i], w_a_out[i], g_q_lat[i],
                   w_q_up[i], g_kv_lat[i], w_kv_up[i], w_b_out[i], w_o[i], g_mix_post[i],
                   g_ffn_pre[i], w_ffn_up[i], conv_ffn_w[i], b_ffn_conv[i], w_ffn_down[i],
                   g_ffn_post[i], w_ple_proj[i], w_ple_gate[i], g_ple_post[i])
    return h
```

```python
import functools
import math

import jax
import jax.numpy as jnp
from jax.experimental import pallas as pl
from jax.experimental.pallas import tpu as pltpu

D_MODEL = 1024
CHUNK = 64
CONV_DIM = 512
N_HEADS = 8
NOPE_DIM = 64
ROPE_DIM = 32
V_DIM = 64
Q_LORA = 384
KV_LORA = 256
ROPE_THETA = 10000.0
D_FF = 2816
PLE_DIM = 256
EPS = 1e-6
QK_DIM = NOPE_DIM + ROPE_DIM

LANES = 128
SUBLANES = 8
BF16_SUBLANES = 16
N_LATER_WEIGHTS = 6
HEAD_SLOT = LANES
HP = N_HEADS * HEAD_SLOT
LAT_COLS = Q_LORA + KV_LORA + LANES
VMEM_LIMIT = 56 * 1024 * 1024

TS_MIX = 512
TQ = 1024
TKEY = TQ // 2
HEADS_PER_STEP = 2
assert HEADS_PER_STEP * V_DIM == LANES
V_ROWS = 80
TS_POST = 512

GELU_C = math.sqrt(2.0 / math.pi)
NEG_BIG = -1e30
BF16 = jnp.bfloat16
F32 = jnp.float32


def _rms(x, g):
    return x * jax.lax.rsqrt(jnp.mean(x * x, axis=-1, keepdims=True) + EPS) * g


def _dot(a, b):
    return jnp.dot(a, b, preferred_element_type=F32)


def _const_spec(shape):
    return pl.BlockSpec(shape, lambda *_: (0,) * len(shape))


def _shifted_rows(carry_ref, pre):
    rows = pre.shape[0]
    ext = jnp.concatenate([carry_ref[...], pre], axis=0)
    carry_ref[...] = pre[rows - SUBLANES:rows, :]
    return pltpu.roll(ext, 1, 0)[SUBLANES:], pltpu.roll(ext, 2, 0)[SUBLANES:]


def _mix_in_kernel(x_ref, pos_ref, invf_ref, g_pre_ref, w_abc_ref, w_lat_ref,
                   w_ga_ref, w_gb_ref, conv_w_ref, w_a_out_ref, g_q_ref, w_q_ref,
                   g_kv_ref, w_k_ref, w_v_ref, vone_ref, *rest, q_scale):
    later_f32 = rest[:N_LATER_WEIGHTS]
    q_ref, k_ref, v_ref, gya_ref, sgb_ref = rest[N_LATER_WEIGHTS:N_LATER_WEIGHTS + 5]
    later_bf16 = rest[N_LATER_WEIGHTS + 5:-1]
    carry_ref = rest[-1]
    ts = x_ref.shape[0]

    for src, dst in zip(later_f32, later_bf16):
        dst[...] = src[...].astype(BF16)

    @pl.when(pl.program_id(1) == 0)
    def _():
        carry_ref[...] = jnp.zeros_like(carry_ref)

    u = _rms(x_ref[...], g_pre_ref[...]).astype(BF16)

    lat = _dot(u, w_lat_ref[...])
    q_lat = lat[:, 0:Q_LORA]
    kv_lat = lat[:, Q_LORA:Q_LORA + KV_LORA]
    kr = lat[:, Q_LORA + KV_LORA:LAT_COLS]

    half = ROPE_DIM // 2
    ang_t = invf_ref[...] * pos_ref[...].astype(F32)
    sin_t = jnp.sin(ang_t)
    rows = lambda n: jnp.zeros((n, ts), F32)
    tail = HEAD_SLOT - QK_DIM
    cos = jnp.concatenate([jnp.ones((NOPE_DIM, ts), F32), jnp.cos(ang_t), rows(tail)], axis=0).T
    sin_dn = jnp.concatenate([rows(NOPE_DIM), -sin_t[0:half], rows(half + tail)], axis=0).T
    sin_up = jnp.concatenate([rows(NOPE_DIM + half), sin_t[half:], rows(tail)], axis=0).T

    def rope(t, c, s_dn, s_up):
        return (t * c + pltpu.roll(t, HEAD_SLOT - half, 1) * s_dn + pltpu.roll(t, half, 1) * s_up)

    qn = _rms(q_lat, g_q_ref[...]).astype(BF16)
    qq = _dot(qn, w_q_ref[...])
    kvn = _rms(kv_lat, g_kv_ref[...]).astype(BF16)
    kk = _dot(kvn, w_k_ref[...])
    vt = jax.lax.dot_general(w_v_ref[...], kvn, (((1,), (1,)), ((), ())),
                             preferred_element_type=F32) + vone_ref[...]
    v_ref[...] = vt.astype(BF16)

    k_pe = rope(kr, cos, sin_dn, sin_up)
    q_tables = (cos * q_scale, sin_dn * q_scale, sin_up * q_scale)
    for h in range(N_HEADS):
        sl = slice(h * HEAD_SLOT, (h + 1) * HEAD_SLOT)
        q_ref[:, sl] = rope(qq[:, sl], *q_tables).astype(BF16)
        k_ref[:, sl] = (kk[:, sl] + k_pe).astype(BF16)

    abc = _dot(u, w_abc_ref[...])
    a_b = abc[:, 0:CONV_DIM]
    ca = abc[:, CONV_DIM:2 * CONV_DIM] * abc[:, 2 * CONV_DIM:3 * CONV_DIM]
    x1, x2 = _shifted_rows(carry_ref, ca)
    cw = conv_w_ref[...]
    cv = x2 * cw[0:1, :] + x1 * cw[1:2, :] + ca * cw[2:3, :]
    ya = _dot((a_b * cv).astype(BF16), w_a_out_ref[...])
    gya_ref[...] = (jax.nn.sigmoid(_dot(u, w_ga_ref[...])) * ya).astype(BF16)
    sgb_ref[...] = jax.nn.sigmoid(_dot(u, w_gb_ref[...])).astype(BF16)


def _attn_kernel(q_ref, k_ref, vt_ref, o_ref, s_ref, smax_ref, m_ref, acc_ref, qt_ref):
    tk, tq = s_ref.shape[-2:]
    assert tq == 2 * tk
    nq = q_ref.shape[0] // tq
    heads = range(HEADS_PER_STEP)
    lanes = lambda h: slice(h * HEAD_SLOT, (h + 1) * HEAD_SLOT)
    v_rows = lambda h: slice(h * HEAD_SLOT, h * HEAD_SLOT + V_ROWS)
    key_tile = lambda j: pl.ds(pl.multiple_of(j * tk, tk), tk)
    every = slice(0, tq)
    later = slice(tk, tq)

    def transpose_queries(qi):
        for h in heads:
            qt_ref[qi % 2, h] = q_ref[pl.ds(pl.multiple_of(qi * tq, tq), tq), lanes(h)].T

    def scores(qi, j, buf, hs=heads):
        for h in hs:
            sc = _dot(k_ref[key_tile(j), lanes(h)], qt_ref[qi % 2, h])
            s_ref[buf, h] = sc
            smax_ref[buf, h] = jnp.max(sc, axis=0, keepdims=True)

    def scores_later(qi, buf, hs=heads):
        for h in hs:
            s_ref[buf, h, :, later] = _dot(k_ref[key_tile(2 * qi + 1), lanes(h)],
                                           qt_ref[qi % 2, h, :, later])

    def accumulate(j, buf, cols, masked, hs=heads):
        width = cols.stop - cols.start
        if masked:
            ck = jax.lax.broadcasted_iota(jnp.int32, (tk, width), 0) // CHUNK
            cq = jax.lax.broadcasted_iota(jnp.int32, (tk, width), 1) // CHUNK
            load = lambda h: jnp.where(ck <= cq, s_ref[buf, h, :, cols], NEG_BIG)
            tile_max = lambda h: jnp.max(load(h), axis=0, keepdims=True)
        else:
            load = lambda h: s_ref[buf, h, :, cols]
            tile_max = lambda h: smax_ref[buf, h, :, cols]
        for h in hs:
            m = m_ref[h, :, cols]
            m_new = jnp.maximum(m, tile_max(h))
            p = jnp.exp2(load(h) - m_new)
            pv = _dot(vt_ref[v_rows(h), key_tile(j)], p.astype(BF16))
            acc_ref[h, 0:V_ROWS, cols] = acc_ref[h, 0:V_ROWS, cols] * jnp.exp2(m - m_new) + pv
            m_ref[h, :, cols] = m_new

    def query_tile(qi, _):
        next_qi = jnp.minimum(qi + 1, nq - 1)
        m_ref[...] = jnp.full_like(m_ref, NEG_BIG)
        acc_ref[...] = jnp.zeros_like(acc_ref)

        @pl.when(qi + 1 < nq)
        def _():
            transpose_queries(qi + 1)

        def tile_pair(i):
            for h in heads:
                scores(qi, 2 * i + 1, 1, (h,))
                accumulate(2 * i, 0, every, False, (h,))
            for h in heads:
                scores(qi, 2 * i + 2, 0, (h,))
                accumulate(2 * i + 1, 1, every, False, (h,))

        def two_pairs(i, _):
            tile_pair(2 * i)
            tile_pair(2 * i + 1)
            return 0

        jax.lax.fori_loop(0, qi // 2, two_pairs, 0)

        @pl.when(qi % 2 == 1)
        def _():
            tile_pair(qi - 1)

        for h in heads:
            scores_later(qi, 1, (h,))
            accumulate(2 * qi, 0, every, True, (h,))
        for h in heads:
            scores(next_qi, 0, 0, (h,))
            accumulate(2 * qi + 1, 1, later, True, (h,))

        out = jnp.concatenate([acc_ref[h, 0:V_DIM, :] / acc_ref[h, V_DIM:V_DIM + 1, :]
                               for h in heads], axis=0)
        o_ref[pl.ds(pl.multiple_of(qi * tq, tq), tq), :] = out.T.astype(BF16)
        return 0

    transpose_queries(0)
    scores(0, 0, 0)
    jax.lax.fori_loop(0, nq, query_tile, 0)


def _post_kernel(x_ref, o_ref, gya_ref, sgb_ref, p_ref, w_b_ref, w_o_ref, g_mix_ref,
                 g_pre_ref, w_up_ref, conv_w_ref, b_ref, w_down_ref, g_post_ref,
                 w_proj_ref, w_gate_ref, g_ple_ref, out_ref, carry_ref):
    @pl.when(pl.program_id(1) == 0)
    def _():
        carry_ref[...] = jnp.zeros_like(carry_ref)

    yb = _dot(o_ref[...], w_b_ref[...])
    mixed = gya_ref[...].astype(F32) + sgb_ref[...].astype(F32) * yb
    h = x_ref[...] + _rms(_dot(mixed.astype(BF16), w_o_ref[...]), g_mix_ref[...])

    u = _rms(h, g_pre_ref[...]).astype(BF16)

    def conv_half(col):
        cols = slice(col, col + D_FF)
        pre = _dot(u, w_up_ref[:, cols])
        x1, x2 = _shifted_rows(carry_ref.at[:, cols], pre)
        cw = conv_w_ref[:, cols]
        return x2 * cw[0:1, :] + x1 * cw[1:2, :] + pre * cw[2:3, :] + b_ref[:, cols]

    gate = conv_half(0)
    half_gate = 0.5 * gate
    act = half_gate + half_gate * jnp.tanh(gate * (GELU_C + (GELU_C * 0.044715) * (gate * gate)))
    f = _dot((act * conv_half(D_FF)).astype(BF16), w_down_ref[...])
    h = h + _rms(f, g_post_ref[...])

    e = _dot(p_ref[...].astype(BF16), w_proj_ref[...])
    g = jax.nn.sigmoid(_dot(h.astype(BF16), w_gate_ref[...]))
    out_ref[...] = h + _rms(e * g, g_ple_ref[...])


def _row_spec(ts, width):
    return pl.BlockSpec((None, ts, width), lambda b, s: (b, s, 0))


def _params(sem):
    return pltpu.CompilerParams(dimension_semantics=sem, vmem_limit_bytes=VMEM_LIMIT)


def _layer(x, p, positions, g_mix_pre, w_in, conv_a_w, w_a_out, g_q_lat, w_q_up, g_kv_lat,
           w_kv_up, w_b_out, w_o, g_mix_post, g_ffn_pre, w_ffn_up, conv_ffn_w, b_ffn_conv,
           w_ffn_down, g_ffn_post, w_ple_proj, w_ple_gate, g_ple_post):
    bsz, seq, _ = x.shape
    row = lambda v: v.reshape(1, -1).astype(F32)

    c0 = 3 * CONV_DIM
    c1 = c0 + Q_LORA
    c2 = c1 + KV_LORA
    c3 = c2 + ROPE_DIM
    c4 = c3 + D_MODEL
    half = ROPE_DIM // 2
    w_kr = jnp.pad(w_in[:, c2:c3], ((0, 0), (NOPE_DIM, HEAD_SLOT - QK_DIM)))
    w_abc = w_in[:, 0:c0].astype(BF16)
    w_lat = jnp.concatenate([w_in[:, c0:c2], w_kr], axis=1).astype(BF16)
    w_ga = w_in[:, c3:c4].astype(BF16)
    w_gb = w_in[:, c4:].astype(BF16)

    wq = w_q_up.reshape(Q_LORA, N_HEADS, QK_DIM)
    w_q = jnp.pad(wq, ((0, 0), (0, 0), (0, HEAD_SLOT - QK_DIM))).reshape(Q_LORA, HP).astype(BF16)

    wkv = w_kv_up.reshape(KV_LORA, N_HEADS, NOPE_DIM + V_DIM)
    w_k = jnp.pad(wkv[:, :, :NOPE_DIM], ((0, 0), (0, 0), (0, HEAD_SLOT - NOPE_DIM))).reshape(KV_LORA, HP).astype(BF16)
    w_vt = jnp.pad(wkv[:, :, NOPE_DIM:], ((0, 0), (0, 0), (0, HEAD_SLOT - V_DIM))).reshape(KV_LORA, HP).T.astype(BF16)
    vone = jnp.tile((jnp.arange(HEAD_SLOT) == V_DIM).astype(F32), N_HEADS).reshape(HP, 1)

    inv = 1.0 / (ROPE_THETA ** (jnp.arange(half, dtype=F32) * (2.0 / ROPE_DIM)))
    invf = jnp.concatenate([inv, inv]).reshape(ROPE_DIM, 1)
    pos3 = positions.reshape(bsz, 1, seq)
    q_scale = (QK_DIM ** -0.5) * math.log2(math.e)

    act = jax.ShapeDtypeStruct((bsz, seq, HP), BF16)
    steps_per_row = seq // TS_MIX
    later = (w_b_out, w_o, w_ffn_up, w_ffn_down, w_ple_proj, w_ple_gate)
    assert len(later) == N_LATER_WEIGHTS

    def later_spec(w):
        rows, cols = w.shape
        for col_blocks in (1, 2, 4, 8):
            row_blocks = bsz * steps_per_row // col_blocks
            if rows % (row_blocks * BF16_SUBLANES) == 0 and cols % (col_blocks * LANES) == 0:
                break
        else:
            raise ValueError(f"no tile-aligned split of {w.shape} over the mix_in grid")
        return pl.BlockSpec(
            (rows // row_blocks, cols // col_blocks),
            lambda b, s: ((b * steps_per_row + s) // col_blocks, (b * steps_per_row + s) % col_blocks))

    later_specs = [later_spec(w) for w in later]
    q, k, vt, gya, sgb, w_b16, w_o16, w_up16, w_down16, w_proj16, w_gate16 = pl.pallas_call(
        functools.partial(_mix_in_kernel, q_scale=q_scale),
        grid=(bsz, seq // TS_MIX),
        in_specs=[_row_spec(TS_MIX, D_MODEL),
                  pl.BlockSpec((None, 1, TS_MIX), lambda b, s: (b, 0, s)),
                  _const_spec((ROPE_DIM, 1)), _const_spec((1, D_MODEL)),
                  _const_spec((D_MODEL, c0)), _const_spec((D_MODEL, LAT_COLS)),
                  _const_spec((D_MODEL, D_MODEL)), _const_spec((D_MODEL, D_MODEL)),
                  _const_spec((3, CONV_DIM)), _const_spec((CONV_DIM, D_MODEL)),
                  _const_spec((1, Q_LORA)), _const_spec((Q_LORA, HP)),
                  _const_spec((1, KV_LORA)), _const_spec((KV_LORA, HP)),
                  _const_spec((HP, KV_LORA)), _const_spec((HP, 1))] + later_specs,
        out_specs=[_row_spec(TS_MIX, HP), _row_spec(TS_MIX, HP),
                   pl.BlockSpec((None, HP, TS_MIX), lambda b, s: (b, 0, s)),
                   _row_spec(TS_MIX, HP), _row_spec(TS_MIX, HP)] + later_specs,
        out_shape=[act, act, jax.ShapeDtypeStruct((bsz, HP, seq), BF16), act, act]
        + [jax.ShapeDtypeStruct(w.shape, BF16) for w in later],
        scratch_shapes=[pltpu.VMEM((SUBLANES, CONV_DIM), F32)],
        compiler_params=_params(("arbitrary", "arbitrary")),
        name="mix_in",
    )(x, pos3, invf, row(g_mix_pre), w_abc, w_lat, w_ga, w_gb, conv_a_w.astype(F32),
      w_a_out.astype(BF16), row(g_q_lat), w_q, row(g_kv_lat), w_k, w_vt, vone, *later)

    group = HEADS_PER_STEP * HEAD_SLOT
    head_cols = pl.BlockSpec((None, seq, group), lambda b, h: (b, 0, h))
    o = pl.pallas_call(
        _attn_kernel,
        grid=(bsz, N_HEADS // HEADS_PER_STEP),
        in_specs=[head_cols, head_cols,
                  pl.BlockSpec((None, group, seq), lambda b, h: (b, h, 0))],
        out_specs=pl.BlockSpec((None, seq, HEADS_PER_STEP * V_DIM), lambda b, h: (b, 0, h)),
        out_shape=jax.ShapeDtypeStruct((bsz, seq, N_HEADS * V_DIM), BF16),
        scratch_shapes=[pltpu.VMEM((2, HEADS_PER_STEP, TKEY, TQ), F32),
                        pltpu.VMEM((2, HEADS_PER_STEP, 1, TQ), F32),
                        pltpu.VMEM((HEADS_PER_STEP, 1, TQ), F32),
                        pltpu.VMEM((HEADS_PER_STEP, HEAD_SLOT, TQ), F32),
                        pltpu.VMEM((2, HEADS_PER_STEP, HEAD_SLOT, TQ), BF16)],
        compiler_params=_params(("arbitrary", "arbitrary")),
        name="attn",
    )(q, k, vt)

    h3 = pl.pallas_call(
        _post_kernel,
        grid=(bsz, seq // TS_POST),
        in_specs=[_row_spec(TS_POST, D_MODEL), _row_spec(TS_POST, N_HEADS * V_DIM),
                  _row_spec(TS_POST, HP), _row_spec(TS_POST, HP), _row_spec(TS_POST, PLE_DIM),
                  _const_spec((N_HEADS * V_DIM, D_MODEL)), _const_spec((D_MODEL, D_MODEL)),
                  _const_spec((1, D_MODEL)), _const_spec((1, D_MODEL)),
                  _const_spec((D_MODEL, 2 * D_FF)), _const_spec((3, 2 * D_FF)),
                  _const_spec((1, 2 * D_FF)), _const_spec((D_FF, D_MODEL)),
                  _const_spec((1, D_MODEL)), _const_spec((PLE_DIM, D_MODEL)),
                  _const_spec((D_MODEL, D_MODEL)), _const_spec((1, D_MODEL))],
        out_specs=_row_spec(TS_POST, D_MODEL),
        out_shape=jax.ShapeDtypeStruct(x.shape, F32),
        scratch_shapes=[pltpu.VMEM((SUBLANES, 2 * D_FF), F32)],
        compiler_params=_params(("arbitrary", "arbitrary")),
        name="post",
    )(x, o, gya, sgb, p, w_b16, w_o16, row(g_mix_post), row(g_ffn_pre),
      w_up16, conv_ffn_w.astype(F32), row(b_ffn_conv), w_down16,
      row(g_ffn_post), w_proj16, w_gate16, row(g_ple_post))
    return h3


def kernel(x, p, positions, g_mix_pre, w_in, conv_a_w, w_a_out, g_q_lat, w_q_up, g_kv_lat, w_kv_up, w_b_out, w_o, g_mix_post, g_ffn_pre, w_ffn_up, conv_ffn_w, b_ffn_conv, w_ffn_down, g_ffn_post, w_ple_proj, w_ple_gate, g_ple_post):
    h = x
    for i in range(g_mix_pre.shape[0]):
        h = _layer(h, p[i], positions, g_mix_pre[i], w_in[i], conv_a_w[i], w_a_out[i], g_q_lat[i],
                   w_q_up[i], g_kv_lat[i], w_kv_up[i], w_b_out[i], w_o[i], g_mix_post[i],
                   g_ffn_pre[i], w_ffn_up[i], conv_ffn_w[i], b_ffn_conv[i], w_ffn_down[i],
                   g_ffn_post[i], w_ple_proj[i], w_ple_gate[i], g_ple_post[i])
    return h
```

```python
import functools
import math

import jax
import jax.numpy as jnp
from jax.experimental import pallas as pl
from jax.experimental.pallas import tpu as pltpu

D_MODEL = 1024
CHUNK = 64
CONV_DIM = 512
N_HEADS = 8
NOPE_DIM = 64
ROPE_DIM = 32
V_DIM = 64
Q_LORA = 384
KV_LORA = 256
ROPE_THETA = 10000.0
D_FF = 2816
PLE_DIM = 256
EPS = 1e-6
QK_DIM = NOPE_DIM + ROPE_DIM

LANES = 128
SUBLANES = 8
BF16_SUBLANES = 16
N_LATER_WEIGHTS = 6
HEAD_SLOT = LANES
HP = N_HEADS * HEAD_SLOT
LAT_COLS = Q_LORA + KV_LORA + LANES
VMEM_LIMIT = 56 * 1024 * 1024

TS_MIX = 512
TQ = 1024
TKEY = TQ // 2
HEADS_PER_STEP = 2
assert HEADS_PER_STEP * V_DIM == LANES
V_ROWS = pl.cdiv(V_DIM + 1, BF16_SUBLANES) * BF16_SUBLANES
TS_POST = 512

GELU_C = math.sqrt(2.0 / math.pi)
NEG_BIG = -1e30
BF16 = jnp.bfloat16
F32 = jnp.float32


def _rms(x, g):
    return x * jax.lax.rsqrt(jnp.mean(x * x, axis=-1, keepdims=True) + EPS) * g


def _dot(a, b):
    return jnp.dot(a, b, preferred_element_type=F32)


def _const_spec(shape):
    return pl.BlockSpec(shape, lambda *_: (0,) * len(shape))


def _shifted_rows(carry_ref, pre):
    rows = pre.shape[0]
    ext = jnp.concatenate([carry_ref[...], pre], axis=0)
    carry_ref[...] = pre[rows - SUBLANES:rows, :]
    return pltpu.roll(ext, 1, 0)[SUBLANES:], pltpu.roll(ext, 2, 0)[SUBLANES:]


def _mix_in_kernel(x_ref, pos_ref, invf_ref, g_pre_ref, w_abc_ref, w_lat_ref,
                   w_ga_ref, w_gb_ref, conv_w_ref, w_a_out_ref, g_q_ref, w_q_ref,
                   g_kv_ref, w_k_ref, w_v_ref, vone_ref, *rest, q_scale):
    later_f32 = rest[:N_LATER_WEIGHTS]
    q_ref, k_ref, v_ref, gya_ref, sgb_ref = rest[N_LATER_WEIGHTS:N_LATER_WEIGHTS + 5]
    later_bf16 = rest[N_LATER_WEIGHTS + 5:-1]
    carry_ref = rest[-1]
    ts = x_ref.shape[0]

    for src, dst in zip(later_f32, later_bf16):
        dst[...] = src[...].astype(BF16)

    @pl.when(pl.program_id(1) == 0)
    def _():
        carry_ref[...] = jnp.zeros_like(carry_ref)

    u = _rms(x_ref[...], g_pre_ref[...]).astype(BF16)

    lat = _dot(u, w_lat_ref[...])
    q_lat = lat[:, 0:Q_LORA]
    kv_lat = lat[:, Q_LORA:Q_LORA + KV_LORA]
    kr = lat[:, Q_LORA + KV_LORA:LAT_COLS]

    half = ROPE_DIM // 2
    ang_t = invf_ref[...] * pos_ref[...].astype(F32)
    sin_t = jnp.sin(ang_t)
    rows = lambda n: jnp.zeros((n, ts), F32)
    tail = HEAD_SLOT - QK_DIM
    cos = jnp.concatenate([jnp.ones((NOPE_DIM, ts), F32), jnp.cos(ang_t), rows(tail)], axis=0).T
    sin_dn = jnp.concatenate([rows(NOPE_DIM), -sin_t[0:half], rows(half + tail)], axis=0).T
    sin_up = jnp.concatenate([rows(NOPE_DIM + half), sin_t[half:], rows(tail)], axis=0).T

    def rope(t, c, s_dn, s_up):
        return (t * c + pltpu.roll(t, HEAD_SLOT - half, 1) * s_dn + pltpu.roll(t, half, 1) * s_up)

    qn = _rms(q_lat, g_q_ref[...]).astype(BF16)
    qq = _dot(qn, w_q_ref[...])
    kvn = _rms(kv_lat, g_kv_ref[...]).astype(BF16)
    kk = _dot(kvn, w_k_ref[...])
    vt = jax.lax.dot_general(w_v_ref[...], kvn, (((1,), (1,)), ((), ())),
                             preferred_element_type=F32) + vone_ref[...]
    v_ref[...] = vt.astype(BF16)

    k_pe = rope(kr, cos, sin_dn, sin_up)
    q_tables = (cos * q_scale, sin_dn * q_scale, sin_up * q_scale)
    for h in range(N_HEADS):
        sl = slice(h * HEAD_SLOT, (h + 1) * HEAD_SLOT)
        q_ref[:, sl] = rope(qq[:, sl], *q_tables).astype(BF16)
        k_ref[:, sl] = (kk[:, sl] + k_pe).astype(BF16)

    abc = _dot(u, w_abc_ref[...])
    a_b = abc[:, 0:CONV_DIM]
    ca = abc[:, CONV_DIM:2 * CONV_DIM] * abc[:, 2 * CONV_DIM:3 * CONV_DIM]
    x1, x2 = _shifted_rows(carry_ref, ca)
    cw = conv_w_ref[...]
    cv = x2 * cw[0:1, :] + x1 * cw[1:2, :] + ca * cw[2:3, :]
    ya = _dot((a_b * cv).astype(BF16), w_a_out_ref[...])
    gya_ref[...] = (jax.nn.sigmoid(_dot(u, w_ga_ref[...])) * ya).astype(BF16)
    sgb_ref[...] = jax.nn.sigmoid(_dot(u, w_gb_ref[...])).astype(BF16)


def _attn_kernel(q_ref, k_ref, vt_ref, o_ref, s_ref, smax_ref, m_ref, acc_ref, qt_ref):
    tk, tq = s_ref.shape[-2:]
    assert tq == 2 * tk
    nq = q_ref.shape[0] // tq
    heads = range(HEADS_PER_STEP)
    lanes = lambda h: slice(h * HEAD_SLOT, (h + 1) * HEAD_SLOT)
    v_rows = lambda h: slice(h * HEAD_SLOT, h * HEAD_SLOT + V_ROWS)
    key_tile = lambda j: pl.ds(pl.multiple_of(j * tk, tk), tk)
    every = slice(0, tq)
    later = slice(tk, tq)

    def transpose_queries(qi):
        for h in heads:
            qt_ref[qi % 2, h] = q_ref[pl.ds(pl.multiple_of(qi * tq, tq), tq), lanes(h)].T

    def scores(qi, j, buf, hs=heads):
        for h in hs:
            sc = _dot(k_ref[key_tile(j), lanes(h)], qt_ref[qi % 2, h])
            s_ref[buf, h] = sc
            smax_ref[buf, h] = jnp.max(sc, axis=0, keepdims=True)

    def scores_later(qi, buf, hs=heads):
        for h in hs:
            s_ref[buf, h, :, later] = _dot(k_ref[key_tile(2 * qi + 1), lanes(h)],
                                           qt_ref[qi % 2, h, :, later])

    def accumulate(j, buf, cols, masked, hs=heads):
        width = cols.stop - cols.start
        if masked:
            ck = jax.lax.broadcasted_iota(jnp.int32, (tk, width), 0) // CHUNK
            cq = jax.lax.broadcasted_iota(jnp.int32, (tk, width), 1) // CHUNK
            load = lambda h: jnp.where(ck <= cq, s_ref[buf, h, :, cols], NEG_BIG)
            tile_max = lambda h: jnp.max(load(h), axis=0, keepdims=True)
        else:
            load = lambda h: s_ref[buf, h, :, cols]
            tile_max = lambda h: smax_ref[buf, h, :, cols]
        for h in hs:
            m = m_ref[h, :, cols]
            m_new = jnp.maximum(m, tile_max(h))
            p = jnp.exp2(load(h) - m_new)
            pv = _dot(vt_ref[v_rows(h), key_tile(j)], p.astype(BF16))
            acc_ref[h, 0:V_ROWS, cols] = acc_ref[h, 0:V_ROWS, cols] * jnp.exp2(m - m_new) + pv
            m_ref[h, :, cols] = m_new

    def query_tile(qi, _):
        next_qi = jnp.minimum(qi + 1, nq - 1)
        m_ref[...] = jnp.full_like(m_ref, NEG_BIG)
        acc_ref[...] = jnp.zeros_like(acc_ref)

        @pl.when(qi + 1 < nq)
        def _():
            transpose_queries(qi + 1)

        def tile_pair(i):
            for h in heads:
                scores(qi, 2 * i + 1, 1, (h,))
                accumulate(2 * i, 0, every, False, (h,))
            for h in heads:
                scores(qi, 2 * i + 2, 0, (h,))
                accumulate(2 * i + 1, 1, every, False, (h,))

        def two_pairs(i, _):
            tile_pair(2 * i)
            tile_pair(2 * i + 1)
            return 0

        jax.lax.fori_loop(0, qi // 2, two_pairs, 0)

        @pl.when(qi % 2 == 1)
        def _():
            tile_pair(qi - 1)

        for h in heads:
            scores_later(qi, 1, (h,))
            accumulate(2 * qi, 0, every, True, (h,))
        for h in heads:
            scores(next_qi, 0, 0, (h,))
            accumulate(2 * qi + 1, 1, later, True, (h,))

        out = jnp.concatenate([acc_ref[h, 0:V_DIM, :] / acc_ref[h, V_DIM:V_DIM + 1, :]
                               for h in heads], axis=0)
        o_ref[pl.ds(pl.multiple_of(qi * tq, tq), tq), :] = out.T.astype(BF16)
        return 0

    transpose_queries(0)
    scores(0, 0, 0)
    jax.lax.fori_loop(0, nq, query_tile, 0)


def _post_kernel(x_ref, o_ref, gya_ref, sgb_ref, p_ref, w_b_ref, w_o_ref, g_mix_ref,
                 g_pre_ref, w_up_ref, conv_w_ref, b_ref, w_down_ref, g_post_ref,
                 w_proj_ref, w_gate_ref, g_ple_ref, out_ref, carry_ref):
    @pl.when(pl.program_id(1) == 0)
    def _():
        carry_ref[...] = jnp.zeros_like(carry_ref)

    yb = _dot(o_ref[...], w_b_ref[...])
    mixed = gya_ref[...].astype(F32) + sgb_ref[...].astype(F32) * yb
    h = x_ref[...] + _rms(_dot(mixed.astype(BF16), w_o_ref[...]), g_mix_ref[...])

    u = _rms(h, g_pre_ref[...]).astype(BF16)

    pre = _dot(u, w_up_ref[...])
    x1, x2 = _shifted_rows(carry_ref, pre)
    cw = conv_w_ref[...]
    up = x2 * cw[0:1, :] + x1 * cw[1:2, :] + pre * cw[2:3, :] + b_ref[...]
    gate = up[:, 0:D_FF]
    half_gate = 0.5 * gate
    act = half_gate + half_gate * jnp.tanh(gate * (GELU_C + (GELU_C * 0.044715) * (gate * gate)))
    f = _dot((act * up[:, D_FF:]).astype(BF16), w_down_ref[...])
    h = h + _rms(f, g_post_ref[...])

    e = _dot(p_ref[...].astype(BF16), w_proj_ref[...])
    g = jax.nn.sigmoid(_dot(h.astype(BF16), w_gate_ref[...]))
    out_ref[...] = h + _rms(e * g, g_ple_ref[...])


def _row_spec(ts, width):
    return pl.BlockSpec((None, ts, width), lambda b, s: (b, s, 0))


def _params(sem):
    return pltpu.CompilerParams(dimension_semantics=sem, vmem_limit_bytes=VMEM_LIMIT)


def _layer(x, p, positions, g_mix_pre, w_in, conv_a_w, w_a_out, g_q_lat, w_q_up, g_kv_lat,
           w_kv_up, w_b_out, w_o, g_mix_post, g_ffn_pre, w_ffn_up, conv_ffn_w, b_ffn_conv,
           w_ffn_down, g_ffn_post, w_ple_proj, w_ple_gate, g_ple_post):
    bsz, seq, _ = x.shape
    row = lambda v: v.reshape(1, -1).astype(F32)

    c0 = 3 * CONV_DIM
    c1 = c0 + Q_LORA
    c2 = c1 + KV_LORA
    c3 = c2 + ROPE_DIM
    c4 = c3 + D_MODEL
    half = ROPE_DIM // 2
    w_kr = jnp.pad(w_in[:, c2:c3], ((0, 0), (NOPE_DIM, HEAD_SLOT - QK_DIM)))
    w_abc = w_in[:, 0:c0].astype(BF16)
    w_lat = jnp.concatenate([w_in[:, c0:c2], w_kr], axis=1).astype(BF16)
    w_ga = w_in[:, c3:c4].astype(BF16)
    w_gb = w_in[:, c4:].astype(BF16)

    wq = w_q_up.reshape(Q_LORA, N_HEADS, QK_DIM)
    w_q = jnp.pad(wq, ((0, 0), (0, 0), (0, HEAD_SLOT - QK_DIM))).reshape(Q_LORA, HP).astype(BF16)

    wkv = w_kv_up.reshape(KV_LORA, N_HEADS, NOPE_DIM + V_DIM)
    w_k = jnp.pad(wkv[:, :, :NOPE_DIM], ((0, 0), (0, 0), (0, HEAD_SLOT - NOPE_DIM))).reshape(KV_LORA, HP).astype(BF16)
    w_vt = jnp.pad(wkv[:, :, NOPE_DIM:], ((0, 0), (0, 0), (0, HEAD_SLOT - V_DIM))).reshape(KV_LORA, HP).T.astype(BF16)
    vone = jnp.tile((jnp.arange(HEAD_SLOT) == V_DIM).astype(F32), N_HEADS).reshape(HP, 1)

    inv = 1.0 / (ROPE_THETA ** (jnp.arange(half, dtype=F32) * (2.0 / ROPE_DIM)))
    invf = jnp.concatenate([inv, inv]).reshape(ROPE_DIM, 1)
    pos3 = positions.reshape(bsz, 1, seq)
    q_scale = (QK_DIM ** -0.5) * math.log2(math.e)

    act = jax.ShapeDtypeStruct((bsz, seq, HP), BF16)
    steps_per_row = seq // TS_MIX
    later = (w_b_out, w_o, w_ffn_up, w_ffn_down, w_ple_proj, w_ple_gate)
    assert len(later) == N_LATER_WEIGHTS

    def later_spec(w):
        rows, cols = w.shape
        for col_blocks in (1, 2, 4, 8):
            row_blocks = bsz * steps_per_row // col_blocks
            if rows % (row_blocks * BF16_SUBLANES) == 0 and cols % (col_blocks * LANES) == 0:
                break
        else:
            raise ValueError(f"no tile-aligned split of {w.shape} over the mix_in grid")
        return pl.BlockSpec(
            (rows // row_blocks, cols // col_blocks),
            lambda b, s: ((b * steps_per_row + s) // col_blocks, (b * steps_per_row + s) % col_blocks))

    later_specs = [later_spec(w) for w in later]
    q, k, vt, gya, sgb, w_b16, w_o16, w_up16, w_down16, w_proj16, w_gate16 = pl.pallas_call(
        functools.partial(_mix_in_kernel, q_scale=q_scale),
        grid=(bsz, seq // TS_MIX),
        in_specs=[_row_spec(TS_MIX, D_MODEL),
                  pl.BlockSpec((None, 1, TS_MIX), lambda b, s: (b, 0, s)),
                  _const_spec((ROPE_DIM, 1)), _const_spec((1, D_MODEL)),
                  _const_spec((D_MODEL, c0)), _const_spec((D_MODEL, LAT_COLS)),
                  _const_spec((D_MODEL, D_MODEL)), _const_spec((D_MODEL, D_MODEL)),
                  _const_spec((3, CONV_DIM)), _const_spec((CONV_DIM, D_MODEL)),
                  _const_spec((1, Q_LORA)), _const_spec((Q_LORA, HP)),
                  _const_spec((1, KV_LORA)), _const_spec((KV_LORA, HP)),
                  _const_spec((HP, KV_LORA)), _const_spec((HP, 1))] + later_specs,
        out_specs=[_row_spec(TS_MIX, HP), _row_spec(TS_MIX, HP),
                   pl.BlockSpec((None, HP, TS_MIX), lambda b, s: (b, 0, s)),
                   _row_spec(TS_MIX, HP), _row_spec(TS_MIX, HP)] + later_specs,
        out_shape=[act, act, jax.ShapeDtypeStruct((bsz, HP, seq), BF16), act, act]
        + [jax.ShapeDtypeStruct(w.shape, BF16) for w in later],
        scratch_shapes=[pltpu.VMEM((SUBLANES, CONV_DIM), F32)],
        compiler_params=_params(("arbitrary", "arbitrary")),
        name="mix_in",
    )(x, pos3, invf, row(g_mix_pre), w_abc, w_lat, w_ga, w_gb, conv_a_w.astype(F32),
      w_a_out.astype(BF16), row(g_q_lat), w_q, row(g_kv_lat), w_k, w_vt, vone, *later)

    group = HEADS_PER_STEP * HEAD_SLOT
    head_cols = pl.BlockSpec((None, seq, group), lambda b, h: (b, 0, h))
    o = pl.pallas_call(
        _attn_kernel,
        grid=(bsz, N_HEADS // HEADS_PER_STEP),
        in_specs=[head_cols, head_cols,
                  pl.BlockSpec((None, group, seq), lambda b, h: (b, h, 0))],
        out_specs=pl.BlockSpec((None, seq, HEADS_PER_STEP * V_DIM), lambda b, h: (b, 0, h)),
        out_shape=jax.ShapeDtypeStruct((bsz, seq, N_HEADS * V_DIM), BF16),
        scratch_shapes=[pltpu.VMEM((2, HEADS_PER_STEP, TKEY, TQ), F32),
                        pltpu.VMEM((2, HEADS_PER_STEP, 1, TQ), F32),
                        pltpu.VMEM((HEADS_PER_STEP, 1, TQ), F32),
                        pltpu.VMEM((HEADS_PER_STEP, HEAD_SLOT, TQ), F32),
                        pltpu.VMEM((2, HEADS_PER_STEP, HEAD_SLOT, TQ), BF16)],
        compiler_params=_params(("arbitrary", "arbitrary")),
        name="attn",
    )(q, k, vt)

    h3 = pl.pallas_call(
        _post_kernel,
        grid=(bsz, seq // TS_POST),
        in_specs=[_row_spec(TS_POST, D_MODEL), _row_spec(TS_POST, N_HEADS * V_DIM),
                  _row_spec(TS_POST, HP), _row_spec(TS_POST, HP), _row_spec(TS_POST, PLE_DIM),
                  _const_spec((N_HEADS * V_DIM, D_MODEL)), _const_spec((D_MODEL, D_MODEL)),
                  _const_spec((1, D_MODEL)), _const_spec((1, D_MODEL)),
                  _const_spec((D_MODEL, 2 * D_FF)), _const_spec((3, 2 * D_FF)),
                  _const_spec((1, 2 * D_FF)), _const_spec((D_FF, D_MODEL)),
                  _const_spec((1, D_MODEL)), _const_spec((PLE_DIM, D_MODEL)),
                  _const_spec((D_MODEL, D_MODEL)), _const_spec((1, D_MODEL))],
        out_specs=_row_spec(TS_POST, D_MODEL),
        out_shape=jax.ShapeDtypeStruct(x.shape, F32),
        scratch_shapes=[pltpu.VMEM((SUBLANES, 2 * D_FF), F32)],
        compiler_params=_params(("arbitrary", "arbitrary")),
        name="post",
    )(x, o, gya, sgb, p, w_b16, w_o16, row(g_mix_post), row(g_ffn_pre),
      w_up16, conv_ffn_w.astype(F32), row(b_ffn_conv), w_down16,
      row(g_ffn_post), w_proj16, w_gate16, row(g_ple_post))
    return h3


def kernel(x, p, positions, g_mix_pre, w_in, conv_a_w, w_a_out, g_q_lat, w_q_up, g_kv_lat, w_kv_up, w_b_out, w_o, g_mix_post, g_ffn_pre, w_ffn_up, conv_ffn_w, b_ffn_conv, w_ffn_down, g_ffn_post, w_ple_proj, w_ple_gate, g_ple_post):
    h = x
    for i in range(g_mix_pre.shape[0]):
        h = _layer(h, p[i], positions, g_mix_pre[i], w_in[i], conv_a_w[i], w_a_out[i], g_q_lat[i],
                   w_q_up[i], g_kv_lat[i], w_kv_up[i], w_b_out[i], w_o[i], g_mix_post[i],
                   g_ffn_pre[i], w_ffn_up[i], conv_ffn_w[i], b_ffn_conv[i], w_ffn_down[i],
                   g_ffn_post[i], w_ple_proj[i], w_ple_gate[i], g_ple_post[i])
    return h
```

```python
import functools
import math

import jax
import jax.numpy as jnp
from jax.experimental import pallas as pl
from jax.experimental.pallas import tpu as pltpu

D_MODEL = 1024
CHUNK = 64
CONV_DIM = 512
N_HEADS = 8
NOPE_DIM = 64
ROPE_DIM = 32
V_DIM = 64
Q_LORA = 384
KV_LORA = 256
ROPE_THETA = 10000.0
D_FF = 2816
PLE_DIM = 256
EPS = 1e-6
QK_DIM = NOPE_DIM + ROPE_DIM

LANES = 128
SUBLANES = 8
BF16_SUBLANES = 16
N_LATER_WEIGHTS = 6
HEAD_SLOT = LANES
HP = N_HEADS * HEAD_SLOT
LAT_COLS = Q_LORA + KV_LORA + LANES
VMEM_LIMIT = 56 * 1024 * 1024

TS_MIX = 512
TQ = 1024
TKEY = TQ // 2
HEADS_PER_STEP = 2
assert HEADS_PER_STEP * V_DIM == LANES
V_ROWS = pl.cdiv(V_DIM + 1, BF16_SUBLANES) * BF16_SUBLANES
TS_POST = 512

GELU_C = math.sqrt(2.0 / math.pi)
NEG_BIG = -1e30
BF16 = jnp.bfloat16
F32 = jnp.float32


def _rms(x, g):
    return x * jax.lax.rsqrt(jnp.mean(x * x, axis=-1, keepdims=True) + EPS) * g


def _dot(a, b):
    return jnp.dot(a, b, preferred_element_type=F32)


def _const_spec(shape):
    return pl.BlockSpec(shape, lambda *_: (0,) * len(shape))


def _shifted_rows(carry_ref, pre):
    rows = pre.shape[0]
    ext = jnp.concatenate([carry_ref[...], pre], axis=0)
    carry_ref[...] = pre[rows - SUBLANES:rows, :]
    return pltpu.roll(ext, 1, 0)[SUBLANES:], pltpu.roll(ext, 2, 0)[SUBLANES:]


def _mix_in_kernel(x_ref, pos_ref, invf_ref, g_pre_ref, w_abc_ref, w_lat_ref,
                   w_ga_ref, w_gb_ref, conv_w_ref, w_a_out_ref, g_q_ref, w_q_ref,
                   g_kv_ref, w_k_ref, w_v_ref, vone_ref, *rest, q_scale):
    later_f32 = rest[:N_LATER_WEIGHTS]
    q_ref, k_ref, v_ref, gya_ref, sgb_ref = rest[N_LATER_WEIGHTS:N_LATER_WEIGHTS + 5]
    later_bf16 = rest[N_LATER_WEIGHTS + 5:-1]
    carry_ref = rest[-1]
    ts = x_ref.shape[0]

    @pl.when(pl.program_id(1) == 0)
    def _():
        carry_ref[...] = jnp.zeros_like(carry_ref)

    u = _rms(x_ref[...], g_pre_ref[...]).astype(BF16)

    lat = _dot(u, w_lat_ref[...])
    abc = _dot(u, w_abc_ref[...])
    ga = _dot(u, w_ga_ref[...])
    gb = _dot(u, w_gb_ref[...])
    q_lat = lat[:, 0:Q_LORA]
    kv_lat = lat[:, Q_LORA:Q_LORA + KV_LORA]
    kr = lat[:, Q_LORA + KV_LORA:LAT_COLS]

    half = ROPE_DIM // 2
    ang_t = invf_ref[...] * pos_ref[...].astype(F32)
    sin_t = jnp.sin(ang_t)
    rows = lambda n: jnp.zeros((n, ts), F32)
    tail = HEAD_SLOT - QK_DIM
    cos = jnp.concatenate([jnp.ones((NOPE_DIM, ts), F32), jnp.cos(ang_t), rows(tail)], axis=0).T
    sin_dn = jnp.concatenate([rows(NOPE_DIM), -sin_t[0:half], rows(half + tail)], axis=0).T
    sin_up = jnp.concatenate([rows(NOPE_DIM + half), sin_t[half:], rows(tail)], axis=0).T

    def rope(t, c, s_dn, s_up):
        return (t * c + pltpu.roll(t, HEAD_SLOT - half, 1) * s_dn + pltpu.roll(t, half, 1) * s_up)

    qn = _rms(q_lat, g_q_ref[...]).astype(BF16)
    qq = _dot(qn, w_q_ref[...])
    kvn = _rms(kv_lat, g_kv_ref[...]).astype(BF16)
    kk = _dot(kvn, w_k_ref[...])
    vt = jax.lax.dot_general(w_v_ref[...], kvn, (((1,), (1,)), ((), ())),
                             preferred_element_type=F32) + vone_ref[...]
    v_ref[...] = vt.astype(BF16)

    k_pe = rope(kr, cos, sin_dn, sin_up)
    q_tables = (cos * q_scale, sin_dn * q_scale, sin_up * q_scale)
    for h in range(N_HEADS):
        sl = slice(h * HEAD_SLOT, (h + 1) * HEAD_SLOT)
        q_ref[:, sl] = rope(qq[:, sl], *q_tables).astype(BF16)
        k_ref[:, sl] = (kk[:, sl] + k_pe).astype(BF16)

    a_b = abc[:, 0:CONV_DIM]
    ca = abc[:, CONV_DIM:2 * CONV_DIM] * abc[:, 2 * CONV_DIM:3 * CONV_DIM]
    x1, x2 = _shifted_rows(carry_ref, ca)
    cw = conv_w_ref[...]
    cv = x2 * cw[0:1, :] + x1 * cw[1:2, :] + ca * cw[2:3, :]
    ya = _dot((a_b * cv).astype(BF16), w_a_out_ref[...])
    gya_ref[...] = (jax.nn.sigmoid(ga) * ya).astype(BF16)
    sgb_ref[...] = jax.nn.sigmoid(gb).astype(BF16)

    for src, dst in zip(later_f32, later_bf16):
        dst[...] = src[...].astype(BF16)


def _attn_kernel(q_ref, k_ref, vt_ref, o_ref, s_ref, smax_ref, m_ref, acc_ref, qt_ref):
    tk, tq = s_ref.shape[-2:]
    assert tq == 2 * tk
    nq = q_ref.shape[0] // tq
    heads = range(HEADS_PER_STEP)
    lanes = lambda h: slice(h * HEAD_SLOT, (h + 1) * HEAD_SLOT)
    v_rows = lambda h: slice(h * HEAD_SLOT, h * HEAD_SLOT + V_ROWS)
    key_tile = lambda j: pl.ds(pl.multiple_of(j * tk, tk), tk)
    every = slice(0, tq)
    later = slice(tk, tq)

    def transpose_queries(qi):
        for h in heads:
            qt_ref[qi % 2, h] = q_ref[pl.ds(pl.multiple_of(qi * tq, tq), tq), lanes(h)].T

    def scores(qi, j, buf, hs=heads):
        for h in hs:
            sc = _dot(k_ref[key_tile(j), lanes(h)], qt_ref[qi % 2, h])
            s_ref[buf, h] = sc
            smax_ref[buf, h] = jnp.max(sc, axis=0, keepdims=True)

    def scores_later(qi, buf, hs=heads):
        for h in hs:
            s_ref[buf, h, :, later] = _dot(k_ref[key_tile(2 * qi + 1), lanes(h)],
                                           qt_ref[qi % 2, h, :, later])

    def accumulate(j, buf, cols, masked, hs=heads):
        width = cols.stop - cols.start
        if masked:
            ck = jax.lax.broadcasted_iota(jnp.int32, (tk, width), 0) // CHUNK
            cq = jax.lax.broadcasted_iota(jnp.int32, (tk, width), 1) // CHUNK
            load = lambda h: jnp.where(ck <= cq, s_ref[buf, h, :, cols], NEG_BIG)
            tile_max = lambda h: jnp.max(load(h), axis=0, keepdims=True)
        else:
            load = lambda h: s_ref[buf, h, :, cols]
            tile_max = lambda h: smax_ref[buf, h, :, cols]
        for h in hs:
            m = m_ref[h, :, cols]
            m_new = jnp.maximum(m, tile_max(h))
            p = jnp.exp2(load(h) - m_new)
            pv = _dot(vt_ref[v_rows(h), key_tile(j)], p.astype(BF16))
            acc_ref[h, 0:V_ROWS, cols] = acc_ref[h, 0:V_ROWS, cols] * jnp.exp2(m - m_new) + pv
            m_ref[h, :, cols] = m_new

    def query_tile(qi, _):
        next_qi = jnp.minimum(qi + 1, nq - 1)
        m_ref[...] = jnp.full_like(m_ref, NEG_BIG)
        acc_ref[...] = jnp.zeros_like(acc_ref)

        @pl.when(qi + 1 < nq)
        def _():
            transpose_queries(qi + 1)

        def tile_pair(i):
            for h in heads:
                scores(qi, 2 * i + 1, 1, (h,))
                accumulate(2 * i, 0, every, False, (h,))
            for h in heads:
                scores(qi, 2 * i + 2, 0, (h,))
                accumulate(2 * i + 1, 1, every, False, (h,))

        def two_pairs(i, _):
            tile_pair(2 * i)
            tile_pair(2 * i + 1)
            return 0

        jax.lax.fori_loop(0, qi // 2, two_pairs, 0)

        @pl.when(qi % 2 == 1)
        def _():
            tile_pair(qi - 1)

        for h in heads:
            scores_later(qi, 1, (h,))
            accumulate(2 * qi, 0, every, True, (h,))
        for h in heads:
            scores(next_qi, 0, 0, (h,))
            accumulate(2 * qi + 1, 1, later, True, (h,))

        out = jnp.concatenate([acc_ref[h, 0:V_DIM, :] / acc_ref[h, V_DIM:V_DIM + 1, :]
                               for h in heads], axis=0)
        o_ref[pl.ds(pl.multiple_of(qi * tq, tq), tq), :] = out.T.astype(BF16)
        return 0

    transpose_queries(0)
    scores(0, 0, 0)
    jax.lax.fori_loop(0, nq, query_tile, 0)


def _post_kernel(x_ref, o_ref, gya_ref, sgb_ref, p_ref, w_b_ref, w_o_ref, g_mix_ref,
                 g_pre_ref, w_up_ref, conv_w_ref, b_ref, w_down_ref, g_post_ref,
                 w_proj_ref, w_gate_ref, g_ple_ref, out_ref, carry_ref):
    @pl.when(pl.program_id(1) == 0)
    def _():
        carry_ref[...] = jnp.zeros_like(carry_ref)

    yb = _dot(o_ref[...], w_b_ref[...])
    mixed = gya_ref[...].astype(F32) + sgb_ref[...].astype(F32) * yb
    h = x_ref[...] + _rms(_dot(mixed.astype(BF16), w_o_ref[...]), g_mix_ref[...])

    u = _rms(h, g_pre_ref[...]).astype(BF16)

    pre = _dot(u, w_up_ref[...])
    x1, x2 = _shifted_rows(carry_ref, pre)
    cw = conv_w_ref[...]
    up = x2 * cw[0:1, :] + x1 * cw[1:2, :] + pre * cw[2:3, :] + b_ref[...]
    gate = up[:, 0:D_FF]
    half_gate = 0.5 * gate
    act = half_gate + half_gate * jnp.tanh(gate * (GELU_C + (GELU_C * 0.044715) * (gate * gate)))
    f = _dot((act * up[:, D_FF:]).astype(BF16), w_down_ref[...])
    h = h + _rms(f, g_post_ref[...])

    e = _dot(p_ref[...].astype(BF16), w_proj_ref[...])
    g = jax.nn.sigmoid(_dot(h.astype(BF16), w_gate_ref[...]))
    out_ref[...] = h + _rms(e * g, g_ple_ref[...])


def _row_spec(ts, width):
    return pl.BlockSpec((None, ts, width), lambda b, s: (b, s, 0))


def _params(sem):
    return pltpu.CompilerParams(dimension_semantics=sem, vmem_limit_bytes=VMEM_LIMIT)


def _layer(x, p, positions, g_mix_pre, w_in, conv_a_w, w_a_out, g_q_lat, w_q_up, g_kv_lat,
           w_kv_up, w_b_out, w_o, g_mix_post, g_ffn_pre, w_ffn_up, conv_ffn_w, b_ffn_conv,
           w_ffn_down, g_ffn_post, w_ple_proj, w_ple_gate, g_ple_post):
    bsz, seq, _ = x.shape
    row = lambda v: v.reshape(1, -1).astype(F32)

    c0 = 3 * CONV_DIM
    c1 = c0 + Q_LORA
    c2 = c1 + KV_LORA
    c3 = c2 + ROPE_DIM
    c4 = c3 + D_MODEL
    half = ROPE_DIM // 2
    w_kr = jnp.pad(w_in[:, c2:c3], ((0, 0), (NOPE_DIM, HEAD_SLOT - QK_DIM)))
    w_abc = w_in[:, 0:c0].astype(BF16)
    w_lat = jnp.concatenate([w_in[:, c0:c2], w_kr], axis=1).astype(BF16)
    w_ga = w_in[:, c3:c4].astype(BF16)
    w_gb = w_in[:, c4:].astype(BF16)

    wq = w_q_up.reshape(Q_LORA, N_HEADS, QK_DIM)
    w_q = jnp.pad(wq, ((0, 0), (0, 0), (0, HEAD_SLOT - QK_DIM))).reshape(Q_LORA, HP).astype(BF16)

    wkv = w_kv_up.reshape(KV_LORA, N_HEADS, NOPE_DIM + V_DIM)
    w_k = jnp.pad(wkv[:, :, :NOPE_DIM], ((0, 0), (0, 0), (0, HEAD_SLOT - NOPE_DIM))).reshape(KV_LORA, HP).astype(BF16)
    w_vt = jnp.pad(wkv[:, :, NOPE_DIM:], ((0, 0), (0, 0), (0, HEAD_SLOT - V_DIM))).reshape(KV_LORA, HP).T.astype(BF16)
    vone = jnp.tile((jnp.arange(HEAD_SLOT) == V_DIM).astype(F32), N_HEADS).reshape(HP, 1)

    inv = 1.0 / (ROPE_THETA ** (jnp.arange(half, dtype=F32) * (2.0 / ROPE_DIM)))
    invf = jnp.concatenate([inv, inv]).reshape(ROPE_DIM, 1)
    pos3 = positions.reshape(bsz, 1, seq)
    q_scale = (QK_DIM ** -0.5) * math.log2(math.e)

    act = jax.ShapeDtypeStruct((bsz, seq, HP), BF16)
    steps_per_row = seq // TS_MIX
    later = (w_b_out, w_o, w_ffn_up, w_ffn_down, w_ple_proj, w_ple_gate)
    assert len(later) == N_LATER_WEIGHTS

    def later_spec(w):
        rows, cols = w.shape
        for col_blocks in (1, 2, 4, 8):
            row_blocks = bsz * steps_per_row // col_blocks
            if rows % (row_blocks * BF16_SUBLANES) == 0 and cols % (col_blocks * LANES) == 0:
                break
        else:
            raise ValueError(f"no tile-aligned split of {w.shape} over the mix_in grid")
        return pl.BlockSpec(
            (rows // row_blocks, cols // col_blocks),
            lambda b, s: ((b * steps_per_row + s) // col_blocks, (b * steps_per_row + s) % col_blocks))

    later_specs = [later_spec(w) for w in later]
    q, k, vt, gya, sgb, w_b16, w_o16, w_up16, w_down16, w_proj16, w_gate16 = pl.pallas_call(
        functools.partial(_mix_in_kernel, q_scale=q_scale),
        grid=(bsz, seq // TS_MIX),
        in_specs=[_row_spec(TS_MIX, D_MODEL),
                  pl.BlockSpec((None, 1, TS_MIX), lambda b, s: (b, 0, s)),
                  _const_spec((ROPE_DIM, 1)), _const_spec((1, D_MODEL)),
                  _const_spec((D_MODEL, c0)), _const_spec((D_MODEL, LAT_COLS)),
                  _const_spec((D_MODEL, D_MODEL)), _const_spec((D_MODEL, D_MODEL)),
                  _const_spec((3, CONV_DIM)), _const_spec((CONV_DIM, D_MODEL)),
                  _const_spec((1, Q_LORA)), _const_spec((Q_LORA, HP)),
                  _const_spec((1, KV_LORA)), _const_spec((KV_LORA, HP)),
                  _const_spec((HP, KV_LORA)), _const_spec((HP, 1))] + later_specs,
        out_specs=[_row_spec(TS_MIX, HP), _row_spec(TS_MIX, HP),
                   pl.BlockSpec((None, HP, TS_MIX), lambda b, s: (b, 0, s)),
                   _row_spec(TS_MIX, HP), _row_spec(TS_MIX, HP)] + later_specs,
        out_shape=[act, act, jax.ShapeDtypeStruct((bsz, HP, seq), BF16), act, act]
        + [jax.ShapeDtypeStruct(w.shape, BF16) for w in later],
        scratch_shapes=[pltpu.VMEM((SUBLANES, CONV_DIM), F32)],
        compiler_params=_params(("arbitrary", "arbitrary")),
        name="mix_in",
    )(x, pos3, invf, row(g_mix_pre), w_abc, w_lat, w_ga, w_gb, conv_a_w.astype(F32),
      w_a_out.astype(BF16), row(g_q_lat), w_q, row(g_kv_lat), w_k, w_vt, vone, *later)

    group = HEADS_PER_STEP * HEAD_SLOT
    head_cols = pl.BlockSpec((None, seq, group), lambda b, h: (b, 0, h))
    o = pl.pallas_call(
        _attn_kernel,
        grid=(bsz, N_HEADS // HEADS_PER_STEP),
        in_specs=[head_cols, head_cols,
                  pl.BlockSpec((None, group, seq), lambda b, h: (b, h, 0))],
        out_specs=pl.BlockSpec((None, seq, HEADS_PER_STEP * V_DIM), lambda b, h: (b, 0, h)),
        out_shape=jax.ShapeDtypeStruct((bsz, seq, N_HEADS * V_DIM), BF16),
        scratch_shapes=[pltpu.VMEM((2, HEADS_PER_STEP, TKEY, TQ), F32),
                        pltpu.VMEM((2, HEADS_PER_STEP, 1, TQ), F32),
                        pltpu.VMEM((HEADS_PER_STEP, 1, TQ), F32),
                        pltpu.VMEM((HEADS_PER_STEP, HEAD_SLOT, TQ), F32),
                        pltpu.VMEM((2, HEADS_PER_STEP, HEAD_SLOT, TQ), BF16)],
        compiler_params=_params(("arbitrary", "arbitrary")),
        name="attn",
    )(q, k, vt)

    h3 = pl.pallas_call(
        _post_kernel,
        grid=(bsz, seq // TS_POST),
        in_specs=[_row_spec(TS_POST, D_MODEL), _row_spec(TS_POST, N_HEADS * V_DIM),
                  _row_spec(TS_POST, HP), _row_spec(TS_POST, HP), _row_spec(TS_POST, PLE_DIM),
                  _const_spec((N_HEADS * V_DIM, D_MODEL)), _const_spec((D_MODEL, D_MODEL)),
                  _const_spec((1, D_MODEL)), _const_spec((1, D_MODEL)),
                  _const_spec((D_MODEL, 2 * D_FF)), _const_spec((3, 2 * D_FF)),
                  _const_spec((1, 2 * D_FF)), _const_spec((D_FF, D_MODEL)),
                  _const_spec((1, D_MODEL)), _const_spec((PLE_DIM, D_MODEL)),
                  _const_spec((D_MODEL, D_MODEL)), _const_spec((1, D_MODEL))],
        out_specs=_row_spec(TS_POST, D_MODEL),
        out_shape=jax.ShapeDtypeStruct(x.shape, F32),
        scratch_shapes=[pltpu.VMEM((SUBLANES, 2 * D_FF), F32)],
        compiler_params=_params(("arbitrary", "arbitrary")),
        name="post",
    )(x, o, gya, sgb, p, w_b16, w_o16, row(g_mix_post), row(g_ffn_pre),
      w_up16, conv_ffn_w.astype(F32), row(b_ffn_conv), w_down16,
      row(g_ffn_post), w_proj16, w_gate16, row(g_ple_post))
    return h3


def kernel(x, p, positions, g_mix_pre, w_in, conv_a_w, w_a_out, g_q_lat, w_q_up, g_kv_lat, w_kv_up, w_b_out, w_o, g_mix_post, g_ffn_pre, w_ffn_up, conv_ffn_w, b_ffn_conv, w_ffn_down, g_ffn_post, w_ple_proj, w_ple_gate, g_ple_post):
    h = x
    for i in range(g_mix_pre.shape[0]):
        h = _layer(h, p[i], positions, g_mix_pre[i], w_in[i], conv_a_w[i], w_a_out[i], g_q_lat[i],
                   w_q_up[i], g_kv_lat[i], w_kv_up[i], w_b_out[i], w_o[i], g_mix_post[i],
                   g_ffn_pre[i], w_ffn_up[i], conv_ffn_w[i], b_ffn_conv[i], w_ffn_down[i],
                   g_ffn_post[i], w_ple_proj[i], w_ple_gate[i], g_ple_post[i])
    return h
```

```python
import functools
import math

import jax
import jax.numpy as jnp
from jax.experimental import pallas as pl
from jax.experimental.pallas import tpu as pltpu

D_MODEL = 1024
CHUNK = 64
CONV_DIM = 512
N_HEADS = 8
NOPE_DIM = 64
ROPE_DIM = 32
V_DIM = 64
Q_LORA = 384
KV_LORA = 256
ROPE_THETA = 10000.0
D_FF = 2816
PLE_DIM = 256
EPS = 1e-6
QK_DIM = NOPE_DIM + ROPE_DIM

LANES = 128
SUBLANES = 8
BF16_SUBLANES = 16
N_LATER_WEIGHTS = 6
HEAD_SLOT = LANES
HP = N_HEADS * HEAD_SLOT
LAT_COLS = Q_LORA + KV_LORA + LANES
VMEM_LIMIT = 56 * 1024 * 1024

TS_MIX = 512
TQ = 1024
TKEY = TQ // 2
HEADS_PER_STEP = 2
assert HEADS_PER_STEP * V_DIM == LANES
V_ROWS = pl.cdiv(V_DIM + 1, BF16_SUBLANES) * BF16_SUBLANES
TS_POST = 512

GELU_C = math.sqrt(2.0 / math.pi)
NEG_BIG = -1e30
BF16 = jnp.bfloat16
F32 = jnp.float32


def _rms(x, g):
    return x * jax.lax.rsqrt(jnp.mean(x * x, axis=-1, keepdims=True) + EPS) * g


def _dot(a, b):
    return jnp.dot(a, b, preferred_element_type=F32)


def _const_spec(shape):
    return pl.BlockSpec(shape, lambda *_: (0,) * len(shape))


def _shifted_rows(carry_ref, pre):
    rows = pre.shape[0]
    ext = jnp.concatenate([carry_ref[...], pre], axis=0)
    carry_ref[...] = pre[rows - SUBLANES:rows, :]
    return pltpu.roll(ext, 1, 0)[SUBLANES:], pltpu.roll(ext, 2, 0)[SUBLANES:]


def _mix_in_kernel(x_ref, pos_ref, invf_ref, g_pre_ref, w_abc_ref, w_lat_ref,
                   w_ga_ref, w_gb_ref, conv_w_ref, w_a_out_ref, g_q_ref, w_q_ref,
                   g_kv_ref, w_k_ref, w_v_ref, vone_ref, *rest, q_scale):
    later_f32 = rest[:N_LATER_WEIGHTS]
    q_ref, k_ref, v_ref, gya_ref, sgb_ref = rest[N_LATER_WEIGHTS:N_LATER_WEIGHTS + 5]
    later_bf16 = rest[N_LATER_WEIGHTS + 5:-1]
    carry_ref = rest[-1]
    ts = x_ref.shape[0]

    @pl.when(pl.program_id(1) == 0)
    def _():
        carry_ref[...] = jnp.zeros_like(carry_ref)

    u = _rms(x_ref[...], g_pre_ref[...]).astype(BF16)

    lat = _dot(u, w_lat_ref[...])
    abc = _dot(u, w_abc_ref[...])
    ga = _dot(u, w_ga_ref[...])
    gb = _dot(u, w_gb_ref[...])
    q_lat = lat[:, 0:Q_LORA]
    kv_lat = lat[:, Q_LORA:Q_LORA + KV_LORA]
    kr = lat[:, Q_LORA + KV_LORA:LAT_COLS]

    half = ROPE_DIM // 2
    ang_t = invf_ref[...] * pos_ref[...].astype(F32)
    sin_t = jnp.sin(ang_t)
    rows = lambda n: jnp.zeros((n, ts), F32)
    tail = HEAD_SLOT - QK_DIM
    cos = jnp.concatenate([jnp.ones((NOPE_DIM, ts), F32), jnp.cos(ang_t), rows(tail)], axis=0).T
    sin_dn = jnp.concatenate([rows(NOPE_DIM), -sin_t[0:half], rows(half + tail)], axis=0).T
    sin_up = jnp.concatenate([rows(NOPE_DIM + half), sin_t[half:], rows(tail)], axis=0).T

    def rope(t, c, s_dn, s_up):
        return (t * c + pltpu.roll(t, HEAD_SLOT - half, 1) * s_dn + pltpu.roll(t, half, 1) * s_up)

    qn = _rms(q_lat, g_q_ref[...]).astype(BF16)
    qq = _dot(qn, w_q_ref[...])
    kvn = _rms(kv_lat, g_kv_ref[...]).astype(BF16)
    kk = _dot(kvn, w_k_ref[...])
    vt = jax.lax.dot_general(w_v_ref[...], kvn, (((1,), (1,)), ((), ())),
                             preferred_element_type=F32) + vone_ref[...]
    v_ref[...] = vt.astype(BF16)

    k_pe = rope(kr, cos, sin_dn, sin_up)
    q_tables = (cos * q_scale, sin_dn * q_scale, sin_up * q_scale)
    for h in range(N_HEADS):
        sl = slice(h * HEAD_SLOT, (h + 1) * HEAD_SLOT)
        q_ref[:, sl] = rope(qq[:, sl], *q_tables).astype(BF16)
        k_ref[:, sl] = (kk[:, sl] + k_pe).astype(BF16)

    a_b = abc[:, 0:CONV_DIM]
    ca = abc[:, CONV_DIM:2 * CONV_DIM] * abc[:, 2 * CONV_DIM:3 * CONV_DIM]
    x1, x2 = _shifted_rows(carry_ref, ca)
    cw = conv_w_ref[...]
    cv = ca * cw[2:3, :] + x1 * cw[1:2, :] + x2 * cw[0:1, :]
    ya = _dot((a_b * cv).astype(BF16), w_a_out_ref[...])
    gya_ref[...] = (jax.nn.sigmoid(ga) * ya).astype(BF16)
    sgb_ref[...] = jax.nn.sigmoid(gb).astype(BF16)

    for src, dst in zip(later_f32, later_bf16):
        dst[...] = src[...].astype(BF16)


def _attn_kernel(q_ref, k_ref, vt_ref, o_ref, s_ref, smax_ref, m_ref, acc_ref, qt_ref):
    tk, tq = s_ref.shape[-2:]
    assert tq == 2 * tk
    nq = q_ref.shape[0] // tq
    heads = range(HEADS_PER_STEP)
    lanes = lambda h: slice(h * HEAD_SLOT, (h + 1) * HEAD_SLOT)
    v_rows = lambda h: slice(h * HEAD_SLOT, h * HEAD_SLOT + V_ROWS)
    key_tile = lambda j: pl.ds(pl.multiple_of(j * tk, tk), tk)
    every = slice(0, tq)
    later = slice(tk, tq)

    def transpose_queries(qi):
        for h in heads:
            qt_ref[qi % 2, h] = q_ref[pl.ds(pl.multiple_of(qi * tq, tq), tq), lanes(h)].T

    def scores(qi, j, buf, hs=heads):
        for h in hs:
            sc = _dot(k_ref[key_tile(j), lanes(h)], qt_ref[qi % 2, h])
            s_ref[buf, h] = sc
            smax_ref[buf, h] = jnp.max(sc, axis=0, keepdims=True)

    def scores_later(qi, buf, hs=heads):
        for h in hs:
            s_ref[buf, h, :, later] = _dot(k_ref[key_tile(2 * qi + 1), lanes(h)],
                                           qt_ref[qi % 2, h, :, later])

    def accumulate(j, buf, cols, masked, hs=heads):
        width = cols.stop - cols.start
        if masked:
            ck = jax.lax.broadcasted_iota(jnp.int32, (tk, width), 0) // CHUNK
            cq = jax.lax.broadcasted_iota(jnp.int32, (tk, width), 1) // CHUNK
            load = lambda h: jnp.where(ck <= cq, s_ref[buf, h, :, cols], NEG_BIG)
            tile_max = lambda h: jnp.max(load(h), axis=0, keepdims=True)
        else:
            load = lambda h: s_ref[buf, h, :, cols]
            tile_max = lambda h: smax_ref[buf, h, :, cols]
        for h in hs:
            m = m_ref[h, :, cols]
            m_new = jnp.maximum(m, tile_max(h))
            p = jnp.exp2(load(h) - m_new)
            pv = _dot(vt_ref[v_rows(h), key_tile(j)], p.astype(BF16))
            acc_ref[h, 0:V_ROWS, cols] = acc_ref[h, 0:V_ROWS, cols] * jnp.exp2(m - m_new) + pv
            m_ref[h, :, cols] = m_new

    def query_tile(qi, _):
        next_qi = jnp.minimum(qi + 1, nq - 1)
        m_ref[...] = jnp.full_like(m_ref, NEG_BIG)
        acc_ref[...] = jnp.zeros_like(acc_ref)

        @pl.when(qi + 1 < nq)
        def _():
            transpose_queries(qi + 1)

        def tile_pair(i):
            for h in heads:
                scores(qi, 2 * i + 1, 1, (h,))
                accumulate(2 * i, 0, every, False, (h,))
            for h in heads:
                scores(qi, 2 * i + 2, 0, (h,))
                accumulate(2 * i + 1, 1, every, False, (h,))

        def two_pairs(i, _):
            tile_pair(2 * i)
            tile_pair(2 * i + 1)
            return 0

        jax.lax.fori_loop(0, qi // 2, two_pairs, 0)

        @pl.when(qi % 2 == 1)
        def _():
            tile_pair(qi - 1)

        for h in heads:
            scores_later(qi, 1, (h,))
            accumulate(2 * qi, 0, every, True, (h,))
        for h in heads:
            scores(next_qi, 0, 0, (h,))
            accumulate(2 * qi + 1, 1, later, True, (h,))

        out = jnp.concatenate([acc_ref[h, 0:V_DIM, :] / acc_ref[h, V_DIM:V_DIM + 1, :]
                               for h in heads], axis=0)
        o_ref[pl.ds(pl.multiple_of(qi * tq, tq), tq), :] = out.T.astype(BF16)
        return 0

    transpose_queries(0)
    scores(0, 0, 0)
    jax.lax.fori_loop(0, nq, query_tile, 0)


def _post_kernel(x_ref, o_ref, gya_ref, sgb_ref, p_ref, w_b_ref, w_o_ref, g_mix_ref,
                 g_pre_ref, w_up_ref, conv_w_ref, b_ref, w_down_ref, g_post_ref,
                 w_proj_ref, w_gate_ref, g_ple_ref, out_ref, carry_ref):
    @pl.when(pl.program_id(1) == 0)
    def _():
        carry_ref[...] = jnp.zeros_like(carry_ref)

    yb = _dot(o_ref[...], w_b_ref[...])
    mixed = gya_ref[...].astype(F32) + sgb_ref[...].astype(F32) * yb
    h = x_ref[...] + _rms(_dot(mixed.astype(BF16), w_o_ref[...]), g_mix_ref[...])

    u = _rms(h, g_pre_ref[...]).astype(BF16)

    pre = _dot(u, w_up_ref[...])
    x1, x2 = _shifted_rows(carry_ref, pre)
    cw = conv_w_ref[...]
    up = (pre * cw[2:3, :] + b_ref[...]) + x1 * cw[1:2, :] + x2 * cw[0:1, :]
    gate = up[:, 0:D_FF]
    half_gate = 0.5 * gate
    act = half_gate + half_gate * jnp.tanh(gate * (GELU_C + (GELU_C * 0.044715) * (gate * gate)))
    f = _dot((act * up[:, D_FF:]).astype(BF16), w_down_ref[...])
    h = h + _rms(f, g_post_ref[...])

    e = _dot(p_ref[...].astype(BF16), w_proj_ref[...])
    g = jax.nn.sigmoid(_dot(h.astype(BF16), w_gate_ref[...]))
    out_ref[...] = h + _rms(e * g, g_ple_ref[...])


def _row_spec(ts, width):
    return pl.BlockSpec((None, ts, width), lambda b, s: (b, s, 0))


def _params(sem):
    return pltpu.CompilerParams(dimension_semantics=sem, vmem_limit_bytes=VMEM_LIMIT)


def _layer(x, p, positions, g_mix_pre, w_in, conv_a_w, w_a_out, g_q_lat, w_q_up, g_kv_lat,
           w_kv_up, w_b_out, w_o, g_mix_post, g_ffn_pre, w_ffn_up, conv_ffn_w, b_ffn_conv,
           w_ffn_down, g_ffn_post, w_ple_proj, w_ple_gate, g_ple_post):
    bsz, seq, _ = x.shape
    row = lambda v: v.reshape(1, -1).astype(F32)

    c0 = 3 * CONV_DIM
    c1 = c0 + Q_LORA
    c2 = c1 + KV_LORA
    c3 = c2 + ROPE_DIM
    c4 = c3 + D_MODEL
    half = ROPE_DIM // 2
    w_kr = jnp.pad(w_in[:, c2:c3], ((0, 0), (NOPE_DIM, HEAD_SLOT - QK_DIM)))
    w_abc = w_in[:, 0:c0].astype(BF16)
    w_lat = jnp.concatenate([w_in[:, c0:c2], w_kr], axis=1).astype(BF16)
    w_ga = w_in[:, c3:c4].astype(BF16)
    w_gb = w_in[:, c4:].astype(BF16)

    wq = w_q_up.reshape(Q_LORA, N_HEADS, QK_DIM)
    w_q = jnp.pad(wq, ((0, 0), (0, 0), (0, HEAD_SLOT - QK_DIM))).reshape(Q_LORA, HP).astype(BF16)

    wkv = w_kv_up.reshape(KV_LORA, N_HEADS, NOPE_DIM + V_DIM)
    w_k = jnp.pad(wkv[:, :, :NOPE_DIM], ((0, 0), (0, 0), (0, HEAD_SLOT - NOPE_DIM))).reshape(KV_LORA, HP).astype(BF16)
    w_vt = jnp.pad(wkv[:, :, NOPE_DIM:], ((0, 0), (0, 0), (0, HEAD_SLOT - V_DIM))).reshape(KV_LORA, HP).T.astype(BF16)
    vone = jnp.tile((jnp.arange(HEAD_SLOT) == V_DIM).astype(F32), N_HEADS).reshape(HP, 1)

    inv = 1.0 / (ROPE_THETA ** (jnp.arange(half, dtype=F32) * (2.0 / ROPE_DIM)))
    invf = jnp.concatenate([inv, inv]).reshape(ROPE_DIM, 1)
    pos3 = positions.reshape(bsz, 1, seq)
    q_scale = (QK_DIM ** -0.5) * math.log2(math.e)

    act = jax.ShapeDtypeStruct((bsz, seq, HP), BF16)
    steps_per_row = seq // TS_MIX
    later = (w_b_out, w_o, w_ffn_up, w_ffn_down, w_ple_proj, w_ple_gate)
    assert len(later) == N_LATER_WEIGHTS

    def later_spec(w):
        rows, cols = w.shape
        for col_blocks in (1, 2, 4, 8):
            row_blocks = bsz * steps_per_row // col_blocks
            if rows % (row_blocks * BF16_SUBLANES) == 0 and cols % (col_blocks * LANES) == 0:
                break
        else:
            raise ValueError(f"no tile-aligned split of {w.shape} over the mix_in grid")
        return pl.BlockSpec(
            (rows // row_blocks, cols // col_blocks),
            lambda b, s: ((b * steps_per_row + s) // col_blocks, (b * steps_per_row + s) % col_blocks))

    later_specs = [later_spec(w) for w in later]
    q, k, vt, gya, sgb, w_b16, w_o16, w_up16, w_down16, w_proj16, w_gate16 = pl.pallas_call(
        functools.partial(_mix_in_kernel, q_scale=q_scale),
        grid=(bsz, seq // TS_MIX),
        in_specs=[_row_spec(TS_MIX, D_MODEL),
                  pl.BlockSpec((None, 1, TS_MIX), lambda b, s: (b, 0, s)),
                  _const_spec((ROPE_DIM, 1)), _const_spec((1, D_MODEL)),
                  _const_spec((D_MODEL, c0)), _const_spec((D_MODEL, LAT_COLS)),
                  _const_spec((D_MODEL, D_MODEL)), _const_spec((D_MODEL, D_MODEL)),
                  _const_spec((3, CONV_DIM)), _const_spec((CONV_DIM, D_MODEL)),
                  _const_spec((1, Q_LORA)), _const_spec((Q_LORA, HP)),
                  _const_spec((1, KV_LORA)), _const_spec((KV_LORA, HP)),
                  _const_spec((HP, KV_LORA)), _const_spec((HP, 1))] + later_specs,
        out_specs=[_row_spec(TS_MIX, HP), _row_spec(TS_MIX, HP),
                   pl.BlockSpec((None, HP, TS_MIX), lambda b, s: (b, 0, s)),
                   _row_spec(TS_MIX, HP), _row_spec(TS_MIX, HP)] + later_specs,
        out_shape=[act, act, jax.ShapeDtypeStruct((bsz, HP, seq), BF16), act, act]
        + [jax.ShapeDtypeStruct(w.shape, BF16) for w in later],
        scratch_shapes=[pltpu.VMEM((SUBLANES, CONV_DIM), F32)],
        compiler_params=_params(("arbitrary", "arbitrary")),
        name="mix_in",
    )(x, pos3, invf, row(g_mix_pre), w_abc, w_lat, w_ga, w_gb, conv_a_w.astype(F32),
      w_a_out.astype(BF16), row(g_q_lat), w_q, row(g_kv_lat), w_k, w_vt, vone, *later)

    group = HEADS_PER_STEP * HEAD_SLOT
    head_cols = pl.BlockSpec((None, seq, group), lambda b, h: (b, 0, h))
    o = pl.pallas_call(
        _attn_kernel,
        grid=(bsz, N_HEADS // HEADS_PER_STEP),
        in_specs=[head_cols, head_cols,
                  pl.BlockSpec((None, group, seq), lambda b, h: (b, h, 0))],
        out_specs=pl.BlockSpec((None, seq, HEADS_PER_STEP * V_DIM), lambda b, h: (b, 0, h)),
        out_shape=jax.ShapeDtypeStruct((bsz, seq, N_HEADS * V_DIM), BF16),
        scratch_shapes=[pltpu.VMEM((2, HEADS_PER_STEP, TKEY, TQ), F32),
                        pltpu.VMEM((2, HEADS_PER_STEP, 1, TQ), F32),
                        pltpu.VMEM((HEADS_PER_STEP, 1, TQ), F32),
                        pltpu.VMEM((HEADS_PER_STEP, HEAD_SLOT, TQ), F32),
                        pltpu.VMEM((2, HEADS_PER_STEP, HEAD_SLOT, TQ), BF16)],
        compiler_params=_params(("arbitrary", "arbitrary")),
        name="attn",
    )(q, k, vt)

    h3 = pl.pallas_call(
        _post_kernel,
        grid=(bsz, seq // TS_POST),
        in_specs=[_row_spec(TS_POST, D_MODEL), _row_spec(TS_POST, N_HEADS * V_DIM),
                  _row_spec(TS_POST, HP), _row_spec(TS_POST, HP), _row_spec(TS_POST, PLE_DIM),
                  _const_spec((N_HEADS * V_DIM, D_MODEL)), _const_spec((D_MODEL, D_MODEL)),
                  _const_spec((1, D_MODEL)), _const_spec((1, D_MODEL)),
                  _const_spec((D_MODEL, 2 * D_FF)), _const_spec((3, 2 * D_FF)),
                  _const_spec((1, 2 * D_FF)), _const_spec((D_FF, D_MODEL)),
                  _const_spec((1, D_MODEL)), _const_spec((PLE_DIM, D_MODEL)),
                  _const_spec((D_MODEL, D_MODEL)), _const_spec((1, D_MODEL))],
        out_specs=_row_spec(TS_POST, D_MODEL),
        out_shape=jax.ShapeDtypeStruct(x.shape, F32),
        scratch_shapes=[pltpu.VMEM((SUBLANES, 2 * D_FF), F32)],
        compiler_params=_params(("arbitrary", "arbitrary")),
        name="post",
    )(x, o, gya, sgb, p, w_b16, w_o16, row(g_mix_post), row(g_ffn_pre),
      w_up16, conv_ffn_w.astype(F32), row(b_ffn_conv), w_down16,
      row(g_ffn_post), w_proj16, w_gate16, row(g_ple_post))
    return h3


def kernel(x, p, positions, g_mix_pre, w_in, conv_a_w, w_a_out, g_q_lat, w_q_up, g_kv_lat, w_kv_up, w_b_out, w_o, g_mix_post, g_ffn_pre, w_ffn_up, conv_ffn_w, b_ffn_conv, w_ffn_down, g_ffn_post, w_ple_proj, w_ple_gate, g_ple_post):
    h = x
    for i in range(g_mix_pre.shape[0]):
        h = _layer(h, p[i], positions, g_mix_pre[i], w_in[i], conv_a_w[i], w_a_out[i], g_q_lat[i],
                   w_q_up[i], g_kv_lat[i], w_kv_up[i], w_b_out[i], w_o[i], g_mix_post[i],
                   g_ffn_pre[i], w_ffn_up[i], conv_ffn_w[i], b_ffn_conv[i], w_ffn_down[i],
                   g_ffn_post[i], w_ple_proj[i], w_ple_gate[i], g_ple_post[i])
    return h
```

```python
import functools
import math

import jax
import jax.numpy as jnp
from jax.experimental import pallas as pl
from jax.experimental.pallas import tpu as pltpu

D_MODEL = 1024
CHUNK = 64
CONV_DIM = 512
N_HEADS = 8
NOPE_DIM = 64
ROPE_DIM = 32
V_DIM = 64
Q_LORA = 384
KV_LORA = 256
ROPE_THETA = 10000.0
D_FF = 2816
PLE_DIM = 256
EPS = 1e-6
QK_DIM = NOPE_DIM + ROPE_DIM

LANES = 128
SUBLANES = 8
BF16_SUBLANES = 16
N_LATER_WEIGHTS = 6
HEAD_SLOT = LANES
HP = N_HEADS * HEAD_SLOT
LAT_COLS = Q_LORA + KV_LORA + LANES
IN_LAT = 3 * CONV_DIM
IN_GATE_A = IN_LAT + Q_LORA + KV_LORA + ROPE_DIM
IN_GATE_B = IN_GATE_A + D_MODEL
N_IN = IN_GATE_B + D_MODEL
VMEM_LIMIT = 56 * 1024 * 1024

TS_MIX = 512
TQ = 1024
TKEY = TQ // 2
HEADS_PER_STEP = 2
assert HEADS_PER_STEP * V_DIM == LANES
V_ROWS = pl.cdiv(V_DIM + 1, BF16_SUBLANES) * BF16_SUBLANES
TS_POST = 512

GELU_C = math.sqrt(2.0 / math.pi)
NEG_BIG = -1e30
BF16 = jnp.bfloat16
F32 = jnp.float32


def _rms(x, g):
    return x * jax.lax.rsqrt(jnp.mean(x * x, axis=-1, keepdims=True) + EPS) * g


def _dot(a, b):
    return jnp.dot(a, b, preferred_element_type=F32)


def _const_spec(shape):
    return pl.BlockSpec(shape, lambda *_: (0,) * len(shape))


def _shifted_rows(carry_ref, pre):
    rows = pre.shape[0]
    ext = jnp.concatenate([carry_ref[...], pre], axis=0)
    carry_ref[...] = pre[rows - SUBLANES:rows, :]
    return pltpu.roll(ext, 1, 0)[SUBLANES:], pltpu.roll(ext, 2, 0)[SUBLANES:]


def _mix_in_kernel(x_ref, pos_ref, invf_ref, g_pre_ref, w_in_t_ref,
                   conv_w_ref, w_a_out_ref, g_q_ref, w_q_ref,
                   g_kv_ref, w_k_ref, w_v_ref, vone_ref, *rest, q_scale):
    later_f32 = rest[:N_LATER_WEIGHTS]
    q_ref, k_ref, v_ref, gya_ref, sgb_ref = rest[N_LATER_WEIGHTS:N_LATER_WEIGHTS + 5]
    later_bf16 = rest[N_LATER_WEIGHTS + 5:-1]
    carry_ref = rest[-1]
    ts = x_ref.shape[0]

    @pl.when(pl.program_id(1) == 0)
    def _():
        carry_ref[...] = jnp.zeros_like(carry_ref)

    u = _rms(x_ref[...], g_pre_ref[...]).astype(BF16)

    proj = lambda lo, hi: jax.lax.dot_general(u, w_in_t_ref[lo:hi, :], (((1,), (1,)), ((), ())),
                                              preferred_element_type=F32)
    lat = proj(IN_LAT, IN_LAT + LAT_COLS)
    abc = proj(0, IN_LAT)
    ga = proj(IN_GATE_A, IN_GATE_B)
    gb = proj(IN_GATE_B, N_IN)
    q_lat = lat[:, 0:Q_LORA]
    kv_lat = lat[:, Q_LORA:Q_LORA + KV_LORA]
    lane = jax.lax.broadcasted_iota(jnp.int32, (ts, HEAD_SLOT), 1)
    kr = jnp.where((lane >= NOPE_DIM) & (lane < QK_DIM),
                   pltpu.roll(lat[:, Q_LORA + KV_LORA:LAT_COLS], NOPE_DIM, 1), 0.0)

    half = ROPE_DIM // 2
    ang_t = invf_ref[...] * pos_ref[...].astype(F32)
    sin_t = jnp.sin(ang_t)
    rows = lambda n: jnp.zeros((n, ts), F32)
    tail = HEAD_SLOT - QK_DIM
    cos = jnp.concatenate([jnp.ones((NOPE_DIM, ts), F32), jnp.cos(ang_t), rows(tail)], axis=0).T
    sin_dn = jnp.concatenate([rows(NOPE_DIM), -sin_t[0:half], rows(half + tail)], axis=0).T
    sin_up = jnp.concatenate([rows(NOPE_DIM + half), sin_t[half:], rows(tail)], axis=0).T

    def rope(t, c, s_dn, s_up):
        return (t * c + pltpu.roll(t, HEAD_SLOT - half, 1) * s_dn + pltpu.roll(t, half, 1) * s_up)

    qn = _rms(q_lat, g_q_ref[...]).astype(BF16)
    qq = _dot(qn, w_q_ref[...])
    kvn = _rms(kv_lat, g_kv_ref[...]).astype(BF16)
    kk = _dot(kvn, w_k_ref[...])
    vt = jax.lax.dot_general(w_v_ref[...], kvn, (((1,), (1,)), ((), ())),
                             preferred_element_type=F32) + vone_ref[...]
    v_ref[...] = vt.astype(BF16)

    k_pe = rope(kr, cos, sin_dn, sin_up)
    q_tables = (cos * q_scale, sin_dn * q_scale, sin_up * q_scale)
    for h in range(N_HEADS):
        sl = slice(h * HEAD_SLOT, (h + 1) * HEAD_SLOT)
        q_ref[:, sl] = rope(qq[:, sl], *q_tables).astype(BF16)
        k_ref[:, sl] = (kk[:, sl] + k_pe).astype(BF16)

    a_b = abc[:, 0:CONV_DIM]
    ca = abc[:, CONV_DIM:2 * CONV_DIM] * abc[:, 2 * CONV_DIM:3 * CONV_DIM]
    x1, x2 = _shifted_rows(carry_ref, ca)
    cw = conv_w_ref[...]
    cv = ca * cw[2:3, :] + x1 * cw[1:2, :] + x2 * cw[0:1, :]
    ya = _dot((a_b * cv).astype(BF16), w_a_out_ref[...])
    gya_ref[...] = (jax.nn.sigmoid(ga) * ya).astype(BF16)
    sgb_ref[...] = jax.nn.sigmoid(gb).astype(BF16)

    for src, dst in zip(later_f32, later_bf16):
        dst[...] = src[...].astype(BF16)


def _attn_kernel(q_ref, k_ref, vt_ref, o_ref, s_ref, smax_ref, m_ref, acc_ref, qt_ref):
    tk, tq = s_ref.shape[-2:]
    assert tq == 2 * tk
    nq = q_ref.shape[0] // tq
    heads = range(HEADS_PER_STEP)
    lanes = lambda h: slice(h * HEAD_SLOT, (h + 1) * HEAD_SLOT)
    v_rows = lambda h: slice(h * HEAD_SLOT, h * HEAD_SLOT + V_ROWS)
    key_tile = lambda j: pl.ds(pl.multiple_of(j * tk, tk), tk)
    every = slice(0, tq)
    later = slice(tk, tq)

    def transpose_queries(qi):
        for h in heads:
            qt_ref[qi % 2, h] = q_ref[pl.ds(pl.multiple_of(qi * tq, tq), tq), lanes(h)].T

    def scores(qi, j, buf, hs=heads):
        for h in hs:
            sc = _dot(k_ref[key_tile(j), lanes(h)], qt_ref[qi % 2, h])
            s_ref[buf, h] = sc
            smax_ref[buf, h] = jnp.max(sc, axis=0, keepdims=True)

    def scores_later(qi, buf, hs=heads):
        for h in hs:
            s_ref[buf, h, :, later] = _dot(k_ref[key_tile(2 * qi + 1), lanes(h)],
                                           qt_ref[qi % 2, h, :, later])

    def accumulate(j, buf, cols, masked, hs=heads):
        width = cols.stop - cols.start
        if masked:
            ck = jax.lax.broadcasted_iota(jnp.int32, (tk, width), 0) // CHUNK
            cq = jax.lax.broadcasted_iota(jnp.int32, (tk, width), 1) // CHUNK
            load = lambda h: jnp.where(ck <= cq, s_ref[buf, h, :, cols], NEG_BIG)
            tile_max = lambda h: jnp.max(load(h), axis=0, keepdims=True)
        else:
            load = lambda h: s_ref[buf, h, :, cols]
            tile_max = lambda h: smax_ref[buf, h, :, cols]
        for h in hs:
            m = m_ref[h, :, cols]
            m_new = jnp.maximum(m, tile_max(h))
            p = jnp.exp2(load(h) - m_new)
            pv = _dot(vt_ref[v_rows(h), key_tile(j)], p.astype(BF16))
            acc_ref[h, 0:V_ROWS, cols] = acc_ref[h, 0:V_ROWS, cols] * jnp.exp2(m - m_new) + pv
            m_ref[h, :, cols] = m_new

    def query_tile(qi, _):
        next_qi = jnp.minimum(qi + 1, nq - 1)
        m_ref[...] = jnp.full_like(m_ref, NEG_BIG)
        acc_ref[...] = jnp.zeros_like(acc_ref)

        @pl.when(qi + 1 < nq)
        def _():
            transpose_queries(qi + 1)

        def tile_pair(i):
            for h in heads:
                scores(qi, 2 * i + 1, 1, (h,))
                accumulate(2 * i, 0, every, False, (h,))
            for h in heads:
                scores(qi, 2 * i + 2, 0, (h,))
                accumulate(2 * i + 1, 1, every, False, (h,))

        def two_pairs(i, _):
            tile_pair(2 * i)
            tile_pair(2 * i + 1)
            return 0

        jax.lax.fori_loop(0, qi // 2, two_pairs, 0)

        @pl.when(qi % 2 == 1)
        def _():
            tile_pair(qi - 1)

        for h in heads:
            scores_later(qi, 1, (h,))
            accumulate(2 * qi, 0, every, True, (h,))
        for h in heads:
            scores(next_qi, 0, 0, (h,))
            accumulate(2 * qi + 1, 1, later, True, (h,))

        out = jnp.concatenate([acc_ref[h, 0:V_DIM, :] / acc_ref[h, V_DIM:V_DIM + 1, :]
                               for h in heads], axis=0)
        o_ref[pl.ds(pl.multiple_of(qi * tq, tq), tq), :] = out.T.astype(BF16)
        return 0

    transpose_queries(0)
    scores(0, 0, 0)
    jax.lax.fori_loop(0, nq, query_tile, 0)


def _post_kernel(x_ref, o_ref, gya_ref, sgb_ref, p_ref, w_b_ref, w_o_ref, g_mix_ref,
                 g_pre_ref, w_up_ref, conv_w_ref, b_ref, w_down_ref, g_post_ref,
                 w_proj_ref, w_gate_ref, g_ple_ref, out_ref, carry_ref):
    @pl.when(pl.program_id(1) == 0)
    def _():
        carry_ref[...] = jnp.zeros_like(carry_ref)

    yb = _dot(o_ref[...], w_b_ref[...])
    mixed = gya_ref[...].astype(F32) + sgb_ref[...].astype(F32) * yb
    h = x_ref[...] + _rms(_dot(mixed.astype(BF16), w_o_ref[...]), g_mix_ref[...])

    u = _rms(h, g_pre_ref[...]).astype(BF16)

    pre = _dot(u, w_up_ref[...])
    x1, x2 = _shifted_rows(carry_ref, pre)
    cw = conv_w_ref[...]
    up = (pre * cw[2:3, :] + b_ref[...]) + x1 * cw[1:2, :] + x2 * cw[0:1, :]
    gate = up[:, 0:D_FF]
    half_gate = 0.5 * gate
    act = half_gate + half_gate * jnp.tanh(gate * (GELU_C + (GELU_C * 0.044715) * (gate * gate)))
    f = _dot((act * up[:, D_FF:]).astype(BF16), w_down_ref[...])
    h = h + _rms(f, g_post_ref[...])

    e = _dot(p_ref[...].astype(BF16), w_proj_ref[...])
    g = jax.nn.sigmoid(_dot(h.astype(BF16), w_gate_ref[...]))
    out_ref[...] = h + _rms(e * g, g_ple_ref[...])


def _row_spec(ts, width):
    return pl.BlockSpec((None, ts, width), lambda b, s: (b, s, 0))


def _params(sem):
    return pltpu.CompilerParams(dimension_semantics=sem, vmem_limit_bytes=VMEM_LIMIT)


def _layer(x, p, positions, g_mix_pre, w_in, conv_a_w, w_a_out, g_q_lat, w_q_up, g_kv_lat,
           w_kv_up, w_b_out, w_o, g_mix_post, g_ffn_pre, w_ffn_up, conv_ffn_w, b_ffn_conv,
           w_ffn_down, g_ffn_post, w_ple_proj, w_ple_gate, g_ple_post):
    bsz, seq, _ = x.shape
    row = lambda v: v.reshape(1, -1).astype(F32)

    half = ROPE_DIM // 2
    assert w_in.shape == (D_MODEL, N_IN)
    w_in_t = w_in.T.astype(BF16)

    wq = w_q_up.reshape(Q_LORA, N_HEADS, QK_DIM)
    w_q = jnp.pad(wq, ((0, 0), (0, 0), (0, HEAD_SLOT - QK_DIM))).reshape(Q_LORA, HP).astype(BF16)

    wkv = w_kv_up.reshape(KV_LORA, N_HEADS, NOPE_DIM + V_DIM)
    w_k = jnp.pad(wkv[:, :, :NOPE_DIM], ((0, 0), (0, 0), (0, HEAD_SLOT - NOPE_DIM))).reshape(KV_LORA, HP).astype(BF16)
    w_vt = jnp.pad(wkv[:, :, NOPE_DIM:], ((0, 0), (0, 0), (0, HEAD_SLOT - V_DIM))).reshape(KV_LORA, HP).T.astype(BF16)
    vone = jnp.tile((jnp.arange(HEAD_SLOT) == V_DIM).astype(F32), N_HEADS).reshape(HP, 1)

    inv = 1.0 / (ROPE_THETA ** (jnp.arange(half, dtype=F32) * (2.0 / ROPE_DIM)))
    invf = jnp.concatenate([inv, inv]).reshape(ROPE_DIM, 1)
    pos3 = positions.reshape(bsz, 1, seq)
    q_scale = (QK_DIM ** -0.5) * math.log2(math.e)

    act = jax.ShapeDtypeStruct((bsz, seq, HP), BF16)
    steps_per_row = seq // TS_MIX
    later = (w_b_out, w_o, w_ffn_up, w_ffn_down, w_ple_proj, w_ple_gate)
    assert len(later) == N_LATER_WEIGHTS

    def later_spec(w):
        rows, cols = w.shape
        for col_blocks in (1, 2, 4, 8):
            row_blocks = bsz * steps_per_row // col_blocks
            if rows % (row_blocks * BF16_SUBLANES) == 0 and cols % (col_blocks * LANES) == 0:
                break
        else:
            raise ValueError(f"no tile-aligned split of {w.shape} over the mix_in grid")
        return pl.BlockSpec(
            (rows // row_blocks, cols // col_blocks),
            lambda b, s: ((b * steps_per_row + s) // col_blocks, (b * steps_per_row + s) % col_blocks))

    later_specs = [later_spec(w) for w in later]
    q, k, vt, gya, sgb, w_b16, w_o16, w_up16, w_down16, w_proj16, w_gate16 = pl.pallas_call(
        functools.partial(_mix_in_kernel, q_scale=q_scale),
        grid=(bsz, seq // TS_MIX),
        in_specs=[_row_spec(TS_MIX, D_MODEL),
                  pl.BlockSpec((None, 1, TS_MIX), lambda b, s: (b, 0, s)),
                  _const_spec((ROPE_DIM, 1)), _const_spec((1, D_MODEL)),
                  _const_spec((N_IN, D_MODEL)),
                  _const_spec((3, CONV_DIM)), _const_spec((CONV_DIM, D_MODEL)),
                  _const_spec((1, Q_LORA)), _const_spec((Q_LORA, HP)),
                  _const_spec((1, KV_LORA)), _const_spec((KV_LORA, HP)),
                  _const_spec((HP, KV_LORA)), _const_spec((HP, 1))] + later_specs,
        out_specs=[_row_spec(TS_MIX, HP), _row_spec(TS_MIX, HP),
                   pl.BlockSpec((None, HP, TS_MIX), lambda b, s: (b, 0, s)),
                   _row_spec(TS_MIX, HP), _row_spec(TS_MIX, HP)] + later_specs,
        out_shape=[act, act, jax.ShapeDtypeStruct((bsz, HP, seq), BF16), act, act]
        + [jax.ShapeDtypeStruct(w.shape, BF16) for w in later],
        scratch_shapes=[pltpu.VMEM((SUBLANES, CONV_DIM), F32)],
        compiler_params=_params(("arbitrary", "arbitrary")),
        name="mix_in",
    )(x, pos3, invf, row(g_mix_pre), w_in_t, conv_a_w.astype(F32),
      w_a_out.astype(BF16), row(g_q_lat), w_q, row(g_kv_lat), w_k, w_vt, vone, *later)

    group = HEADS_PER_STEP * HEAD_SLOT
    head_cols = pl.BlockSpec((None, seq, group), lambda b, h: (b, 0, h))
    o = pl.pallas_call(
        _attn_kernel,
        grid=(bsz, N_HEADS // HEADS_PER_STEP),
        in_specs=[head_cols, head_cols,
                  pl.BlockSpec((None, group, seq), lambda b, h: (b, h, 0))],
        out_specs=pl.BlockSpec((None, seq, HEADS_PER_STEP * V_DIM), lambda b, h: (b, 0, h)),
        out_shape=jax.ShapeDtypeStruct((bsz, seq, N_HEADS * V_DIM), BF16),
        scratch_shapes=[pltpu.VMEM((2, HEADS_PER_STEP, TKEY, TQ), F32),
                        pltpu.VMEM((2, HEADS_PER_STEP, 1, TQ), F32),
                        pltpu.VMEM((HEADS_PER_STEP, 1, TQ), F32),
                        pltpu.VMEM((HEADS_PER_STEP, HEAD_SLOT, TQ), F32),
                        pltpu.VMEM((2, HEADS_PER_STEP, HEAD_SLOT, TQ), BF16)],
        compiler_params=_params(("arbitrary", "arbitrary")),
        name="attn",
    )(q, k, vt)

    h3 = pl.pallas_call(
        _post_kernel,
        grid=(bsz, seq // TS_POST),
        in_specs=[_row_spec(TS_POST, D_MODEL), _row_spec(TS_POST, N_HEADS * V_DIM),
                  _row_spec(TS_POST, HP), _row_spec(TS_POST, HP), _row_spec(TS_POST, PLE_DIM),
                  _const_spec((N_HEADS * V_DIM, D_MODEL)), _const_spec((D_MODEL, D_MODEL)),
                  _const_spec((1, D_MODEL)), _const_spec((1, D_MODEL)),
                  _const_spec((D_MODEL, 2 * D_FF)), _const_spec((3, 2 * D_FF)),
                  _const_spec((1, 2 * D_FF)), _const_spec((D_FF, D_MODEL)),
                  _const_spec((1, D_MODEL)), _const_spec((PLE_DIM, D_MODEL)),
                  _const_spec((D_MODEL, D_MODEL)), _const_spec((1, D_MODEL))],
        out_specs=_row_spec(TS_POST, D_MODEL),
        out_shape=jax.ShapeDtypeStruct(x.shape, F32),
        scratch_shapes=[pltpu.VMEM((SUBLANES, 2 * D_FF), F32)],
        compiler_params=_params(("arbitrary", "arbitrary")),
        name="post",
    )(x, o, gya, sgb, p, w_b16, w_o16, row(g_mix_post), row(g_ffn_pre),
      w_up16, conv_ffn_w.astype(F32), row(b_ffn_conv), w_down16,
      row(g_ffn_post), w_proj16, w_gate16, row(g_ple_post))
    return h3


def kernel(x, p, positions, g_mix_pre, w_in, conv_a_w, w_a_out, g_q_lat, w_q_up, g_kv_lat, w_kv_up, w_b_out, w_o, g_mix_post, g_ffn_pre, w_ffn_up, conv_ffn_w, b_ffn_conv, w_ffn_down, g_ffn_post, w_ple_proj, w_ple_gate, g_ple_post):
    h = x
    for i in range(g_mix_pre.shape[0]):
        h = _layer(h, p[i], positions, g_mix_pre[i], w_in[i], conv_a_w[i], w_a_out[i], g_q_lat[i],
                   w_q_up[i], g_kv_lat[i], w_kv_up[i], w_b_out[i], w_o[i], g_mix_post[i],
                   g_ffn_pre[i], w_ffn_up[i], conv_ffn_w[i], b_ffn_conv[i], w_ffn_down[i],
                   g_ffn_post[i], w_ple_proj[i], w_ple_gate[i], g_ple_post[i])
    return h
```

```python
import functools
import math

import jax
import jax.numpy as jnp
from jax.experimental import pallas as pl
from jax.experimental.pallas import tpu as pltpu

D_MODEL = 1024
CHUNK = 64
CONV_DIM = 512
N_HEADS = 8
NOPE_DIM = 64
ROPE_DIM = 32
V_DIM = 64
Q_LORA = 384
KV_LORA = 256
ROPE_THETA = 10000.0
D_FF = 2816
PLE_DIM = 256
EPS = 1e-6
QK_DIM = NOPE_DIM + ROPE_DIM

LANES = 128
SUBLANES = 8
BF16_SUBLANES = 16
N_LATER_WEIGHTS = 6
HEAD_SLOT = LANES
HP = N_HEADS * HEAD_SLOT
LAT_COLS = Q_LORA + KV_LORA + LANES
IN_LAT = 3 * CONV_DIM
IN_GATE_A = IN_LAT + Q_LORA + KV_LORA + ROPE_DIM
IN_GATE_B = IN_GATE_A + D_MODEL
N_IN = IN_GATE_B + D_MODEL
VMEM_LIMIT = 56 * 1024 * 1024

TS_MIX = 512
TQ = 1024
TKEY = TQ // 2
HEADS_PER_STEP = 2
assert HEADS_PER_STEP * V_DIM == LANES
V_ROWS = pl.cdiv(V_DIM + 1, BF16_SUBLANES) * BF16_SUBLANES
TS_POST = 512

GELU_C = math.sqrt(2.0 / math.pi)
NEG_BIG = -1e30
BF16 = jnp.bfloat16
F32 = jnp.float32


def _rms(x, g):
    return x * jax.lax.rsqrt(jnp.mean(x * x, axis=-1, keepdims=True) + EPS) * g


def _dot(a, b):
    return jnp.dot(a, b, preferred_element_type=F32)


def _const_spec(shape):
    return pl.BlockSpec(shape, lambda *_: (0,) * len(shape))


def _shifted_rows(carry_ref, pre):
    rows = pre.shape[0]
    ext = jnp.concatenate([carry_ref[...], pre], axis=0)
    carry_ref[...] = pre[rows - SUBLANES:rows, :]
    return pltpu.roll(ext, 1, 0)[SUBLANES:], pltpu.roll(ext, 2, 0)[SUBLANES:]


def _mix_in_kernel(x_ref, pos_ref, invf_ref, g_pre_ref, w_in_t_ref,
                   conv_w_ref, w_a_out_ref, g_q_ref, w_q_ref,
                   g_kv_ref, w_k_ref, w_v_ref, vone_ref, *rest, q_scale):
    later_f32 = rest[:N_LATER_WEIGHTS]
    q_ref, k_ref, v_ref, gya_ref, sgb_ref = rest[N_LATER_WEIGHTS:N_LATER_WEIGHTS + 5]
    later_bf16 = rest[N_LATER_WEIGHTS + 5:-1]
    carry_ref = rest[-1]
    ts = x_ref.shape[0]

    @pl.when(pl.program_id(1) == 0)
    def _():
        carry_ref[...] = jnp.zeros_like(carry_ref)

    u = _rms(x_ref[...], g_pre_ref[...]).astype(BF16)

    proj = lambda lo, hi: jax.lax.dot_general(u, w_in_t_ref[lo:hi, :].astype(BF16), (((1,), (1,)), ((), ())),
                                              preferred_element_type=F32)
    lat = proj(IN_LAT, IN_LAT + LAT_COLS)
    abc = proj(0, IN_LAT)
    ga = proj(IN_GATE_A, IN_GATE_B)
    gb = proj(IN_GATE_B, N_IN)
    q_lat = lat[:, 0:Q_LORA]
    kv_lat = lat[:, Q_LORA:Q_LORA + KV_LORA]
    lane = jax.lax.broadcasted_iota(jnp.int32, (ts, HEAD_SLOT), 1)
    kr = jnp.where((lane >= NOPE_DIM) & (lane < QK_DIM),
                   pltpu.roll(lat[:, Q_LORA + KV_LORA:LAT_COLS], NOPE_DIM, 1), 0.0)

    half = ROPE_DIM // 2
    ang_t = invf_ref[...] * pos_ref[...].astype(F32)
    sin_t = jnp.sin(ang_t)
    rows = lambda n: jnp.zeros((n, ts), F32)
    tail = HEAD_SLOT - QK_DIM
    cos = jnp.concatenate([jnp.ones((NOPE_DIM, ts), F32), jnp.cos(ang_t), rows(tail)], axis=0).T
    sin_dn = jnp.concatenate([rows(NOPE_DIM), -sin_t[0:half], rows(half + tail)], axis=0).T
    sin_up = jnp.concatenate([rows(NOPE_DIM + half), sin_t[half:], rows(tail)], axis=0).T

    def rope(t, c, s_dn, s_up):
        return (t * c + pltpu.roll(t, HEAD_SLOT - half, 1) * s_dn + pltpu.roll(t, half, 1) * s_up)

    qn = _rms(q_lat, g_q_ref[...]).astype(BF16)
    qq = _dot(qn, w_q_ref[...])
    kvn = _rms(kv_lat, g_kv_ref[...]).astype(BF16)
    kk = _dot(kvn, w_k_ref[...])
    vt = jax.lax.dot_general(w_v_ref[...], kvn, (((1,), (1,)), ((), ())),
                             preferred_element_type=F32) + vone_ref[...]
    v_ref[...] = vt.astype(BF16)

    k_pe = rope(kr, cos, sin_dn, sin_up)
    q_tables = (cos * q_scale, sin_dn * q_scale, sin_up * q_scale)
    for h in range(N_HEADS):
        sl = slice(h * HEAD_SLOT, (h + 1) * HEAD_SLOT)
        q_ref[:, sl] = rope(qq[:, sl], *q_tables).astype(BF16)
        k_ref[:, sl] = (kk[:, sl] + k_pe).astype(BF16)

    a_b = abc[:, 0:CONV_DIM]
    ca = abc[:, CONV_DIM:2 * CONV_DIM] * abc[:, 2 * CONV_DIM:3 * CONV_DIM]
    x1, x2 = _shifted_rows(carry_ref, ca)
    cw = conv_w_ref[...]
    cv = ca * cw[2:3, :] + x1 * cw[1:2, :] + x2 * cw[0:1, :]
    ya = _dot((a_b * cv).astype(BF16), w_a_out_ref[...])
    gya_ref[...] = (jax.nn.sigmoid(ga) * ya).astype(BF16)
    sgb_ref[...] = jax.nn.sigmoid(gb).astype(BF16)

    for src, dst in zip(later_f32, later_bf16):
        dst[...] = src[...].astype(BF16)


def _attn_kernel(q_ref, k_ref, vt_ref, o_ref, s_ref, smax_ref, m_ref, acc_ref, qt_ref):
    tk, tq = s_ref.shape[-2:]
    assert tq == 2 * tk
    nq = q_ref.shape[0] // tq
    heads = range(HEADS_PER_STEP)
    lanes = lambda h: slice(h * HEAD_SLOT, (h + 1) * HEAD_SLOT)
    v_rows = lambda h: slice(h * HEAD_SLOT, h * HEAD_SLOT + V_ROWS)
    key_tile = lambda j: pl.ds(pl.multiple_of(j * tk, tk), tk)
    every = slice(0, tq)
    later = slice(tk, tq)

    def transpose_queries(qi):
        for h in heads:
            qt_ref[qi % 2, h] = q_ref[pl.ds(pl.multiple_of(qi * tq, tq), tq), lanes(h)].T

    def scores(qi, j, buf, hs=heads):
        for h in hs:
            sc = _dot(k_ref[key_tile(j), lanes(h)], qt_ref[qi % 2, h])
            s_ref[buf, h] = sc
            smax_ref[buf, h] = jnp.max(sc, axis=0, keepdims=True)

    def scores_later(qi, buf, hs=heads):
        for h in hs:
            s_ref[buf, h, :, later] = _dot(k_ref[key_tile(2 * qi + 1), lanes(h)],
                                           qt_ref[qi % 2, h, :, later])

    def accumulate(j, buf, cols, masked, hs=heads):
        width = cols.stop - cols.start
        if masked:
            ck = jax.lax.broadcasted_iota(jnp.int32, (tk, width), 0) // CHUNK
            cq = jax.lax.broadcasted_iota(jnp.int32, (tk, width), 1) // CHUNK
            load = lambda h: jnp.where(ck <= cq, s_ref[buf, h, :, cols], NEG_BIG)
            tile_max = lambda h: jnp.max(load(h), axis=0, keepdims=True)
        else:
            load = lambda h: s_ref[buf, h, :, cols]
            tile_max = lambda h: smax_ref[buf, h, :, cols]
        for h in hs:
            m = m_ref[h, :, cols]
            m_new = jnp.maximum(m, tile_max(h))
            p = jnp.exp2(load(h) - m_new)
            pv = _dot(vt_ref[v_rows(h), key_tile(j)], p.astype(BF16))
            acc_ref[h, 0:V_ROWS, cols] = acc_ref[h, 0:V_ROWS, cols] * jnp.exp2(m - m_new) + pv
            m_ref[h, :, cols] = m_new

    def query_tile(qi, _):
        next_qi = jnp.minimum(qi + 1, nq - 1)
        m_ref[...] = jnp.full_like(m_ref, NEG_BIG)
        acc_ref[...] = jnp.zeros_like(acc_ref)

        @pl.when(qi + 1 < nq)
        def _():
            transpose_queries(qi + 1)

        def tile_pair(i):
            for h in heads:
                scores(qi, 2 * i + 1, 1, (h,))
                accumulate(2 * i, 0, every, False, (h,))
            for h in heads:
                scores(qi, 2 * i + 2, 0, (h,))
                accumulate(2 * i + 1, 1, every, False, (h,))

        def two_pairs(i, _):
            tile_pair(2 * i)
            tile_pair(2 * i + 1)
            return 0

        jax.lax.fori_loop(0, qi // 2, two_pairs, 0)

        @pl.when(qi % 2 == 1)
        def _():
            tile_pair(qi - 1)

        for h in heads:
            scores_later(qi, 1, (h,))
            accumulate(2 * qi, 0, every, True, (h,))
        for h in heads:
            scores(next_qi, 0, 0, (h,))
            accumulate(2 * qi + 1, 1, later, True, (h,))

        out = jnp.concatenate([acc_ref[h, 0:V_DIM, :] / acc_ref[h, V_DIM:V_DIM + 1, :]
                               for h in heads], axis=0)
        o_ref[pl.ds(pl.multiple_of(qi * tq, tq), tq), :] = out.T.astype(BF16)
        return 0

    transpose_queries(0)
    scores(0, 0, 0)
    jax.lax.fori_loop(0, nq, query_tile, 0)


def _post_kernel(x_ref, o_ref, gya_ref, sgb_ref, p_ref, w_b_ref, w_o_ref, g_mix_ref,
                 g_pre_ref, w_up_ref, conv_w_ref, b_ref, w_down_ref, g_post_ref,
                 w_proj_ref, w_gate_ref, g_ple_ref, out_ref, carry_ref):
    @pl.when(pl.program_id(1) == 0)
    def _():
        carry_ref[...] = jnp.zeros_like(carry_ref)

    yb = _dot(o_ref[...], w_b_ref[...])
    mixed = gya_ref[...].astype(F32) + sgb_ref[...].astype(F32) * yb
    h = x_ref[...] + _rms(_dot(mixed.astype(BF16), w_o_ref[...]), g_mix_ref[...])

    u = _rms(h, g_pre_ref[...]).astype(BF16)

    pre = _dot(u, w_up_ref[...])
    x1, x2 = _shifted_rows(carry_ref, pre)
    cw = conv_w_ref[...]
    up = (pre * cw[2:3, :] + b_ref[...]) + x1 * cw[1:2, :] + x2 * cw[0:1, :]
    gate = up[:, 0:D_FF]
    half_gate = 0.5 * gate
    act = half_gate + half_gate * jnp.tanh(gate * (GELU_C + (GELU_C * 0.044715) * (gate * gate)))
    f = _dot((act * up[:, D_FF:]).astype(BF16), w_down_ref[...])
    h = h + _rms(f, g_post_ref[...])

    e = _dot(p_ref[...].astype(BF16), w_proj_ref[...])
    g = jax.nn.sigmoid(_dot(h.astype(BF16), w_gate_ref[...]))
    out_ref[...] = h + _rms(e * g, g_ple_ref[...])


def _row_spec(ts, width):
    return pl.BlockSpec((None, ts, width), lambda b, s: (b, s, 0))


def _params(sem):
    return pltpu.CompilerParams(dimension_semantics=sem, vmem_limit_bytes=VMEM_LIMIT)


def _layer(x, p, positions, g_mix_pre, w_in, conv_a_w, w_a_out, g_q_lat, w_q_up, g_kv_lat,
           w_kv_up, w_b_out, w_o, g_mix_post, g_ffn_pre, w_ffn_up, conv_ffn_w, b_ffn_conv,
           w_ffn_down, g_ffn_post, w_ple_proj, w_ple_gate, g_ple_post):
    bsz, seq, _ = x.shape
    row = lambda v: v.reshape(1, -1).astype(F32)

    half = ROPE_DIM // 2
    assert w_in.shape == (D_MODEL, N_IN)
    w_in_t = w_in.T

    wq = w_q_up.reshape(Q_LORA, N_HEADS, QK_DIM)
    w_q = jnp.pad(wq, ((0, 0), (0, 0), (0, HEAD_SLOT - QK_DIM))).reshape(Q_LORA, HP).astype(BF16)

    wkv = w_kv_up.reshape(KV_LORA, N_HEADS, NOPE_DIM + V_DIM)
    w_k = jnp.pad(wkv[:, :, :NOPE_DIM], ((0, 0), (0, 0), (0, HEAD_SLOT - NOPE_DIM))).reshape(KV_LORA, HP).astype(BF16)
    w_vt = jnp.pad(wkv[:, :, NOPE_DIM:], ((0, 0), (0, 0), (0, HEAD_SLOT - V_DIM))).reshape(KV_LORA, HP).T.astype(BF16)
    vone = jnp.tile((jnp.arange(HEAD_SLOT) == V_DIM).astype(F32), N_HEADS).reshape(HP, 1)

    inv = 1.0 / (ROPE_THETA ** (jnp.arange(half, dtype=F32) * (2.0 / ROPE_DIM)))
    invf = jnp.concatenate([inv, inv]).reshape(ROPE_DIM, 1)
    pos3 = positions.reshape(bsz, 1, seq)
    q_scale = (QK_DIM ** -0.5) * math.log2(math.e)

    act = jax.ShapeDtypeStruct((bsz, seq, HP), BF16)
    steps_per_row = seq // TS_MIX
    later = (w_b_out, w_o, w_ffn_up, w_ffn_down, w_ple_proj, w_ple_gate)
    assert len(later) == N_LATER_WEIGHTS

    def later_spec(w):
        rows, cols = w.shape
        for col_blocks in (1, 2, 4, 8):
            row_blocks = bsz * steps_per_row // col_blocks
            if rows % (row_blocks * BF16_SUBLANES) == 0 and cols % (col_blocks * LANES) == 0:
                break
        else:
            raise ValueError(f"no tile-aligned split of {w.shape} over the mix_in grid")
        return pl.BlockSpec(
            (rows // row_blocks, cols // col_blocks),
            lambda b, s: ((b * steps_per_row + s) // col_blocks, (b * steps_per_row + s) % col_blocks))

    later_specs = [later_spec(w) for w in later]
    q, k, vt, gya, sgb, w_b16, w_o16, w_up16, w_down16, w_proj16, w_gate16 = pl.pallas_call(
        functools.partial(_mix_in_kernel, q_scale=q_scale),
        grid=(bsz, seq // TS_MIX),
        in_specs=[_row_spec(TS_MIX, D_MODEL),
                  pl.BlockSpec((None, 1, TS_MIX), lambda b, s: (b, 0, s)),
                  _const_spec((ROPE_DIM, 1)), _const_spec((1, D_MODEL)),
                  _const_spec((N_IN, D_MODEL)),
                  _const_spec((3, CONV_DIM)), _const_spec((CONV_DIM, D_MODEL)),
                  _const_spec((1, Q_LORA)), _const_spec((Q_LORA, HP)),
                  _const_spec((1, KV_LORA)), _const_spec((KV_LORA, HP)),
                  _const_spec((HP, KV_LORA)), _const_spec((HP, 1))] + later_specs,
        out_specs=[_row_spec(TS_MIX, HP), _row_spec(TS_MIX, HP),
                   pl.BlockSpec((None, HP, TS_MIX), lambda b, s: (b, 0, s)),
                   _row_spec(TS_MIX, HP), _row_spec(TS_MIX, HP)] + later_specs,
        out_shape=[act, act, jax.ShapeDtypeStruct((bsz, HP, seq), BF16), act, act]
        + [jax.ShapeDtypeStruct(w.shape, BF16) for w in later],
        scratch_shapes=[pltpu.VMEM((SUBLANES, CONV_DIM), F32)],
        compiler_params=_params(("arbitrary", "arbitrary")),
        name="mix_in",
    )(x, pos3, invf, row(g_mix_pre), w_in_t, conv_a_w.astype(F32),
      w_a_out.astype(BF16), row(g_q_lat), w_q, row(g_kv_lat), w_k, w_vt, vone, *later)

    group = HEADS_PER_STEP * HEAD_SLOT
    head_cols = pl.BlockSpec((None, seq, group), lambda b, h: (b, 0, h))
    o = pl.pallas_call(
        _attn_kernel,
        grid=(bsz, N_HEADS // HEADS_PER_STEP),
        in_specs=[head_cols, head_cols,
                  pl.BlockSpec((None, group, seq), lambda b, h: (b, h, 0))],
        out_specs=pl.BlockSpec((None, seq, HEADS_PER_STEP * V_DIM), lambda b, h: (b, 0, h)),
        out_shape=jax.ShapeDtypeStruct((bsz, seq, N_HEADS * V_DIM), BF16),
        scratch_shapes=[pltpu.VMEM((2, HEADS_PER_STEP, TKEY, TQ), F32),
                        pltpu.VMEM((2, HEADS_PER_STEP, 1, TQ), F32),
                        pltpu.VMEM((HEADS_PER_STEP, 1, TQ), F32),
                        pltpu.VMEM((HEADS_PER_STEP, HEAD_SLOT, TQ), F32),
                        pltpu.VMEM((2, HEADS_PER_STEP, HEAD_SLOT, TQ), BF16)],
        compiler_params=_params(("arbitrary", "arbitrary")),
        name="attn",
    )(q, k, vt)

    h3 = pl.pallas_call(
        _post_kernel,
        grid=(bsz, seq // TS_POST),
        in_specs=[_row_spec(TS_POST, D_MODEL), _row_spec(TS_POST, N_HEADS * V_DIM),
                  _row_spec(TS_POST, HP), _row_spec(TS_POST, HP), _row_spec(TS_POST, PLE_DIM),
                  _const_spec((N_HEADS * V_DIM, D_MODEL)), _const_spec((D_MODEL, D_MODEL)),
                  _const_spec((1, D_MODEL)), _const_spec((1, D_MODEL)),
                  _const_spec((D_MODEL, 2 * D_FF)), _const_spec((3, 2 * D_FF)),
                  _const_spec((1, 2 * D_FF)), _const_spec((D_FF, D_MODEL)),
                  _const_spec((1, D_MODEL)), _const_spec((PLE_DIM, D_MODEL)),
                  _const_spec((D_MODEL, D_MODEL)), _const_spec((1, D_MODEL))],
        out_specs=_row_spec(TS_POST, D_MODEL),
        out_shape=jax.ShapeDtypeStruct(x.shape, F32),
        scratch_shapes=[pltpu.VMEM((SUBLANES, 2 * D_FF), F32)],
        compiler_params=_params(("arbitrary", "arbitrary")),
        name="post",
    )(x, o, gya, sgb, p, w_b16, w_o16, row(g_mix_post), row(g_ffn_pre),
      w_up16, conv_ffn_w.astype(F32), row(b_ffn_conv), w_down16,
      row(g_ffn_post), w_proj16, w_gate16, row(g_ple_post))
    return h3


def kernel(x, p, positions, g_mix_pre, w_in, conv_a_w, w_a_out, g_q_lat, w_q_up, g_kv_lat, w_kv_up, w_b_out, w_o, g_mix_post, g_ffn_pre, w_ffn_up, conv_ffn_w, b_ffn_conv, w_ffn_down, g_ffn_post, w_ple_proj, w_ple_gate, g_ple_post):
    h = x
    for i in range(g_mix_pre.shape[0]):
        h = _layer(h, p[i], positions, g_mix_pre[i], w_in[i], conv_a_w[i], w_a_out[i], g_q_lat[i],
                   w_q_up[i], g_kv_lat[i], w_kv_up[i], w_b_out[i], w_o[i], g_mix_post[i],
                   g_ffn_pre[i], w_ffn_up[i], conv_ffn_w[i], b_ffn_conv[i], w_ffn_down[i],
                   g_ffn_post[i], w_ple_proj[i], w_ple_gate[i], g_ple_post[i])
    return h
```

```python
import functools
import math

import jax
import jax.numpy as jnp
from jax.experimental import pallas as pl
from jax.experimental.pallas import tpu as pltpu

D_MODEL = 1024
CHUNK = 64
CONV_DIM = 512
N_HEADS = 8
NOPE_DIM = 64
ROPE_DIM = 32
V_DIM = 64
Q_LORA = 384
KV_LORA = 256
ROPE_THETA = 10000.0
D_FF = 2816
PLE_DIM = 256
EPS = 1e-6
QK_DIM = NOPE_DIM + ROPE_DIM

LANES = 128
SUBLANES = 8
BF16_SUBLANES = 16
N_LATER_WEIGHTS = 6
HEAD_SLOT = LANES
HP = N_HEADS * HEAD_SLOT
LAT_COLS = Q_LORA + KV_LORA + LANES
IN_LAT = 3 * CONV_DIM
IN_GATE_A = IN_LAT + Q_LORA + KV_LORA + ROPE_DIM
IN_GATE_B = IN_GATE_A + D_MODEL
N_IN = IN_GATE_B + D_MODEL
VMEM_LIMIT = 56 * 1024 * 1024

TS_MIX = 512
TQ = 1024
TKEY = TQ // 2
HEADS_PER_STEP = 2
assert HEADS_PER_STEP * V_DIM == LANES
V_ROWS = pl.cdiv(V_DIM + 1, BF16_SUBLANES) * BF16_SUBLANES
VT_ROWS = N_HEADS * V_ROWS
TS_POST = 512

GELU_C = math.sqrt(2.0 / math.pi)
NEG_BIG = -1e30
BF16 = jnp.bfloat16
F32 = jnp.float32


def _rms(x, g):
    return x * jax.lax.rsqrt(jnp.mean(x * x, axis=-1, keepdims=True) + EPS) * g


def _dot(a, b):
    return jnp.dot(a, b, preferred_element_type=F32)


def _const_spec(shape):
    return pl.BlockSpec(shape, lambda *_: (0,) * len(shape))


def _shifted_rows(carry_ref, pre):
    rows = pre.shape[0]
    ext = jnp.concatenate([carry_ref[...], pre], axis=0)
    carry_ref[...] = pre[rows - SUBLANES:rows, :]
    return pltpu.roll(ext, 1, 0)[SUBLANES:], pltpu.roll(ext, 2, 0)[SUBLANES:]


def _mix_in_kernel(x_ref, pos_ref, invf_ref, g_pre_ref, w_in_t_ref,
                   conv_w_ref, w_a_out_ref, g_q_ref, w_q_ref,
                   g_kv_ref, w_k_ref, w_v_ref, vone_ref, *rest, q_scale):
    later_f32 = rest[:N_LATER_WEIGHTS]
    q_ref, k_ref, v_ref, gya_ref, sgb_ref = rest[N_LATER_WEIGHTS:N_LATER_WEIGHTS + 5]
    later_bf16 = rest[N_LATER_WEIGHTS + 5:-1]
    carry_ref = rest[-1]
    ts = x_ref.shape[0]

    @pl.when(pl.program_id(1) == 0)
    def _():
        carry_ref[...] = jnp.zeros_like(carry_ref)

    u = _rms(x_ref[...], g_pre_ref[...]).astype(BF16)

    proj = lambda lo, hi: jax.lax.dot_general(u, w_in_t_ref[lo:hi, :].astype(BF16), (((1,), (1,)), ((), ())),
                                              preferred_element_type=F32)
    lat = proj(IN_LAT, IN_LAT + LAT_COLS)
    abc = proj(0, IN_LAT)
    ga = proj(IN_GATE_A, IN_GATE_B)
    gb = proj(IN_GATE_B, N_IN)
    q_lat = lat[:, 0:Q_LORA]
    kv_lat = lat[:, Q_LORA:Q_LORA + KV_LORA]
    lane = jax.lax.broadcasted_iota(jnp.int32, (ts, HEAD_SLOT), 1)
    kr = jnp.where((lane >= NOPE_DIM) & (lane < QK_DIM),
                   pltpu.roll(lat[:, Q_LORA + KV_LORA:LAT_COLS], NOPE_DIM, 1), 0.0)

    half = ROPE_DIM // 2
    ang_t = invf_ref[...] * pos_ref[...].astype(F32)
    sin_t = jnp.sin(ang_t)
    rows = lambda n: jnp.zeros((n, ts), F32)
    tail = HEAD_SLOT - QK_DIM
    cos = jnp.concatenate([jnp.ones((NOPE_DIM, ts), F32), jnp.cos(ang_t), rows(tail)], axis=0).T
    sin_dn = jnp.concatenate([rows(NOPE_DIM), -sin_t[0:half], rows(half + tail)], axis=0).T
    sin_up = jnp.concatenate([rows(NOPE_DIM + half), sin_t[half:], rows(tail)], axis=0).T

    def rope(t, c, s_dn, s_up):
        return (t * c + pltpu.roll(t, HEAD_SLOT - half, 1) * s_dn + pltpu.roll(t, half, 1) * s_up)

    qn = _rms(q_lat, g_q_ref[...]).astype(BF16)
    qq = _dot(qn, w_q_ref[...])
    kvn = _rms(kv_lat, g_kv_ref[...]).astype(BF16)
    kk = _dot(kvn, w_k_ref[...])
    vt = jax.lax.dot_general(w_v_ref[...], kvn, (((1,), (1,)), ((), ())),
                             preferred_element_type=F32) + vone_ref[...]
    v_ref[...] = vt.astype(BF16)

    k_pe = rope(kr, cos, sin_dn, sin_up)
    q_tables = (cos * q_scale, sin_dn * q_scale, sin_up * q_scale)
    for h in range(N_HEADS):
        sl = slice(h * HEAD_SLOT, (h + 1) * HEAD_SLOT)
        q_ref[:, sl] = rope(qq[:, sl], *q_tables).astype(BF16)
        k_ref[:, sl] = (kk[:, sl] + k_pe).astype(BF16)

    a_b = abc[:, 0:CONV_DIM]
    ca = abc[:, CONV_DIM:2 * CONV_DIM] * abc[:, 2 * CONV_DIM:3 * CONV_DIM]
    x1, x2 = _shifted_rows(carry_ref, ca)
    cw = conv_w_ref[...]
    cv = ca * cw[2:3, :] + x1 * cw[1:2, :] + x2 * cw[0:1, :]
    ya = _dot((a_b * cv).astype(BF16), w_a_out_ref[...])
    gya_ref[...] = (jax.nn.sigmoid(ga) * ya).astype(BF16)
    sgb_ref[...] = jax.nn.sigmoid(gb).astype(BF16)

    for src, dst in zip(later_f32, later_bf16):
        dst[...] = src[...].astype(BF16)


def _attn_kernel(q_ref, k_ref, vt_ref, o_ref, s_ref, smax_ref, m_ref, acc_ref, qt_ref):
    tk, tq = s_ref.shape[-2:]
    assert tq == 2 * tk
    nq = q_ref.shape[0] // tq
    heads = range(HEADS_PER_STEP)
    lanes = lambda h: slice(h * HEAD_SLOT, (h + 1) * HEAD_SLOT)
    v_rows = lambda h: slice(h * V_ROWS, (h + 1) * V_ROWS)
    key_tile = lambda j: pl.ds(pl.multiple_of(j * tk, tk), tk)
    every = slice(0, tq)
    later = slice(tk, tq)

    def transpose_queries(qi):
        for h in heads:
            qt_ref[qi % 2, h] = q_ref[pl.ds(pl.multiple_of(qi * tq, tq), tq), lanes(h)].T

    def scores(qi, j, buf, hs=heads):
        for h in hs:
            sc = _dot(k_ref[key_tile(j), lanes(h)], qt_ref[qi % 2, h])
            s_ref[buf, h] = sc
            smax_ref[buf, h] = jnp.max(sc, axis=0, keepdims=True)

    def scores_later(qi, buf, hs=heads):
        for h in hs:
            s_ref[buf, h, :, later] = _dot(k_ref[key_tile(2 * qi + 1), lanes(h)],
                                           qt_ref[qi % 2, h, :, later])

    def accumulate(j, buf, cols, masked, hs=heads):
        width = cols.stop - cols.start
        if masked:
            ck = jax.lax.broadcasted_iota(jnp.int32, (tk, width), 0) // CHUNK
            cq = jax.lax.broadcasted_iota(jnp.int32, (tk, width), 1) // CHUNK
            load = lambda h: jnp.where(ck <= cq, s_ref[buf, h, :, cols], NEG_BIG)
            tile_max = lambda h: jnp.max(load(h), axis=0, keepdims=True)
        else:
            load = lambda h: s_ref[buf, h, :, cols]
            tile_max = lambda h: smax_ref[buf, h, :, cols]
        for h in hs:
            m = m_ref[h, :, cols]
            m_new = jnp.maximum(m, tile_max(h))
            p = jnp.exp2(load(h) - m_new)
            pv = _dot(vt_ref[v_rows(h), key_tile(j)], p.astype(BF16))
            acc_ref[h, 0:V_ROWS, cols] = acc_ref[h, 0:V_ROWS, cols] * jnp.exp2(m - m_new) + pv
            m_ref[h, :, cols] = m_new

    def query_tile(qi, _):
        next_qi = jnp.minimum(qi + 1, nq - 1)
        m_ref[...] = jnp.full_like(m_ref, NEG_BIG)
        acc_ref[...] = jnp.zeros_like(acc_ref)

        @pl.when(qi + 1 < nq)
        def _():
            transpose_queries(qi + 1)

        def tile_pair(i):
            for h in heads:
                scores(qi, 2 * i + 1, 1, (h,))
                accumulate(2 * i, 0, every, False, (h,))
            for h in heads:
                scores(qi, 2 * i + 2, 0, (h,))
                accumulate(2 * i + 1, 1, every, False, (h,))

        def two_pairs(i, _):
            tile_pair(2 * i)
            tile_pair(2 * i + 1)
            return 0

        jax.lax.fori_loop(0, qi // 2, two_pairs, 0)

        @pl.when(qi % 2 == 1)
        def _():
            tile_pair(qi - 1)

        for h in heads:
            scores_later(qi, 1, (h,))
            accumulate(2 * qi, 0, every, True, (h,))
        for h in heads:
            scores(next_qi, 0, 0, (h,))
            accumulate(2 * qi + 1, 1, later, True, (h,))

        out = jnp.concatenate([acc_ref[h, 0:V_DIM, :] / acc_ref[h, V_DIM:V_DIM + 1, :]
                               for h in heads], axis=0)
        o_ref[pl.ds(pl.multiple_of(qi * tq, tq), tq), :] = out.T.astype(BF16)
        return 0

    transpose_queries(0)
    scores(0, 0, 0)
    jax.lax.fori_loop(0, nq, query_tile, 0)


def _post_kernel(x_ref, o_ref, gya_ref, sgb_ref, p_ref, w_b_ref, w_o_ref, g_mix_ref,
                 g_pre_ref, w_up_ref, conv_w_ref, b_ref, w_down_ref, g_post_ref,
                 w_proj_ref, w_gate_ref, g_ple_ref, out_ref, carry_ref):
    @pl.when(pl.program_id(1) == 0)
    def _():
        carry_ref[...] = jnp.zeros_like(carry_ref)

    yb = _dot(o_ref[...], w_b_ref[...])
    mixed = gya_ref[...].astype(F32) + sgb_ref[...].astype(F32) * yb
    h = x_ref[...] + _rms(_dot(mixed.astype(BF16), w_o_ref[...]), g_mix_ref[...])

    u = _rms(h, g_pre_ref[...]).astype(BF16)

    pre = _dot(u, w_up_ref[...])
    x1, x2 = _shifted_rows(carry_ref, pre)
    cw = conv_w_ref[...]
    up = (pre * cw[2:3, :] + b_ref[...]) + x1 * cw[1:2, :] + x2 * cw[0:1, :]
    gate = up[:, 0:D_FF]
    half_gate = 0.5 * gate
    act = half_gate + half_gate * jnp.tanh(gate * (GELU_C + (GELU_C * 0.044715) * (gate * gate)))
    f = _dot((act * up[:, D_FF:]).astype(BF16), w_down_ref[...])
    h = h + _rms(f, g_post_ref[...])

    e = _dot(p_ref[...].astype(BF16), w_proj_ref[...])
    hb = h.astype(BF16)
    half_rows = h.shape[0] // 2
    for r in (slice(0, half_rows), slice(half_rows, 2 * half_rows)):
        g = jax.nn.sigmoid(_dot(hb[r], w_gate_ref[...]))
        out_ref[r, :] = h[r] + _rms(e[r] * g, g_ple_ref[...])


def _row_spec(ts, width):
    return pl.BlockSpec((None, ts, width), lambda b, s: (b, s, 0))


def _params(sem):
    return pltpu.CompilerParams(dimension_semantics=sem, vmem_limit_bytes=VMEM_LIMIT)


def _layer(x, p, positions, g_mix_pre, w_in, conv_a_w, w_a_out, g_q_lat, w_q_up, g_kv_lat,
           w_kv_up, w_b_out, w_o, g_mix_post, g_ffn_pre, w_ffn_up, conv_ffn_w, b_ffn_conv,
           w_ffn_down, g_ffn_post, w_ple_proj, w_ple_gate, g_ple_post):
    bsz, seq, _ = x.shape
    row = lambda v: v.reshape(1, -1).astype(F32)

    half = ROPE_DIM // 2
    assert w_in.shape == (D_MODEL, N_IN)
    w_in_t = w_in.T

    wq = w_q_up.reshape(Q_LORA, N_HEADS, QK_DIM)
    w_q = jnp.pad(wq, ((0, 0), (0, 0), (0, HEAD_SLOT - QK_DIM))).reshape(Q_LORA, HP).astype(BF16)

    wkv = w_kv_up.reshape(KV_LORA, N_HEADS, NOPE_DIM + V_DIM)
    w_k = jnp.pad(wkv[:, :, :NOPE_DIM], ((0, 0), (0, 0), (0, HEAD_SLOT - NOPE_DIM))).reshape(KV_LORA, HP).astype(BF16)
    w_vt = jnp.pad(wkv[:, :, NOPE_DIM:], ((0, 0), (0, 0), (0, V_ROWS - V_DIM))).reshape(KV_LORA, VT_ROWS).T.astype(BF16)
    vone = jnp.tile((jnp.arange(V_ROWS) == V_DIM).astype(F32), N_HEADS).reshape(VT_ROWS, 1)

    inv = 1.0 / (ROPE_THETA ** (jnp.arange(half, dtype=F32) * (2.0 / ROPE_DIM)))
    invf = jnp.concatenate([inv, inv]).reshape(ROPE_DIM, 1)
    pos3 = positions.reshape(bsz, 1, seq)
    q_scale = (QK_DIM ** -0.5) * math.log2(math.e)

    act = jax.ShapeDtypeStruct((bsz, seq, HP), BF16)
    steps_per_row = seq // TS_MIX
    later = (w_b_out, w_o, w_ffn_up, w_ffn_down, w_ple_proj, w_ple_gate)
    assert len(later) == N_LATER_WEIGHTS

    def later_spec(w):
        rows, cols = w.shape
        for col_blocks in (1, 2, 4, 8):
            row_blocks = bsz * steps_per_row // col_blocks
            if rows % (row_blocks * BF16_SUBLANES) == 0 and cols % (col_blocks * LANES) == 0:
                break
        else:
            raise ValueError(f"no tile-aligned split of {w.shape} over the mix_in grid")
        return pl.BlockSpec(
            (rows // row_blocks, cols // col_blocks),
            lambda b, s: ((b * steps_per_row + s) // col_blocks, (b * steps_per_row + s) % col_blocks))

    later_specs = [later_spec(w) for w in later]
    q, k, vt, gya, sgb, w_b16, w_o16, w_up16, w_down16, w_proj16, w_gate16 = pl.pallas_call(
        functools.partial(_mix_in_kernel, q_scale=q_scale),
        grid=(bsz, seq // TS_MIX),
        in_specs=[_row_spec(TS_MIX, D_MODEL),
                  pl.BlockSpec((None, 1, TS_MIX), lambda b, s: (b, 0, s)),
                  _const_spec((ROPE_DIM, 1)), _const_spec((1, D_MODEL)),
                  _const_spec((N_IN, D_MODEL)),
                  _const_spec((3, CONV_DIM)), _const_spec((CONV_DIM, D_MODEL)),
                  _const_spec((1, Q_LORA)), _const_spec((Q_LORA, HP)),
                  _const_spec((1, KV_LORA)), _const_spec((KV_LORA, HP)),
                  _const_spec((VT_ROWS, KV_LORA)), _const_spec((VT_ROWS, 1))] + later_specs,
        out_specs=[_row_spec(TS_MIX, HP), _row_spec(TS_MIX, HP),
                   pl.BlockSpec((None, VT_ROWS, TS_MIX), lambda b, s: (b, 0, s)),
                   _row_spec(TS_MIX, HP), _row_spec(TS_MIX, HP)] + later_specs,
        out_shape=[act, act, jax.ShapeDtypeStruct((bsz, VT_ROWS, seq), BF16), act, act]
        + [jax.ShapeDtypeStruct(w.shape, BF16) for w in later],
        scratch_shapes=[pltpu.VMEM((SUBLANES, CONV_DIM), F32)],
        compiler_params=_params(("arbitrary", "arbitrary")),
        name="mix_in",
    )(x, pos3, invf, row(g_mix_pre), w_in_t, conv_a_w.astype(F32),
      w_a_out.astype(BF16), row(g_q_lat), w_q, row(g_kv_lat), w_k, w_vt, vone, *later)

    group = HEADS_PER_STEP * HEAD_SLOT
    head_cols = pl.BlockSpec((None, seq, group), lambda b, h: (b, 0, h))
    o = pl.pallas_call(
        _attn_kernel,
        grid=(bsz, N_HEADS // HEADS_PER_STEP),
        in_specs=[head_cols, head_cols,
                  pl.BlockSpec((None, HEADS_PER_STEP * V_ROWS, seq), lambda b, h: (b, h, 0))],
        out_specs=pl.BlockSpec((None, seq, HEADS_PER_STEP * V_DIM), lambda b, h: (b, 0, h)),
        out_shape=jax.ShapeDtypeStruct((bsz, seq, N_HEADS * V_DIM), BF16),
        scratch_shapes=[pltpu.VMEM((2, HEADS_PER_STEP, TKEY, TQ), F32),
                        pltpu.VMEM((2, HEADS_PER_STEP, 1, TQ), F32),
                        pltpu.VMEM((HEADS_PER_STEP, 1, TQ), F32),
                        pltpu.VMEM((HEADS_PER_STEP, HEAD_SLOT, TQ), F32),
                        pltpu.VMEM((2, HEADS_PER_STEP, HEAD_SLOT, TQ), BF16)],
        compiler_params=_params(("arbitrary", "arbitrary")),
        name="attn",
    )(q, k, vt)

    h3 = pl.pallas_call(
        _post_kernel,
        grid=(bsz, seq // TS_POST),
        in_specs=[_row_spec(TS_POST, D_MODEL), _row_spec(TS_POST, N_HEADS * V_DIM),
                  _row_spec(TS_POST, HP), _row_spec(TS_POST, HP), _row_spec(TS_POST, PLE_DIM),
                  _const_spec((N_HEADS * V_DIM, D_MODEL)), _const_spec((D_MODEL, D_MODEL)),
                  _const_spec((1, D_MODEL)), _const_spec((1, D_MODEL)),
                  _const_spec((D_MODEL, 2 * D_FF)), _const_spec((3, 2 * D_FF)),
                  _const_spec((1, 2 * D_FF)), _const_spec((D_FF, D_MODEL)),
                  _const_spec((1, D_MODEL)), _const_spec((PLE_DIM, D_MODEL)),
                  _const_spec((D_MODEL, D_MODEL)), _const_spec((1, D_MODEL))],
        out_specs=_row_spec(TS_POST, D_MODEL),
        out_shape=jax.ShapeDtypeStruct(x.shape, F32),
        scratch_shapes=[pltpu.VMEM((SUBLANES, 2 * D_FF), F32)],
        compiler_params=_params(("arbitrary", "arbitrary")),
        name="post",
    )(x, o, gya, sgb, p, w_b16, w_o16, row(g_mix_post), row(g_ffn_pre),
      w_up16, conv_ffn_w.astype(F32), row(b_ffn_conv), w_down16,
      row(g_ffn_post), w_proj16, w_gate16, row(g_ple_post))
    return h3


def kernel(x, p, positions, g_mix_pre, w_in, conv_a_w, w_a_out, g_q_lat, w_q_up, g_kv_lat, w_kv_up, w_b_out, w_o, g_mix_post, g_ffn_pre, w_ffn_up, conv_ffn_w, b_ffn_conv, w_ffn_down, g_ffn_post, w_ple_proj, w_ple_gate, g_ple_post):
    h = x
    for i in range(g_mix_pre.shape[0]):
        h = _layer(h, p[i], positions, g_mix_pre[i], w_in[i], conv_a_w[i], w_a_out[i], g_q_lat[i],
                   w_q_up[i], g_kv_lat[i], w_kv_up[i], w_b_out[i], w_o[i], g_mix_post[i],
                   g_ffn_pre[i], w_ffn_up[i], conv_ffn_w[i], b_ffn_conv[i], w_ffn_down[i],
                   g_ffn_post[i], w_ple_proj[i], w_ple_gate[i], g_ple_post[i])
    return h
```
